```python
import jax, jax.numpy as jnp
from jax import lax
import numpy as np

D_MODEL = 2048
BATCH = 4
SEQ = 2048
DEPTH = 1
DEC_BATCH = 128
DEC_SEQ = 1
PAST_LEN = 2048
PAGE_SIZE = 128

RWKV_HEADS = 16
RWKV_HEAD_DIM = 64
D_RWKV = RWKV_HEADS * RWKV_HEAD_DIM
D_DECAY_LORA = 64
D_ICLR_LORA = 64
D_GATE_LORA = 160
D_RWKV_PROJ = 3 * D_RWKV + D_DECAY_LORA + D_ICLR_LORA + D_GATE_LORA
RWKV_LN_EPS = 64e-5

ATT_HEAD_DIM = 64
ATT_GROUPS = ((128, 1), (512, 4), (2048, 16))
HEADS_PER_GROUP = 4
ATT_HEADS = HEADS_PER_GROUP * len(ATT_GROUPS)
D_ATT = ATT_HEADS * ATT_HEAD_DIM
ROPE_THETA = 10000.0

D_FF = 5632

N_IN = D_RWKV_PROJ + 3 * D_ATT + 2 * D_MODEL
NORM_EPS = 1e-6
NEG_INF = -1e30

kernel_name = "hybrid_rwkv7_dilated_attn_macaron_step"


def f32(t):
    return t.astype(jnp.float32)


def rms_norm(x, g):
    xf = f32(x)
    y = xf * lax.rsqrt(jnp.mean(xf * xf, axis=-1, keepdims=True) + NORM_EPS)
    return (y * f32(g)).astype(x.dtype)


def swiglu_half_ffn(x, norm_g, w_in, w_out):
    h = rms_norm(x, norm_g)
    gate, up = jnp.split(h @ w_in, 2, axis=-1)
    return 0.5 * ((jax.nn.silu(gate) * up) @ w_out)


def rotary(x, positions):
    half = x.shape[-1] // 2
    inv_freq = ROPE_THETA ** (-jnp.arange(half, dtype=jnp.float32) / half)
    ang = f32(positions)[:, None] * inv_freq[None, :]
    cos = jnp.cos(ang)[None, :, None, :]
    sin = jnp.sin(ang)[None, :, None, :]
    x1, x2 = jnp.split(f32(x), 2, axis=-1)
    return jnp.concatenate([x1 * cos - x2 * sin, x2 * cos + x1 * sin], axis=-1).astype(x.dtype)


def rwkv7_recurrence(r, w, k, v, kk, a, state0):
    def step(S, inp):
        r_t, w_t, k_t, v_t, kk_t, a_t = inp
        sa = jnp.einsum('bhvk,bhk->bhv', S, -kk_t)
        S = (S * w_t[:, :, None, :] + sa[..., None] * (kk_t * a_t)[:, :, None, :]
             + v_t[..., None] * k_t[:, :, None, :])
        return S, jnp.einsum('bhvk,bhk->bhv', S, r_t)
    xs = tuple(jnp.moveaxis(t, 1, 0) for t in (r, w, k, v, kk, a))
    S, ys = lax.scan(step, state0, xs)
    return jnp.moveaxis(ys, 0, 1), S


def rwkv7_mixer(p, shift_prev, wkv_prev, mu, w0, w2, a0, a2, g2, k_k, k_a, r_k, ln_w, ln_b):
    B, T, _ = p.shape
    pf = f32(p)
    p_prev = jnp.concatenate([f32(shift_prev)[:, None], pf[:, :-1]], axis=1)
    pm = pf + (p_prev - pf) * f32(mu)
    r, k, v, w_low, a_low, g_low = jnp.split(
        pm, [D_RWKV, 2 * D_RWKV, 3 * D_RWKV, 3 * D_RWKV + D_DECAY_LORA,
             3 * D_RWKV + D_DECAY_LORA + D_ICLR_LORA], axis=-1)
    w_log = -jax.nn.softplus(-(f32(w0) + jnp.tanh(w_low) @ f32(w2))) - 0.5
    decay = jnp.exp(-jnp.exp(w_log))
    a = jax.nn.sigmoid(f32(a0) + a_low @ f32(a2))
    g = jax.nn.sigmoid(g_low) @ f32(g2)
    kk = k * f32(k_k)
    k = k * (1.0 + (a - 1.0) * f32(k_a))

    def heads(t):
        return t.reshape(B, T, RWKV_HEADS, RWKV_HEAD_DIM)
    r, k, v, decay, a, kk = map(heads, (r, k, v, decay, a, kk))
    kk = kk / jnp.maximum(jnp.sqrt(jnp.sum(kk * kk, axis=-1, keepdims=True)), 1e-12)
    y, S = rwkv7_recurrence(r, decay, k, v, kk, a, f32(wkv_prev))
    mean = jnp.mean(y, axis=-1, keepdims=True)
    var = jnp.mean(jnp.square(y - mean), axis=-1, keepdims=True)
    yn = ((y - mean) * lax.rsqrt(var + RWKV_LN_EPS)).reshape(B, T, D_RWKV) * f32(ln_w) + f32(ln_b)
    bonus = (jnp.sum(r * k * f32(r_k), axis=-1, keepdims=True) * v).reshape(B, T, D_RWKV)
    out = (yn + bonus) * g
    return out.astype(p.dtype), p[:, -1], S.astype(p.dtype)


def dilated_group_attention(q, k_ctx, v_ctx, n_past, window, dilation):
    T = q.shape[1]
    n_keys = window // dilation + 1
    idx = n_past + jnp.arange(T)[:, None] - dilation * jnp.arange(n_keys)[None, :]
    valid = idx >= 0
    idx = jnp.maximum(idx, 0)
    kg = jnp.take(k_ctx, idx, axis=1)
    vg = jnp.take(v_ctx, idx, axis=1)
    s = f32(jnp.einsum('bthd,btjhd->bthj', q, kg)) * (ATT_HEAD_DIM ** -0.5)
    s = jnp.where(valid[None, :, None, :], s, NEG_INF)
    lse = jax.nn.logsumexp(s, axis=-1)
    p = jnp.exp(s - lse[..., None])
    o = jnp.einsum('bthj,btjhd->bthd', p.astype(v_ctx.dtype), vg)
    return o, lse


def dilated_attention(q, k, v, positions, kv_pasts, q_norm, k_norm):
    B, T = q.shape[:2]
    q = rotary(rms_norm(q, q_norm), positions)
    k = rotary(rms_norm(k, k_norm), positions)
    outs, lses, new_bufs = [], [], []
    for g, (window, dilation) in enumerate(ATT_GROUPS):
        hs = slice(g * HEADS_PER_GROUP, (g + 1) * HEADS_PER_GROUP)
        kv_new = jnp.stack([k[:, :, hs], v[:, :, hs]], axis=2)
        ctx = jnp.concatenate([kv_pasts[g].astype(kv_new.dtype), kv_new], axis=1)
        n_past = kv_pasts[g].shape[1]
        o, lse = dilated_group_attention(q[:, :, hs], ctx[:, :, 0], ctx[:, :, 1], n_past, window, dilation)
        outs.append(o)
        lses.append(lse)
        keep = min(window, ctx.shape[1])
        new_bufs.append(ctx[:, ctx.shape[1] - keep:])
    alpha = jax.nn.softmax(jnp.stack(lses, axis=0), axis=0)
    o = jnp.concatenate([outs[g] * alpha[g][..., None].astype(outs[g].dtype)
                         for g in range(len(ATT_GROUPS))], axis=2)
    return o.reshape(B, T, D_ATT), new_bufs


def hybrid_layer(x, positions, shift_prev, wkv_prev, kv_pasts, params):
    (ffn1_norm, ffn1_w_in, ffn1_w_out, mix_norm, w_in, b_gate,
     rwkv_mu, rwkv_w0, rwkv_w2, rwkv_a0, rwkv_a2, rwkv_g2, rwkv_k_k, rwkv_k_a, rwkv_r_k,
     rwkv_ln_w, rwkv_ln_b, w_branch_rwkv, attn_q_norm, attn_k_norm, w_branch_attn, w_out,
     ffn2_norm, ffn2_w_in, ffn2_w_out) = params
    B, T, _ = x.shape
    x = x + swiglu_half_ffn(x, ffn1_norm, ffn1_w_in, ffn1_w_out)
    h = rms_norm(x, mix_norm)
    proj = h @ w_in
    p_rwkv, p_att, p_gate = jnp.split(proj, [D_RWKV_PROJ, D_RWKV_PROJ + 3 * D_ATT], axis=-1)
    o_a, shift_new, wkv_new = rwkv7_mixer(p_rwkv, shift_prev, wkv_prev, rwkv_mu, rwkv_w0, rwkv_w2,
                                          rwkv_a0, rwkv_a2, rwkv_g2, rwkv_k_k, rwkv_k_a, rwkv_r_k,
                                          rwkv_ln_w, rwkv_ln_b)
    q, k, v = (t.reshape(B, T, ATT_HEADS, ATT_HEAD_DIM) for t in jnp.split(p_att, 3, axis=-1))
    o_b, new_bufs = dilated_attention(q, k, v, positions, kv_pasts, attn_q_norm, attn_k_norm)
    g_a, g_b = jnp.split(jax.nn.sigmoid(p_gate + b_gate), 2, axis=-1)
    merged = g_a * (o_a @ w_branch_rwkv) + g_b * (o_b @ w_branch_attn)
    x = x + merged @ w_out
    x = x + swiglu_half_ffn(x, ffn2_norm, ffn2_w_in, ffn2_w_out)
    return x, shift_new, wkv_new, new_bufs


def setup_inputs(seed: int = 0) -> dict:
    key = jax.random.key(seed)
    nk = iter(jax.random.split(key, 48))

    def normal(shape, scale=1.0):
        return jax.random.normal(next(nk), shape, jnp.float32) * scale

    def gain(shape):
        return 1.0 + normal(shape, 0.1)

    L = DEPTH
    win_lens = [min(w, PAST_LEN) for w, _ in ATT_GROUPS]
    return {
        "x_prompt": normal((BATCH, SEQ, D_MODEL)),
        "x_sample": normal((DEC_BATCH, DEC_SEQ, D_MODEL)),
        "state_rwkv_shift": normal((L, DEC_BATCH, D_RWKV_PROJ)),
        "state_rwkv_wkv": normal((L, DEC_BATCH, RWKV_HEADS, RWKV_HEAD_DIM, RWKV_HEAD_DIM), 0.3),
        "cache_att_w128": normal((L, DEC_BATCH, win_lens[0], 2, HEADS_PER_GROUP, ATT_HEAD_DIM)),
        "cache_att_w512": normal((L, DEC_BATCH, win_lens[1], 2, HEADS_PER_GROUP, ATT_HEAD_DIM)),
        "cache_att_w2048": normal((L, DEC_BATCH, win_lens[2], 2, HEADS_PER_GROUP, ATT_HEAD_DIM)),
        "ffn1_norm": gain((L, D_MODEL)),
        "ffn1_w_in": normal((L, D_MODEL, 2 * D_FF), D_MODEL ** -0.5),
        "ffn1_w_out": normal((L, D_FF, D_MODEL), D_FF ** -0.5),
        "mix_norm": gain((L, D_MODEL)),
        "w_in": normal((L, D_MODEL, N_IN), D_MODEL ** -0.5),
        "b_gate": normal((L, 2 * D_MODEL), 0.1),
        "rwkv_mu": jax.random.uniform(next(nk), (L, D_RWKV_PROJ), jnp.float32, 0.0, 1.0),
        "rwkv_w0": jax.random.uniform(next(nk), (L, D_RWKV), jnp.float32, -6.0, -1.0),
        "rwkv_w2": normal((L, D_DECAY_LORA, D_RWKV), 0.1 * D_DECAY_LORA ** -0.5),
        "rwkv_a0": normal((L, D_RWKV), 0.1),
        "rwkv_a2": normal((L, D_ICLR_LORA, D_RWKV), D_ICLR_LORA ** -0.5),
        "rwkv_g2": normal((L, D_GATE_LORA, D_RWKV), D_GATE_LORA ** -0.5),
        "rwkv_k_k": 0.85 + normal((L, D_RWKV), 0.05),
        "rwkv_k_a": 1.0 + normal((L, D_RWKV), 0.05),
        "rwkv_r_k": normal((L, RWKV_HEADS, RWKV_HEAD_DIM), 0.1),
        "rwkv_ln_w": gain((L, D_RWKV)),
        "rwkv_ln_b": normal((L, D_RWKV), 0.02),
        "w_branch_rwkv": normal((L, D_RWKV, D_MODEL), D_RWKV ** -0.5),
        "attn_q_norm": gain((L, ATT_HEAD_DIM)),
        "attn_k_norm": gain((L, ATT_HEAD_DIM)),
        "w_branch_attn": normal((L, D_ATT, D_MODEL), D_ATT ** -0.5),
        "w_out": normal((L, D_MODEL, D_MODEL), D_MODEL ** -0.5),
        "ffn2_norm": gain((L, D_MODEL)),
        "ffn2_w_in": normal((L, D_MODEL, 2 * D_FF), D_MODEL ** -0.5),
        "ffn2_w_out": normal((L, D_FF, D_MODEL), D_FF ** -0.5),
    }


def reference(x_prompt, x_sample, state_rwkv_shift, state_rwkv_wkv, cache_att_w128, cache_att_w512,
              cache_att_w2048, ffn1_norm, ffn1_w_in, ffn1_w_out, mix_norm, w_in, b_gate, rwkv_mu, rwkv_w0,
              rwkv_w2, rwkv_a0, rwkv_a2, rwkv_g2, rwkv_k_k, rwkv_k_a, rwkv_r_k, rwkv_ln_w, rwkv_ln_b,
              w_branch_rwkv, attn_q_norm, attn_k_norm, w_branch_attn, w_out, ffn2_norm, ffn2_w_in, ffn2_w_out):
    params = (ffn1_norm, ffn1_w_in, ffn1_w_out, mix_norm, w_in, b_gate,
              rwkv_mu, rwkv_w0, rwkv_w2, rwkv_a0, rwkv_a2, rwkv_g2, rwkv_k_k, rwkv_k_a, rwkv_r_k,
              rwkv_ln_w, rwkv_ln_b, w_branch_rwkv, attn_q_norm, attn_k_norm, w_branch_attn, w_out,
              ffn2_norm, ffn2_w_in, ffn2_w_out)
    pos_prompt = jnp.arange(SEQ)
    pos_sample = PAST_LEN + jnp.arange(DEC_SEQ)
    y_p, y_s = x_prompt, x_sample
    p_shift, p_wkv, s_shift, s_wkv = [], [], [], []
    p_kv = [[] for _ in ATT_GROUPS]
    s_kv = [[] for _ in ATT_GROUPS]
    for l in range(DEPTH):
        lp = tuple(w[l] for w in params)
        empty_kv = tuple(jnp.zeros((BATCH, 0, 2, HEADS_PER_GROUP, ATT_HEAD_DIM), x_prompt.dtype)
                         for _ in ATT_GROUPS)
        y_p, sh, wk, bufs = hybrid_layer(
            y_p, pos_prompt, jnp.zeros((BATCH, D_RWKV_PROJ), x_prompt.dtype),
            jnp.zeros((BATCH, RWKV_HEADS, RWKV_HEAD_DIM, RWKV_HEAD_DIM), jnp.float32), empty_kv, lp)
        p_shift.append(sh)
        p_wkv.append(wk)
        for g in range(len(ATT_GROUPS)):
            p_kv[g].append(bufs[g])
        y_s, sh, wk, bufs = hybrid_layer(
            y_s, pos_sample, state_rwkv_shift[l], state_rwkv_wkv[l],
            (cache_att_w128[l], cache_att_w512[l], cache_att_w2048[l]), lp)
        s_shift.append(sh)
        s_wkv.append(wk)
        for g in range(len(ATT_GROUPS)):
            s_kv[g].append(bufs[g])
    return (y_p, y_s,
            jnp.stack(p_shift), jnp.stack(p_wkv),
            jnp.stack(p_kv[0]), jnp.stack(p_kv[1]), jnp.stack(p_kv[2]),
            jnp.stack(s_shift), jnp.stack(s_wkv),
            jnp.stack(s_kv[0]), jnp.stack(s_kv[1]), jnp.stack(s_kv[2]))
```

```python
import functools

import jax
import jax.numpy as jnp
from jax import lax
from jax.experimental import pallas as pl
from jax.experimental.pallas import tpu as pltpu

D_MODEL = 2048
BATCH = 4
SEQ = 2048
DEC_BATCH = 128
PAST_LEN = 2048
M_PROMPT = BATCH * SEQ
M_ALL = M_PROMPT + DEC_BATCH

RWKV_HEADS = 16
HEAD_DIM = 64
D_RWKV = RWKV_HEADS * HEAD_DIM
D_DECAY_LORA = 64
D_ICLR_LORA = 64
D_GATE_LORA = 160
D_RWKV_PROJ = 3 * D_RWKV + D_DECAY_LORA + D_ICLR_LORA + D_GATE_LORA
RWKV_PROJ_PAD = 3584
RWKV_LN_EPS = 64e-5

ATT_GROUPS = ((128, 1), (512, 4), (2048, 16))
HEADS_PER_GROUP = 4
ATT_HEADS = HEADS_PER_GROUP * len(ATT_GROUPS)
D_ATT = ATT_HEADS * HEAD_DIM
D_GROUP = HEADS_PER_GROUP * HEAD_DIM
N_KEYS_PAST = 128
ROPE_THETA = 10000.0
D_FF = 5632
NORM_EPS = 1e-6
NEG_INF = -1e30

LANES = 128
VMEM_LIMIT = 56 * 1024 * 1024

F32 = jnp.float32
BF16 = jnp.bfloat16


def _params(*sem):
    return pltpu.CompilerParams(dimension_semantics=sem, vmem_limit_bytes=VMEM_LIMIT)


def _const_spec(shape):
    nd = len(shape)
    return pl.BlockSpec(shape, lambda *_: (0,) * nd, pipeline_mode=pl.Buffered(1))


def _rms_to_bf16(x, g):
    ms = jnp.mean(x * x, axis=-1, keepdims=True)
    return (x * lax.rsqrt(ms + NORM_EPS) * g).astype(BF16)


def _segsum(x, e):
    hi = x.astype(BF16)
    lo = (x - hi.astype(F32)).astype(BF16)
    return (jnp.dot(hi, e, preferred_element_type=F32) + jnp.dot(lo, e, preferred_element_type=F32))


def _dot_split(x, w_hi, w_lo):
    hi = x.astype(BF16)
    lo = (x - hi.astype(F32)).astype(BF16)
    return (jnp.dot(hi, w_hi, preferred_element_type=F32) + jnp.dot(lo, w_hi, preferred_element_type=F32)
            + jnp.dot(hi, w_lo, preferred_element_type=F32))


FFN_TM = 640
FFN_TF = 512


def _ffn_kernel(x_ref, g_ref, wg_ref, wu_ref, wo_ref, o_ref, h_ref):
    j = pl.program_id(1)

    @pl.when(j == 0)
    def _():
        h_ref[...] = _rms_to_bf16(x_ref[...], g_ref[...])
        o_ref[...] = jnp.zeros_like(o_ref)

    h = h_ref[...]
    gate = jnp.dot(h, wg_ref[...], preferred_element_type=F32)
    up = jnp.dot(h, wu_ref[...], preferred_element_type=F32)
    act = (gate * jax.nn.sigmoid(gate) * up).astype(BF16)
    o_ref[...] += jnp.dot(act, wo_ref[...], preferred_element_type=F32)

    @pl.when(j == pl.num_programs(1) - 1)
    def _():
        o_ref[...] = x_ref[...] + 0.5 * o_ref[...]


def _ffn(x, g, w_in, w_out):
    m = x.shape[0]
    nj = D_FF // FFN_TF
    return pl.pallas_call(
        _ffn_kernel,
        grid=(m // FFN_TM, nj),
        in_specs=[
            pl.BlockSpec((FFN_TM, D_MODEL), lambda i, j: (i, 0)),
            pl.BlockSpec((1, D_MODEL), lambda i, j: (0, 0)),
            pl.BlockSpec((D_MODEL, FFN_TF), lambda i, j: (0, j)),
            pl.BlockSpec((D_MODEL, FFN_TF), lambda i, j: (0, j + nj)),
            pl.BlockSpec((FFN_TF, D_MODEL), lambda i, j: (j, 0)),
        ],
        out_specs=pl.BlockSpec((FFN_TM, D_MODEL), lambda i, j: (i, 0)),
        out_shape=jax.ShapeDtypeStruct((m, D_MODEL), F32),
        scratch_shapes=[pltpu.VMEM((FFN_TM, D_MODEL), BF16)],
        compiler_params=_params("parallel", "arbitrary"),
        name="ffn",
    )(x, g, w_in, w_in, w_out)


PROJ_TM = 640


def _normproj_kernel(x_ref, g_ref, w_ref, o_ref, h_ref):
    @pl.when(pl.program_id(1) == 0)
    def _():
        h_ref[...] = _rms_to_bf16(x_ref[...], g_ref[...])

    o_ref[...] = jnp.dot(h_ref[...], w_ref[...], preferred_element_type=F32)


def _normproj(x, g, w, tn, name):
    m, n = x.shape[0], w.shape[1]
    return pl.pallas_call(
        _normproj_kernel,
        grid=(m // PROJ_TM, n // tn),
        in_specs=[
            pl.BlockSpec((PROJ_TM, D_MODEL), lambda i, j: (i, 0)),
            pl.BlockSpec((1, D_MODEL), lambda i, j: (0, 0)),
            pl.BlockSpec((D_MODEL, tn), lambda i, j: (0, j)),
        ],
        out_specs=pl.BlockSpec((PROJ_TM, tn), lambda i, j: (i, j)),
        out_shape=jax.ShapeDtypeStruct((m, n), F32),
        scratch_shapes=[pltpu.VMEM((PROJ_TM, D_MODEL), BF16)],
        compiler_params=_params("parallel", "arbitrary"),
        name=name,
    )(x, g, w)


PREP_TM = 128
PREP_TILES_PER_SEQ = SEQ // PREP_TM
PREP_SAMPLE_TILE = M_PROMPT // PREP_TM
KW_R, KW_DECAY, KW_K, KW_KK, KW_B = range(5)


def _rwkv_prep_kernel(p_ref, tail_ref, state_ref, mu_ref, w0_ref, a0_ref, kk_ref, ka_ref,
                      wl_hi_ref, wl_lo_ref, wg_hi_ref, wg_lo_ref, e_ref, kw_ref, v_ref, g_ref):
    i = pl.program_id(0)
    p = p_ref[...]
    tail = jnp.where(i % PREP_TILES_PER_SEQ == 0, 0.0, tail_ref[7:8, :])
    row = lax.broadcasted_iota(jnp.int32, p.shape, 0)
    prev = jnp.where(row == 0, tail, pltpu.roll(p, 1, axis=0))
    prev = jnp.where(i == PREP_SAMPLE_TILE, state_ref[...], prev)
    pm = p + (prev - p) * mu_ref[...]

    r = pm[:, 0:D_RWKV]
    k = pm[:, D_RWKV:2 * D_RWKV]
    v = pm[:, 2 * D_RWKV:3 * D_RWKV]
    lora = pm[:, 3 * D_RWKV:]
    lane = lax.broadcasted_iota(jnp.int32, (p.shape[0], LANES), 1)
    x_wa = lora[:, 0:LANES]
    x_wa = jnp.where(lane < D_DECAY_LORA, jnp.tanh(x_wa), x_wa)
    x_g = jax.nn.sigmoid(lora[:, LANES:3 * LANES])
    wa = _dot_split(x_wa, wl_hi_ref[...], wl_lo_ref[...])
    g = _dot_split(x_g, wg_hi_ref[...], wg_lo_ref[...])

    z = -(w0_ref[...] + wa[:, 0:D_RWKV])
    softplus = jnp.maximum(z, 0.0) + jnp.log(1.0 + jnp.exp(-jnp.abs(z)))
    decay = jnp.exp(-jnp.exp(-softplus - 0.5))
    a = jax.nn.sigmoid(a0_ref[...] + wa[:, D_RWKV:])
    kk = k * kk_ref[...]
    kk = kk / jnp.maximum(jnp.sqrt(_segsum(kk * kk, e_ref[...])), 1e-12)
    kw_ref[KW_R] = r
    kw_ref[KW_DECAY] = decay
    kw_ref[KW_K] = k * (1.0 + (a - 1.0) * ka_ref[...])
    kw_ref[KW_KK] = kk
    kw_ref[KW_B] = kk * a
    v_ref[...] = v
    g_ref[...] = g


def _rwkv_prep(p_rwkv, state_shift, mu, w0, a0, k_k, k_a, wl_hi, wl_lo, wg_hi, wg_lo, e_heads):
    n_tiles = M_ALL // PREP_TM
    row = lambda n: pl.BlockSpec((1, n), lambda i: (0, 0))
    return pl.pallas_call(
        _rwkv_prep_kernel,
        grid=(n_tiles,),
        in_specs=[
            pl.BlockSpec((PREP_TM, RWKV_PROJ_PAD), lambda i: (i, 0)),
            pl.BlockSpec((8, RWKV_PROJ_PAD), lambda i: (jnp.maximum(i * (PREP_TM // 8) - 1, 0), 0)),
            pl.BlockSpec((DEC_BATCH, RWKV_PROJ_PAD), lambda i: (0, 0)),
            row(RWKV_PROJ_PAD), row(D_RWKV), row(D_RWKV), row(D_RWKV), row(D_RWKV),
            _const_spec(wl_hi.shape), _const_spec(wl_lo.shape),
            _const_spec(wg_hi.shape), _const_spec(wg_lo.shape),
            _const_spec(e_heads.shape),
        ],
        out_specs=[
            pl.BlockSpec((5, PREP_TM, D_RWKV), lambda i: (0, i, 0)),
            pl.BlockSpec((PREP_TM, D_RWKV), lambda i: (i, 0)),
            pl.BlockSpec((PREP_TM, D_RWKV), lambda i: (i, 0)),
        ],
        out_shape=[
            jax.ShapeDtypeStruct((5, M_ALL, D_RWKV), F32),
            jax.ShapeDtypeStruct((M_ALL, D_RWKV), F32),
            jax.ShapeDtypeStruct((M_ALL, D_RWKV), F32),
        ],
        compiler_params=_params("parallel"),
        name="rwkv_prep",
    )(p_rwkv, p_rwkv, state_shift, mu, w0, a0, k_k, k_a, wl_hi, wl_lo, wg_hi, wg_lo, e_heads)


SEQ_TB = 64
K_HALF = HEAD_DIM // 2
BH = BATCH * RWKV_HEADS


def _rwkv_seq_kernel(kw_ref, v_ref, y_ref, sout_ref, s_ref):
    @pl.when(pl.program_id(0) == 0)
    def _():
        s_ref[...] = jnp.zeros_like(s_ref)

    def step(t, carry):
        n_acc = 4
        acc = [jnp.zeros((HEAD_DIM, LANES), F32) for _ in range(n_acc)]
        for kp in range(K_HALF):
            acc[kp % n_acc] = acc[kp % n_acc] + s_ref[kp] * kw_ref[t, KW_KK, pl.ds(kp, 1), :]
        sa = (acc[0] + acc[1]) + (acc[2] + acc[3])
        sa = -(sa + pltpu.roll(sa, BH, axis=1))
        vt = v_ref[t]
        yacc = [jnp.zeros((HEAD_DIM, LANES), F32) for _ in range(n_acc)]
        for kp in range(K_HALF):
            s_new = (s_ref[kp] * kw_ref[t, KW_DECAY, pl.ds(kp, 1), :]
                     + sa * kw_ref[t, KW_B, pl.ds(kp, 1), :]
                     + vt * kw_ref[t, KW_K, pl.ds(kp, 1), :])
            s_ref[kp] = s_new
            yacc[kp % n_acc] = yacc[kp % n_acc] + s_new * kw_ref[t, KW_R, pl.ds(kp, 1), :]
        y = (yacc[0] + yacc[1]) + (yacc[2] + yacc[3])
        y_ref[t] = y + pltpu.roll(y, BH, axis=1)
        return carry

    lax.fori_loop(0, SEQ_TB, step, 0)

    @pl.when(pl.program_id(0) == pl.num_programs(0) - 1)
    def _():
        sout_ref[...] = s_ref[...]


def _rwkv_seq(kw_t, v_t):
    return pl.pallas_call(
        _rwkv_seq_kernel,
        grid=(SEQ // SEQ_TB,),
        in_specs=[
            pl.BlockSpec((SEQ_TB, 5, K_HALF, LANES), lambda i: (i, 0, 0, 0)),
            pl.BlockSpec((SEQ_TB, HEAD_DIM, LANES), lambda i: (i, 0, 0)),
        ],
        out_specs=[
            pl.BlockSpec((SEQ_TB, HEAD_DIM, LANES), lambda i: (i, 0, 0)),
            pl.BlockSpec((K_HALF, HEAD_DIM, LANES), lambda i: (0, 0, 0)),
        ],
        out_shape=[
            jax.ShapeDtypeStruct((SEQ, HEAD_DIM, LANES), F32),
            jax.ShapeDtypeStruct((K_HALF, HEAD_DIM, LANES), F32),
        ],
        scratch_shapes=[pltpu.VMEM((K_HALF, HEAD_DIM, LANES), F32)],
        compiler_params=_params("arbitrary"),
        name="rwkv_seq",
    )(kw_t, v_t)


STEP_TB = 8


def _rwkv_step_kernel(kw_ref, v_ref, s_ref, y_ref, sout_ref):
    r_i = lax.broadcasted_iota(jnp.int32, (HEAD_DIM, HEAD_DIM), 0)
    c_i = lax.broadcasted_iota(jnp.int32, (HEAD_DIM, HEAD_DIM), 1)
    eye = (r_i == c_i).astype(F32)

    def body(b, carry):
        rows = [kw_ref[idx, pl.ds(b, 1), :] for idx in range(5)]
        v_row = v_ref[pl.ds(b, 1), :]
        ys = []
        for h in range(RWKV_HEADS):
            hs = slice(h * HEAD_DIM, (h + 1) * HEAD_DIM)
            s = s_ref[b, h]
            v_col = jnp.sum(eye * v_row[:, hs], axis=1, keepdims=True)
            sa = -jnp.sum(s * rows[KW_KK][:, hs], axis=1, keepdims=True)
            s_new = s * rows[KW_DECAY][:, hs] + sa * rows[KW_B][:, hs] + v_col * rows[KW_K][:, hs]
            sout_ref[b, h] = s_new
            y_col = jnp.sum(s_new * rows[KW_R][:, hs], axis=1, keepdims=True)
            ys.append(jnp.sum(eye * y_col, axis=0, keepdims=True))
        y_ref[pl.ds(b, 1), :] = jnp.concatenate(ys, axis=1)
        return carry

    lax.fori_loop(0, STEP_TB, body, 0)


def _rwkv_step(kw, v, state):
    off = M_PROMPT // STEP_TB
    return pl.pallas_call(
        _rwkv_step_kernel,
        grid=(DEC_BATCH // STEP_TB,),
        in_specs=[
            pl.BlockSpec((5, STEP_TB, D_RWKV), lambda i: (0, i + off, 0)),
            pl.BlockSpec((STEP_TB, D_RWKV), lambda i: (i + off, 0)),
            pl.BlockSpec((STEP_TB, RWKV_HEADS, HEAD_DIM, HEAD_DIM), lambda i: (i, 0, 0, 0)),
        ],
        out_specs=[
            pl.BlockSpec((STEP_TB, D_RWKV), lambda i: (i, 0)),
            pl.BlockSpec((STEP_TB, RWKV_HEADS, HEAD_DIM, HEAD_DIM), lambda i: (i, 0, 0, 0)),
        ],
        out_shape=[
            jax.ShapeDtypeStruct((DEC_BATCH, D_RWKV), F32),
            jax.ShapeDtypeStruct((DEC_BATCH, RWKV_HEADS, HEAD_DIM, HEAD_DIM), F32),
        ],
        compiler_params=_params("parallel"),
        name="rwkv_step",
    )(kw, v, state)


ATT_PREP_TM = 640


def _swap_halves(x):
    lane = lax.broadcasted_iota(jnp.int32, x.shape, 1)
    first = (lane % HEAD_DIM) < HEAD_DIM // 2
    return jnp.where(first, pltpu.roll(x, LANES - HEAD_DIM // 2, axis=1), pltpu.roll(x, HEAD_DIM // 2, axis=1))


def _att_prep_kernel(p_ref, cos_ref, sin_ref, qg_ref, kg_ref, e_ref, o_ref):
    cos = cos_ref[...]
    sin = sin_ref[...]
    e = e_ref[...]
    for part, g_ref in ((0, qg_ref), (1, kg_ref)):
        x = p_ref[:, part * D_ATT:(part + 1) * D_ATT]
        ms = _segsum(x * x, e) * (1.0 / HEAD_DIM)
        xn = x * lax.rsqrt(ms + NORM_EPS) * g_ref[...]
        for c in range(D_ATT // LANES):
            xc = xn[:, c * LANES:(c + 1) * LANES]
            o_ref[:, part * D_ATT + c * LANES: part * D_ATT + (c + 1) * LANES] = xc * cos + _swap_halves(xc) * sin


def _att_prep(p_att, cos, sin, q_gain, k_gain, e_att):
    return pl.pallas_call(
        _att_prep_kernel,
        grid=(M_ALL // ATT_PREP_TM,),
        in_specs=[
            pl.BlockSpec((ATT_PREP_TM, 2 * D_ATT), lambda i: (i, 0)),
            pl.BlockSpec((ATT_PREP_TM, LANES), lambda i: (i, 0)),
            pl.BlockSpec((ATT_PREP_TM, LANES), lambda i: (i, 0)),
            pl.BlockSpec((1, D_ATT), lambda i: (0, 0)),
            pl.BlockSpec((1, D_ATT), lambda i: (0, 0)),
            _const_spec(e_att.shape),
        ],
        out_specs=pl.BlockSpec((ATT_PREP_TM, 2 * D_ATT), lambda i: (i, 0)),
        out_shape=jax.ShapeDtypeStruct((M_ALL, 2 * D_ATT), F32),
        compiler_params=_params("parallel"),
        name="att_prep",
    )(p_att, cos, sin, q_gain, k_gain, e_att)


ATT_TQ = 128
ATT_NSB = 8


def _att_window_kernel(q_ref, kp_ref, kc_ref, vp_ref, vc_ref, o_ref):
    qi = pl.program_id(1)
    q = q_ref[...].astype(BF16)
    k = jnp.concatenate([kp_ref[...], kc_ref[...]], axis=1).astype(BF16)
    v = jnp.concatenate([vp_ref[...], vc_ref[...]], axis=1).astype(BF16)
    s = jnp.einsum("sqd,skd->sqk", q, k, preferred_element_type=F32) * (HEAD_DIM ** -0.5)
    qpos = lax.broadcasted_iota(jnp.int32, s.shape, 1) + ATT_TQ
    kpos = lax.broadcasted_iota(jnp.int32, s.shape, 2)
    dist = qpos - kpos
    valid = (dist >= 0) & (dist <= N_KEYS_PAST) & ((kpos >= ATT_TQ) | (qi > 0))
    s = jnp.where(valid, s, NEG_INF)
    m = jnp.max(s, axis=-1, keepdims=True)
    p = jnp.exp(s - m)
    l = jnp.sum(p, axis=-1, keepdims=True)
    lse = m + jnp.log(l)
    pn = (p / l).astype(BF16)
    o = jnp.einsum("sqk,skd->sqd", pn, v, preferred_element_type=F32)
    o_ref[:, :, 0:HEAD_DIM] = o
    o_ref[:, :, HEAD_DIM:] = jnp.broadcast_to(lse, o.shape)


def _att_window(q, k, v):
    ns, ls, _ = q.shape
    blk = (ATT_NSB, ATT_TQ, HEAD_DIM)
    cur = lambda s, i: (s, i, 0)
    prv = lambda s, i: (s, jnp.maximum(i - 1, 0), 0)
    return pl.pallas_call(
        _att_window_kernel,
        grid=(ns // ATT_NSB, ls // ATT_TQ),
        in_specs=[pl.BlockSpec(blk, cur), pl.BlockSpec(blk, prv), pl.BlockSpec(blk, cur),
                  pl.BlockSpec(blk, prv), pl.BlockSpec(blk, cur)],
        out_specs=pl.BlockSpec((ATT_NSB, ATT_TQ, 2 * HEAD_DIM), cur),
        out_shape=jax.ShapeDtypeStruct((ns, ls, 2 * HEAD_DIM), F32),
        compiler_params=_params("parallel", "parallel"),
        name="att_window",
    )(q, k, k, v, v)


ATT_SB = 8


def _att_sample_kernel(qk_ref, v_ref, c0_ref, c1_ref, c2_ref, o0_ref, o1_ref, o2_ref):
    scale = HEAD_DIM ** -0.5

    def body(b, carry):
        qk_row = qk_ref[pl.ds(b, 1), :]
        v_row = v_ref[pl.ds(b, 1), :]
        for g, (c_ref, o_ref) in enumerate(((c0_ref, o0_ref), (c1_ref, o1_ref), (c2_ref, o2_ref))):
            kv_old = c_ref[b]
            parts = []
            for h in range(HEADS_PER_GROUP):
                col = g * D_GROUP + h * HEAD_DIM
                q = qk_row[:, col:col + HEAD_DIM]
                k_new = qk_row[:, D_ATT + col:D_ATT + col + HEAD_DIM]
                v_new = v_row[:, col:col + HEAD_DIM]
                k_old = kv_old[:, h * HEAD_DIM:(h + 1) * HEAD_DIM]
                v_old = kv_old[:, D_GROUP + h * HEAD_DIM:D_GROUP + (h + 1) * HEAD_DIM]
                s_old = jnp.sum(k_old * q, axis=1, keepdims=True) * scale
                s_new = jnp.sum(k_new * q, axis=1, keepdims=True) * scale
                m = jnp.maximum(jnp.max(s_old, axis=0, keepdims=True), s_new)
                p_old = jnp.exp(s_old - m)
                p_new = jnp.exp(s_new - m)
                l = jnp.sum(p_old, axis=0, keepdims=True) + p_new
                o = (jnp.sum(p_old * v_old, axis=0, keepdims=True) + p_new * v_new) / l
                parts += [o, jnp.broadcast_to(m + jnp.log(l), o.shape)]
            o_ref[pl.ds(b, 1), :] = jnp.concatenate(parts, axis=1)
        return carry

    lax.fori_loop(0, ATT_SB, body, 0)


def _att_sample(qk, p_att_v, caches):
    off = M_PROMPT // ATT_SB
    kv = 2 * D_GROUP
    cache_specs = [pl.BlockSpec((ATT_SB, N_KEYS_PAST, kv), lambda i: (i, 0, 0)) for _ in caches]
    out = jax.ShapeDtypeStruct((DEC_BATCH, HEADS_PER_GROUP * LANES), F32)
    return pl.pallas_call(
        _att_sample_kernel,
        grid=(DEC_BATCH // ATT_SB,),
        in_specs=[pl.BlockSpec((ATT_SB, 2 * D_ATT), lambda i: (i + off, 0)),
                  pl.BlockSpec((ATT_SB, D_ATT), lambda i: (i + off, 2))] + cache_specs,
        out_specs=[pl.BlockSpec((ATT_SB, HEADS_PER_GROUP * LANES), lambda i: (i, 0))] * 3,
        out_shape=[out] * 3,
        compiler_params=_params("parallel"),
        name="att_sample",
    )(qk, p_att_v, *caches)


MERGE_TM = 128


def _merge_kernel(x_ref, y_ref, r_ref, k_ref, v_ref, g_ref, a0_ref, a1_ref, a2_ref, pg_ref, bg_ref,
                  lnw_ref, lnb_ref, rk_ref, e_ref, wbr_ref, wba0_ref, wba1_ref, wba2_ref, wo_ref, o_ref):
    e = e_ref[...]
    inv_n = 1.0 / HEAD_DIM
    y = y_ref[...]
    d = y - _segsum(y, e) * inv_n
    var = _segsum(d * d, e) * inv_n
    yn = d * lax.rsqrt(var + RWKV_LN_EPS) * lnw_ref[...] + lnb_ref[...]
    bonus = _segsum(r_ref[...] * k_ref[...] * rk_ref[...], e) * v_ref[...]
    o_a = ((yn + bonus) * g_ref[...]).astype(BF16)
    br_a = jnp.dot(o_a, wbr_ref[...], preferred_element_type=F32)
    a = (a0_ref[...], a1_ref[...], a2_ref[...])
    m = jnp.maximum(jnp.maximum(a[0], a[1]), a[2])
    ex = [jnp.exp(t - m) for t in a]
    inv = 1.0 / (ex[0] + ex[1] + ex[2])
    br_b = jnp.zeros_like(br_a)
    for grp, wb_ref in enumerate((wba0_ref, wba1_ref, wba2_ref)):
        alpha = ex[grp] * inv
        parts = []
        for h in range(HEADS_PER_GROUP):
            hs = slice(h * LANES, (h + 1) * LANES)
            parts.append(a[grp][:, hs] * pltpu.roll(alpha[:, hs], HEAD_DIM, axis=1))
        xg = jnp.concatenate(parts, axis=1).astype(BF16)
        br_b = br_b + jnp.dot(xg, wb_ref[...], preferred_element_type=F32)
    gates = jax.nn.sigmoid(pg_ref[...] + bg_ref[...])
    merged = (gates[:, 0:D_MODEL] * br_a + gates[:, D_MODEL:] * br_b).astype(BF16)
    o_ref[...] = x_ref[...] + jnp.dot(merged, wo_ref[...], preferred_element_type=F32)


def _merge(x, y, kw, v, g, att, p_gate, b_gate, ln_w, ln_b, r_k, e_heads, w_br, w_ba, w_o):
    tile = lambda n: pl.BlockSpec((MERGE_TM, n), lambda i: (i, 0))
    row = lambda n: pl.BlockSpec((1, n), lambda i: (0, 0))
    kw_spec = lambda idx: pl.BlockSpec((None, MERGE_TM, D_RWKV), lambda i: (idx, i, 0))
    att_w = HEADS_PER_GROUP * LANES
    return pl.pallas_call(
        _merge_kernel,
        grid=(M_ALL // MERGE_TM,),
        in_specs=[tile(D_MODEL), tile(D_RWKV), kw_spec(KW_R), kw_spec(KW_K), tile(D_RWKV), tile(D_RWKV),
                  tile(att_w), tile(att_w), tile(att_w), tile(2 * D_MODEL), row(2 * D_MODEL),
                  row(D_RWKV), row(D_RWKV), row(D_RWKV), _const_spec(e_heads.shape), _const_spec(w_br.shape),
                  _const_spec(w_ba[0].shape), _const_spec(w_ba[1].shape), _const_spec(w_ba[2].shape),
                  _const_spec(w_o.shape)],
        out_specs=tile(D_MODEL),
        out_shape=jax.ShapeDtypeStruct((M_ALL, D_MODEL), F32),
        compiler_params=_params("parallel"),
        name="merge",
    )(x, y, kw, kw, v, g, att[0], att[1], att[2], p_gate, b_gate, ln_w, ln_b, r_k, e_heads, w_br,
      w_ba[0], w_ba[1], w_ba[2], w_o)


def _split_bf16(w):
    hi = w.astype(BF16)
    return hi, (w - hi.astype(F32)).astype(BF16)


def _head_ones(n_heads):
    return jnp.kron(jnp.eye(n_heads, dtype=F32), jnp.ones((HEAD_DIM, HEAD_DIM), F32)).astype(BF16)


def _rotary_tables():
    half = HEAD_DIM // 2
    inv_freq = ROPE_THETA ** (-jnp.arange(half, dtype=F32) / half)
    pos = jnp.concatenate([jnp.tile(jnp.arange(SEQ), BATCH), jnp.full((DEC_BATCH,), PAST_LEN)]).astype(F32)
    ang = pos[:, None] * inv_freq[None, :]
    cos, sin = jnp.cos(ang), jnp.sin(ang)
    cos = jnp.concatenate([cos, cos] * (LANES // HEAD_DIM), axis=1)
    sin = jnp.concatenate([-sin, sin] * (LANES // HEAD_DIM), axis=1)
    return cos, sin


def kernel(x_prompt, x_sample, state_rwkv_shift, state_rwkv_wkv, cache_att_w128, cache_att_w512, cache_att_w2048,
           ffn1_norm, ffn1_w_in, ffn1_w_out, mix_norm, w_in, b_gate, rwkv_mu, rwkv_w0, rwkv_w2, rwkv_a0, rwkv_a2,
           rwkv_g2, rwkv_k_k, rwkv_k_a, rwkv_r_k, rwkv_ln_w, rwkv_ln_b, w_branch_rwkv, attn_q_norm, attn_k_norm,
           w_branch_attn, w_out, ffn2_norm, ffn2_w_in, ffn2_w_out):
    caches = (cache_att_w128, cache_att_w512, cache_att_w2048)
    x = jnp.concatenate([x_prompt.reshape(M_PROMPT, D_MODEL), x_sample.reshape(DEC_BATCH, D_MODEL)], axis=0)

    w_mix = w_in[0]
    pad = RWKV_PROJ_PAD - D_RWKV_PROJ
    w_rwkv = jnp.pad(w_mix[:, :D_RWKV_PROJ], ((0, 0), (0, pad))).astype(BF16)
    w_att = w_mix[:, D_RWKV_PROJ:D_RWKV_PROJ + 3 * D_ATT].astype(BF16)
    w_gate = w_mix[:, D_RWKV_PROJ + 3 * D_ATT:].astype(BF16)
    mu = jnp.pad(rwkv_mu, ((0, 0), (0, pad)))
    state_shift = jnp.pad(state_rwkv_shift[0], ((0, 0), (0, pad)))
    zeros = jnp.zeros((D_DECAY_LORA, D_RWKV), F32)
    w_lora = jnp.concatenate([jnp.concatenate([rwkv_w2[0], zeros], axis=1),
                              jnp.concatenate([zeros, rwkv_a2[0]], axis=1)], axis=0)
    w_glora = jnp.pad(rwkv_g2[0], ((0, 2 * LANES - D_GATE_LORA), (0, 0)))
    wl_hi, wl_lo = _split_bf16(w_lora)
    wg_hi, wg_lo = _split_bf16(w_glora)
    e_heads = _head_ones(RWKV_HEADS)
    e_att = _head_ones(ATT_HEADS)
    cos, sin = _rotary_tables()
    q_gain = jnp.tile(attn_q_norm, (1, ATT_HEADS))
    k_gain = jnp.tile(attn_k_norm, (1, ATT_HEADS))
    wba = w_branch_attn[0].reshape(len(ATT_GROUPS), HEADS_PER_GROUP, HEAD_DIM, D_MODEL)
    wba = jnp.pad(wba, ((0, 0), (0, 0), (0, LANES - HEAD_DIM), (0, 0))).astype(BF16)
    wba = wba.reshape(len(ATT_GROUPS), HEADS_PER_GROUP * LANES, D_MODEL)

    x1 = _ffn(x, ffn1_norm, ffn1_w_in[0].astype(BF16), ffn1_w_out[0].astype(BF16))

    p_rwkv = _normproj(x1, mix_norm, w_rwkv, 512, "proj_rwkv")
    p_att = _normproj(x1, mix_norm, w_att, 768, "proj_att")
    p_gate = _normproj(x1, mix_norm, w_gate, 512, "proj_gate")

    kw, v_r, g_r = _rwkv_prep(p_rwkv, state_shift, mu, rwkv_w0, rwkv_a0, rwkv_k_k, rwkv_k_a,
                              wl_hi, wl_lo, wg_hi, wg_lo, e_heads)
    kw_t = kw[:, :M_PROMPT].reshape(5, BATCH, SEQ, RWKV_HEADS, 2, K_HALF)
    kw_t = kw_t.transpose(2, 0, 5, 4, 1, 3).reshape(SEQ, 5, K_HALF, LANES)
    v_t = v_r[:M_PROMPT].reshape(BATCH, SEQ, RWKV_HEADS, HEAD_DIM).transpose(1, 3, 0, 2).reshape(SEQ, HEAD_DIM, BH)
    v_t = jnp.concatenate([v_t, v_t], axis=-1)
    y_t, s_t = _rwkv_seq(kw_t, v_t)
    y_p = y_t[:, :, :BH].reshape(SEQ, HEAD_DIM, BATCH, RWKV_HEADS).transpose(2, 0, 3, 1).reshape(M_PROMPT, D_RWKV)
    wkv_p = s_t.reshape(K_HALF, HEAD_DIM, 2, BATCH, RWKV_HEADS).transpose(3, 4, 1, 2, 0)
    wkv_p = wkv_p.reshape(BATCH, RWKV_HEADS, HEAD_DIM, HEAD_DIM)
    y_s, wkv_s = _rwkv_step(kw, v_r, state_rwkv_wkv[0])
    y_r = jnp.concatenate([y_p, y_s], axis=0)

    qk = _att_prep(p_att, cos, sin, q_gain, k_gain, e_att)
    att_p, kv_p, kv_s = [], [], []
    for g, (window, dil) in enumerate(ATT_GROUPS):
        ls = SEQ // dil

        def seqs(t):
            t = t.reshape(BATCH, ls, dil, HEADS_PER_GROUP, HEAD_DIM).transpose(0, 2, 3, 1, 4)
            return t.reshape(BATCH * dil * HEADS_PER_GROUP, ls, HEAD_DIM)

        q_g = qk[:M_PROMPT, g * D_GROUP:(g + 1) * D_GROUP]
        k_g = qk[:M_PROMPT, D_ATT + g * D_GROUP:D_ATT + (g + 1) * D_GROUP]
        v_g = p_att[:M_PROMPT, 2 * D_ATT + g * D_GROUP:2 * D_ATT + (g + 1) * D_GROUP]
        o = _att_window(seqs(q_g), seqs(k_g), seqs(v_g))
        o = o.reshape(BATCH, dil, HEADS_PER_GROUP, ls, LANES).transpose(0, 3, 1, 2, 4)
        att_p.append(o.reshape(M_PROMPT, HEADS_PER_GROUP * LANES))
        keep = min(window, SEQ)
        kv = jnp.stack([k_g.reshape(BATCH, SEQ, HEADS_PER_GROUP, HEAD_DIM),
                        v_g.reshape(BATCH, SEQ, HEADS_PER_GROUP, HEAD_DIM)], axis=2)
        kv_p.append(kv[:, SEQ - keep:][None])
        k_s = qk[M_PROMPT:, D_ATT + g * D_GROUP:D_ATT + (g + 1) * D_GROUP]
        v_s = p_att[M_PROMPT:, 2 * D_ATT + g * D_GROUP:2 * D_ATT + (g + 1) * D_GROUP]
        new = jnp.stack([k_s.reshape(DEC_BATCH, 1, HEADS_PER_GROUP, HEAD_DIM),
                         v_s.reshape(DEC_BATCH, 1, HEADS_PER_GROUP, HEAD_DIM)], axis=2)
        kv_s.append(jnp.concatenate([caches[g][0][:, 1:], new], axis=1)[None])
    dilated = [c[0].reshape(DEC_BATCH, N_KEYS_PAST, dil * 2 * D_GROUP) for c, (_, dil) in zip(caches, ATT_GROUPS)]
    att_s = _att_sample(qk, p_att, dilated)
    att = [jnp.concatenate([a_p, a_s], axis=0) for a_p, a_s in zip(att_p, att_s)]

    x2 = _merge(x1, y_r, kw, v_r, g_r, att, p_gate, b_gate, rwkv_ln_w, rwkv_ln_b,
                rwkv_r_k.reshape(1, D_RWKV), e_heads, w_branch_rwkv[0].astype(BF16), wba, w_out[0].astype(BF16))
    y = _ffn(x2, ffn2_norm, ffn2_w_in[0].astype(BF16), ffn2_w_out[0].astype(BF16))

    shift_p = p_rwkv[:M_PROMPT, :D_RWKV_PROJ].reshape(BATCH, SEQ, D_RWKV_PROJ)[:, -1][None]
    shift_s = p_rwkv[M_PROMPT:, :D_RWKV_PROJ][None]
    return (y[:M_PROMPT].reshape(BATCH, SEQ, D_MODEL), y[M_PROMPT:].reshape(DEC_BATCH, 1, D_MODEL),
            shift_p, wkv_p[None], kv_p[0], kv_p[1], kv_p[2],
            shift_s, wkv_s[None], kv_s[0], kv_s[1], kv_s[2])
```

```python
import functools

import jax
import jax.numpy as jnp
from jax import lax
from jax.experimental import pallas as pl
from jax.experimental.pallas import tpu as pltpu

D_MODEL = 2048
BATCH = 4
SEQ = 2048
DEC_BATCH = 128
PAST_LEN = 2048
M_PROMPT = BATCH * SEQ
M_ALL = M_PROMPT + DEC_BATCH

RWKV_HEADS = 16
HEAD_DIM = 64
D_RWKV = RWKV_HEADS * HEAD_DIM
D_DECAY_LORA = 64
D_ICLR_LORA = 64
D_GATE_LORA = 160
D_RWKV_PROJ = 3 * D_RWKV + D_DECAY_LORA + D_ICLR_LORA + D_GATE_LORA
RWKV_PROJ_PAD = 3584
RWKV_LN_EPS = 64e-5

ATT_GROUPS = ((128, 1), (512, 4), (2048, 16))
HEADS_PER_GROUP = 4
ATT_HEADS = HEADS_PER_GROUP * len(ATT_GROUPS)
D_ATT = ATT_HEADS * HEAD_DIM
D_GROUP = HEADS_PER_GROUP * HEAD_DIM
N_KEYS_PAST = 128
ROPE_THETA = 10000.0
D_FF = 5632
NORM_EPS = 1e-6
NEG_INF = -1e30

LANES = 128
VMEM_LIMIT = 56 * 1024 * 1024

F32 = jnp.float32
BF16 = jnp.bfloat16


def _params(*sem):
    return pltpu.CompilerParams(dimension_semantics=sem, vmem_limit_bytes=VMEM_LIMIT)


def _const_spec(shape):
    nd = len(shape)
    return pl.BlockSpec(shape, lambda *_: (0,) * nd, pipeline_mode=pl.Buffered(1))


def _rms_to_bf16(x, g):
    ms = jnp.mean(x * x, axis=-1, keepdims=True)
    return (x * lax.rsqrt(ms + NORM_EPS) * g).astype(BF16)


def _segsum(x, e):
    hi = x.astype(BF16)
    lo = (x - hi.astype(F32)).astype(BF16)
    return (jnp.dot(hi, e, preferred_element_type=F32) + jnp.dot(lo, e, preferred_element_type=F32))


def _dot_split(x, w_hi, w_lo):
    hi = x.astype(BF16)
    lo = (x - hi.astype(F32)).astype(BF16)
    return (jnp.dot(hi, w_hi, preferred_element_type=F32) + jnp.dot(lo, w_hi, preferred_element_type=F32)
            + jnp.dot(hi, w_lo, preferred_element_type=F32))


FFN_TM = 640
FFN_TF = 512


def _ffn_kernel(x_ref, g_ref, wg_ref, wu_ref, wo_ref, o_ref, h_ref):
    j = pl.program_id(1)

    @pl.when(j == 0)
    def _():
        h_ref[...] = _rms_to_bf16(x_ref[...], g_ref[...])
        o_ref[...] = jnp.zeros_like(o_ref)

    h = h_ref[...]
    gate = jnp.dot(h, wg_ref[...], preferred_element_type=F32)
    up = jnp.dot(h, wu_ref[...], preferred_element_type=F32)
    act = (gate * jax.nn.sigmoid(gate) * up).astype(BF16)
    o_ref[...] += jnp.dot(act, wo_ref[...], preferred_element_type=F32)

    @pl.when(j == pl.num_programs(1) - 1)
    def _():
        o_ref[...] = x_ref[...] + 0.5 * o_ref[...]


def _ffn(x, g, w_in, w_out):
    m = x.shape[0]
    nj = D_FF // FFN_TF
    return pl.pallas_call(
        _ffn_kernel,
        grid=(m // FFN_TM, nj),
        in_specs=[
            pl.BlockSpec((FFN_TM, D_MODEL), lambda i, j: (i, 0)),
            pl.BlockSpec((1, D_MODEL), lambda i, j: (0, 0)),
            pl.BlockSpec((D_MODEL, FFN_TF), lambda i, j: (0, j)),
            pl.BlockSpec((D_MODEL, FFN_TF), lambda i, j: (0, j + nj)),
            pl.BlockSpec((FFN_TF, D_MODEL), lambda i, j: (j, 0)),
        ],
        out_specs=pl.BlockSpec((FFN_TM, D_MODEL), lambda i, j: (i, 0)),
        out_shape=jax.ShapeDtypeStruct((m, D_MODEL), F32),
        scratch_shapes=[pltpu.VMEM((FFN_TM, D_MODEL), BF16)],
        compiler_params=_params("parallel", "arbitrary"),
        name="ffn",
    )(x, g, w_in, w_in, w_out)


PROJ_TM = 640


def _normproj_kernel(x_ref, g_ref, w_ref, o_ref, h_ref):
    @pl.when(pl.program_id(1) == 0)
    def _():
        h_ref[...] = _rms_to_bf16(x_ref[...], g_ref[...])

    o_ref[...] = jnp.dot(h_ref[...], w_ref[...], preferred_element_type=F32)


def _normproj(x, g, w, tn, name):
    m, n = x.shape[0], w.shape[1]
    return pl.pallas_call(
        _normproj_kernel,
        grid=(m // PROJ_TM, n // tn),
        in_specs=[
            pl.BlockSpec((PROJ_TM, D_MODEL), lambda i, j: (i, 0)),
            pl.BlockSpec((1, D_MODEL), lambda i, j: (0, 0)),
            pl.BlockSpec((D_MODEL, tn), lambda i, j: (0, j)),
        ],
        out_specs=pl.BlockSpec((PROJ_TM, tn), lambda i, j: (i, j)),
        out_shape=jax.ShapeDtypeStruct((m, n), F32),
        scratch_shapes=[pltpu.VMEM((PROJ_TM, D_MODEL), BF16)],
        compiler_params=_params("parallel", "arbitrary"),
        name=name,
    )(x, g, w)


PREP_TM = 128
PREP_TILES_PER_SEQ = SEQ // PREP_TM
PREP_SAMPLE_TILE = M_PROMPT // PREP_TM
KW_R, KW_DECAY, KW_K, KW_KK, KW_B = range(5)


def _rwkv_prep_kernel(p_ref, tail_ref, state_ref, mu_ref, w0_ref, a0_ref, kk_ref, ka_ref, rk_ref,
                      wl_hi_ref, wl_lo_ref, wg_hi_ref, wg_lo_ref, e_ref, kw_ref, v_ref, bonus_ref, g_ref):
    i = pl.program_id(0)
    p = p_ref[...]
    tail = jnp.where(i % PREP_TILES_PER_SEQ == 0, 0.0, tail_ref[7:8, :])
    row = lax.broadcasted_iota(jnp.int32, p.shape, 0)
    prev = jnp.where(row == 0, tail, pltpu.roll(p, 1, axis=0))
    prev = jnp.where(i == PREP_SAMPLE_TILE, state_ref[...], prev)
    pm = p + (prev - p) * mu_ref[...]

    r = pm[:, 0:D_RWKV]
    k = pm[:, D_RWKV:2 * D_RWKV]
    v = pm[:, 2 * D_RWKV:3 * D_RWKV]
    lora = pm[:, 3 * D_RWKV:]
    lane = lax.broadcasted_iota(jnp.int32, (p.shape[0], LANES), 1)
    x_wa = lora[:, 0:LANES]
    x_wa = jnp.where(lane < D_DECAY_LORA, jnp.tanh(x_wa), x_wa)
    x_g = jax.nn.sigmoid(lora[:, LANES:3 * LANES])
    wa = _dot_split(x_wa, wl_hi_ref[...], wl_lo_ref[...])
    g = _dot_split(x_g, wg_hi_ref[...], wg_lo_ref[...])

    z = -(w0_ref[...] + wa[:, 0:D_RWKV])
    softplus = jnp.maximum(z, 0.0) + jnp.log(1.0 + jnp.exp(-jnp.abs(z)))
    decay = jnp.exp(-jnp.exp(-softplus - 0.5))
    a = jax.nn.sigmoid(a0_ref[...] + wa[:, D_RWKV:])
    e = e_ref[...]
    kk = k * kk_ref[...]
    kk = kk / jnp.maximum(jnp.sqrt(_segsum(kk * kk, e)), 1e-12)
    k_hat = k * (1.0 + (a - 1.0) * ka_ref[...])
    kw_ref[KW_R] = r
    kw_ref[KW_DECAY] = decay
    kw_ref[KW_K] = k_hat
    kw_ref[KW_KK] = kk
    kw_ref[KW_B] = kk * a
    v_ref[...] = v
    bonus_ref[...] = _segsum(r * k_hat * rk_ref[...], e) * v
    g_ref[...] = g


def _rwkv_prep(p_rwkv, state_shift, mu, w0, a0, k_k, k_a, r_k, wl_hi, wl_lo, wg_hi, wg_lo, e_heads):
    n_tiles = M_ALL // PREP_TM
    row = lambda n: pl.BlockSpec((1, n), lambda i: (0, 0))
    return pl.pallas_call(
        _rwkv_prep_kernel,
        grid=(n_tiles,),
        in_specs=[
            pl.BlockSpec((PREP_TM, RWKV_PROJ_PAD), lambda i: (i, 0)),
            pl.BlockSpec((8, RWKV_PROJ_PAD), lambda i: (jnp.maximum(i * (PREP_TM // 8) - 1, 0), 0)),
            pl.BlockSpec((DEC_BATCH, RWKV_PROJ_PAD), lambda i: (0, 0)),
            row(RWKV_PROJ_PAD), row(D_RWKV), row(D_RWKV), row(D_RWKV), row(D_RWKV), row(D_RWKV),
            _const_spec(wl_hi.shape), _const_spec(wl_lo.shape),
            _const_spec(wg_hi.shape), _const_spec(wg_lo.shape),
            _const_spec(e_heads.shape),
        ],
        out_specs=[
            pl.BlockSpec((5, PREP_TM, D_RWKV), lambda i: (0, i, 0)),
            pl.BlockSpec((PREP_TM, D_RWKV), lambda i: (i, 0)),
            pl.BlockSpec((PREP_TM, D_RWKV), lambda i: (i, 0)),
            pl.BlockSpec((PREP_TM, D_RWKV), lambda i: (i, 0)),
        ],
        out_shape=[
            jax.ShapeDtypeStruct((5, M_ALL, D_RWKV), F32),
            jax.ShapeDtypeStruct((M_ALL, D_RWKV), F32),
            jax.ShapeDtypeStruct((M_ALL, D_RWKV), F32),
            jax.ShapeDtypeStruct((M_ALL, D_RWKV), F32),
        ],
        compiler_params=_params("parallel"),
        name="rwkv_prep",
    )(p_rwkv, p_rwkv, state_shift, mu, w0, a0, k_k, k_a, r_k, wl_hi, wl_lo, wg_hi, wg_lo, e_heads)


SEQ_TB = 64
K_HALF = HEAD_DIM // 2
BH = BATCH * RWKV_HEADS


def _rwkv_seq_kernel(kw_ref, v_ref, y_ref, sout_ref, s_ref):
    @pl.when(pl.program_id(0) == 0)
    def _():
        s_ref[...] = jnp.zeros_like(s_ref)

    sk = jnp.zeros((HEAD_DIM, LANES), F32)
    for kp in range(K_HALF):
        sk = sk + s_ref[kp] * kw_ref[0, KW_KK, pl.ds(kp, 1), :]

    def step(t, sk):
        sa = -(sk + pltpu.roll(sk, BH, axis=1))
        vt = v_ref[t]
        t_next = jnp.minimum(t + 1, SEQ_TB - 1)
        y = jnp.zeros((HEAD_DIM, LANES), F32)
        sk_next = jnp.zeros((HEAD_DIM, LANES), F32)
        for kp in range(K_HALF):
            s_new = (s_ref[kp] * kw_ref[t, KW_DECAY, pl.ds(kp, 1), :]
                     + sa * kw_ref[t, KW_B, pl.ds(kp, 1), :]
                     + vt * kw_ref[t, KW_K, pl.ds(kp, 1), :])
            s_ref[kp] = s_new
            y = y + s_new * kw_ref[t, KW_R, pl.ds(kp, 1), :]
            sk_next = sk_next + s_new * kw_ref[t_next, KW_KK, pl.ds(kp, 1), :]
        y_ref[t] = y + pltpu.roll(y, BH, axis=1)
        return sk_next

    lax.fori_loop(0, SEQ_TB, step, sk)

    @pl.when(pl.program_id(0) == pl.num_programs(0) - 1)
    def _():
        sout_ref[...] = s_ref[...]


def _rwkv_seq(kw_t, v_t):
    return pl.pallas_call(
        _rwkv_seq_kernel,
        grid=(SEQ // SEQ_TB,),
        in_specs=[
            pl.BlockSpec((SEQ_TB, 5, K_HALF, LANES), lambda i: (i, 0, 0, 0)),
            pl.BlockSpec((SEQ_TB, HEAD_DIM, LANES), lambda i: (i, 0, 0)),
        ],
        out_specs=[
            pl.BlockSpec((SEQ_TB, HEAD_DIM, LANES), lambda i: (i, 0, 0)),
            pl.BlockSpec((K_HALF, HEAD_DIM, LANES), lambda i: (0, 0, 0)),
        ],
        out_shape=[
            jax.ShapeDtypeStruct((SEQ, HEAD_DIM, LANES), F32),
            jax.ShapeDtypeStruct((K_HALF, HEAD_DIM, LANES), F32),
        ],
        scratch_shapes=[pltpu.VMEM((K_HALF, HEAD_DIM, LANES), F32)],
        compiler_params=_params("arbitrary"),
        name="rwkv_seq",
    )(kw_t, v_t)


def _rwkv_step_kernel(kw_ref, v_ref, s_ref, y_ref, sout_ref):
    def body(vi, carry):
        s = s_ref[0, vi]
        sa = -jnp.sum(s * kw_ref[KW_KK], axis=0, keepdims=True)
        s_new = s * kw_ref[KW_DECAY] + sa * kw_ref[KW_B] + v_ref[pl.ds(vi, 1), :] * kw_ref[KW_K]
        sout_ref[0, vi] = s_new
        y_ref[pl.ds(vi, 1), :] = jnp.sum(s_new * kw_ref[KW_R], axis=0, keepdims=True)
        return carry

    lax.fori_loop(0, HEAD_DIM, body, 0)


def _rwkv_step(kw_t, v_t, state_t):
    return pl.pallas_call(
        _rwkv_step_kernel,
        grid=(RWKV_HEADS,),
        in_specs=[
            pl.BlockSpec((5, HEAD_DIM, DEC_BATCH), lambda h: (0, h, 0)),
            pl.BlockSpec((HEAD_DIM, DEC_BATCH), lambda h: (h, 0)),
            pl.BlockSpec((1, HEAD_DIM, HEAD_DIM, DEC_BATCH), lambda h: (h, 0, 0, 0)),
        ],
        out_specs=[
            pl.BlockSpec((HEAD_DIM, DEC_BATCH), lambda h: (h, 0)),
            pl.BlockSpec((1, HEAD_DIM, HEAD_DIM, DEC_BATCH), lambda h: (h, 0, 0, 0)),
        ],
        out_shape=[
            jax.ShapeDtypeStruct((D_RWKV, DEC_BATCH), F32),
            jax.ShapeDtypeStruct((RWKV_HEADS, HEAD_DIM, HEAD_DIM, DEC_BATCH), F32),
        ],
        compiler_params=_params("parallel"),
        name="rwkv_step",
    )(kw_t, v_t, state_t)


ATT_PREP_TM = 640


def _swap_halves(x):
    lane = lax.broadcasted_iota(jnp.int32, x.shape, 1)
    first = (lane % HEAD_DIM) < HEAD_DIM // 2
    return jnp.where(first, pltpu.roll(x, LANES - HEAD_DIM // 2, axis=1), pltpu.roll(x, HEAD_DIM // 2, axis=1))


def _att_prep_kernel(p_ref, cos_ref, sin_ref, qg_ref, kg_ref, e_ref, o_ref):
    cos = cos_ref[...]
    sin = sin_ref[...]
    e = e_ref[...]
    for part, g_ref in ((0, qg_ref), (1, kg_ref)):
        x = p_ref[:, part * D_ATT:(part + 1) * D_ATT]
        ms = _segsum(x * x, e) * (1.0 / HEAD_DIM)
        xn = x * lax.rsqrt(ms + NORM_EPS) * g_ref[...]
        for c in range(D_ATT // LANES):
            xc = xn[:, c * LANES:(c + 1) * LANES]
            o_ref[:, part * D_ATT + c * LANES: part * D_ATT + (c + 1) * LANES] = xc * cos + _swap_halves(xc) * sin


def _att_prep(p_att, cos, sin, q_gain, k_gain, e_att):
    return pl.pallas_call(
        _att_prep_kernel,
        grid=(M_ALL // ATT_PREP_TM,),
        in_specs=[
            pl.BlockSpec((ATT_PREP_TM, 2 * D_ATT), lambda i: (i, 0)),
            pl.BlockSpec((ATT_PREP_TM, LANES), lambda i: (i, 0)),
            pl.BlockSpec((ATT_PREP_TM, LANES), lambda i: (i, 0)),
            pl.BlockSpec((1, D_ATT), lambda i: (0, 0)),
            pl.BlockSpec((1, D_ATT), lambda i: (0, 0)),
            _const_spec(e_att.shape),
        ],
        out_specs=pl.BlockSpec((ATT_PREP_TM, 2 * D_ATT), lambda i: (i, 0)),
        out_shape=jax.ShapeDtypeStruct((M_ALL, 2 * D_ATT), F32),
        compiler_params=_params("parallel"),
        name="att_prep",
    )(p_att, cos, sin, q_gain, k_gain, e_att)


ATT_TQ = 128
ATT_NSB = 8


def _att_window_kernel(q_ref, kp_ref, kc_ref, vp_ref, vc_ref, o_ref):
    qi = pl.program_id(1)
    q = q_ref[...].astype(BF16)
    k = jnp.concatenate([kp_ref[...], kc_ref[...]], axis=1).astype(BF16)
    v = jnp.concatenate([vp_ref[...], vc_ref[...]], axis=1).astype(BF16)
    s = jnp.einsum("sqd,skd->sqk", q, k, preferred_element_type=F32) * (HEAD_DIM ** -0.5)
    qpos = lax.broadcasted_iota(jnp.int32, s.shape, 1) + ATT_TQ
    kpos = lax.broadcasted_iota(jnp.int32, s.shape, 2)
    dist = qpos - kpos
    valid = (dist >= 0) & (dist <= N_KEYS_PAST) & ((kpos >= ATT_TQ) | (qi > 0))
    s = jnp.where(valid, s, NEG_INF)
    m = jnp.max(s, axis=-1, keepdims=True)
    p = jnp.exp(s - m)
    l = jnp.sum(p, axis=-1, keepdims=True)
    lse = m + jnp.log(l)
    pn = (p / l).astype(BF16)
    o = jnp.einsum("sqk,skd->sqd", pn, v, preferred_element_type=F32)
    o_ref[:, :, 0:HEAD_DIM] = o
    o_ref[:, :, HEAD_DIM:] = jnp.broadcast_to(lse, o.shape)


def _att_window(q, k, v):
    ns, ls, _ = q.shape
    blk = (ATT_NSB, ATT_TQ, HEAD_DIM)
    cur = lambda s, i: (s, i, 0)
    prv = lambda s, i: (s, jnp.maximum(i - 1, 0), 0)
    return pl.pallas_call(
        _att_window_kernel,
        grid=(ns // ATT_NSB, ls // ATT_TQ),
        in_specs=[pl.BlockSpec(blk, cur), pl.BlockSpec(blk, prv), pl.BlockSpec(blk, cur),
                  pl.BlockSpec(blk, prv), pl.BlockSpec(blk, cur)],
        out_specs=pl.BlockSpec((ATT_NSB, ATT_TQ, 2 * HEAD_DIM), cur),
        out_shape=jax.ShapeDtypeStruct((ns, ls, 2 * HEAD_DIM), F32),
        compiler_params=_params("parallel", "parallel"),
        name="att_window",
    )(q, k, k, v, v)


CACHE_BATCH_PER_STEP = {128: 8, 512: 4, 2048: 1}


def _cache_attn_kernel(qkv_ref, c_ref, o_ref, cout_ref, *, dil, nb):
    n_tiles = c_ref.shape[-1] // LANES
    scale = HEAD_DIM ** -0.5
    r_i = lax.broadcasted_iota(jnp.int32, (HEAD_DIM, HEAD_DIM), 0)
    c_i = lax.broadcasted_iota(jnp.int32, (HEAD_DIM, HEAD_DIM), 1)
    eye = (r_i == c_i).astype(F32)
    on_grid = (lax.broadcasted_iota(jnp.int32, (1, LANES), 1) % dil) == 0
    last_lane = lax.broadcasted_iota(jnp.int32, (HEAD_DIM, LANES), 1) == LANES - 1

    def to_col(row):
        return jnp.sum(eye * row, axis=1, keepdims=True)

    for bi in range(nb):
        row = qkv_ref[bi]
        parts = []
        for h in range(HEADS_PER_GROUP):
            q = row[:, h * HEAD_DIM:(h + 1) * HEAD_DIM]
            k_new = row[:, D_GROUP + h * HEAD_DIM:D_GROUP + (h + 1) * HEAD_DIM]
            v_new = row[:, 2 * D_GROUP + h * HEAD_DIM:2 * D_GROUP + (h + 1) * HEAD_DIM]
            q_col, k_col, v_col = to_col(q), to_col(k_new), to_col(v_new)
            s_tiles = [None] * n_tiles
            nxt = jnp.broadcast_to(k_col, (HEAD_DIM, LANES))
            for j in reversed(range(n_tiles)):
                ts = slice(j * LANES, (j + 1) * LANES)
                kt = c_ref[bi, 0, h, :, ts]
                s = jnp.sum(kt * q_col, axis=0, keepdims=True) * scale
                s_tiles[j] = jnp.where(on_grid, s, NEG_INF)
                rolled = pltpu.roll(kt, LANES - 1, axis=1)
                cout_ref[bi, 0, h, :, ts] = jnp.where(last_lane, nxt, rolled)
                nxt = rolled
            s_new = jnp.sum(k_new * q, axis=1, keepdims=True) * scale
            m_row = s_tiles[0]
            for j in range(1, n_tiles):
                m_row = jnp.maximum(m_row, s_tiles[j])
            m = jnp.maximum(jnp.max(m_row, axis=1, keepdims=True), s_new)
            p_new = jnp.exp(s_new - m)
            l_row = jnp.zeros((1, LANES), F32)
            acc = jnp.zeros((HEAD_DIM, LANES), F32)
            nxt = jnp.broadcast_to(v_col, (HEAD_DIM, LANES))
            for j in reversed(range(n_tiles)):
                ts = slice(j * LANES, (j + 1) * LANES)
                p = jnp.exp(s_tiles[j] - m)
                l_row = l_row + p
                vt = c_ref[bi, 1, h, :, ts]
                acc = acc + vt * p
                rolled = pltpu.roll(vt, LANES - 1, axis=1)
                cout_ref[bi, 1, h, :, ts] = jnp.where(last_lane, nxt, rolled)
                nxt = rolled
            l = jnp.sum(l_row, axis=1, keepdims=True) + p_new
            o_col = (jnp.sum(acc, axis=1, keepdims=True) + p_new * v_col) / l
            parts += [jnp.sum(eye * o_col, axis=0, keepdims=True), jnp.broadcast_to(m + jnp.log(l), (1, HEAD_DIM))]
        o_ref[bi] = jnp.concatenate(parts, axis=1)


def _cache_attn(qkv, cache_t, dil):
    window = cache_t.shape[-1]
    nb = CACHE_BATCH_PER_STEP[window]
    blk = (nb, 2, HEADS_PER_GROUP, HEAD_DIM, window)
    return pl.pallas_call(
        functools.partial(_cache_attn_kernel, dil=dil, nb=nb),
        grid=(DEC_BATCH // nb,),
        in_specs=[pl.BlockSpec((nb, 1, 3 * D_GROUP), lambda i: (i, 0, 0)),
                  pl.BlockSpec(blk, lambda i: (i, 0, 0, 0, 0))],
        out_specs=[pl.BlockSpec((nb, 1, HEADS_PER_GROUP * LANES), lambda i: (i, 0, 0)),
                   pl.BlockSpec(blk, lambda i: (i, 0, 0, 0, 0))],
        out_shape=[jax.ShapeDtypeStruct((DEC_BATCH, 1, HEADS_PER_GROUP * LANES), F32),
                   jax.ShapeDtypeStruct(cache_t.shape, F32)],
        compiler_params=_params("parallel"),
        name=f"cache_attn_w{window}",
    )(qkv, cache_t)


MERGE_TM = 128


def _merge_kernel(x_ref, y_ref, bonus_ref, g_ref, a0_ref, a1_ref, a2_ref, pg_ref, bg_ref,
                  lnw_ref, lnb_ref, e_ref, wbr_ref, wba0_ref, wba1_ref, wba2_ref, wo_ref, o_ref):
    e = e_ref[...]
    inv_n = 1.0 / HEAD_DIM
    y = y_ref[...]
    d = y - _segsum(y, e) * inv_n
    var = _segsum(d * d, e) * inv_n
    yn = d * lax.rsqrt(var + RWKV_LN_EPS) * lnw_ref[...] + lnb_ref[...]
    o_a = ((yn + bonus_ref[...]) * g_ref[...]).astype(BF16)
    br_a = jnp.dot(o_a, wbr_ref[...], preferred_element_type=F32)
    a = (a0_ref[...], a1_ref[...], a2_ref[...])
    m = jnp.maximum(jnp.maximum(a[0], a[1]), a[2])
    ex = [jnp.exp(t - m) for t in a]
    inv = 1.0 / (ex[0] + ex[1] + ex[2])
    br_b = jnp.zeros_like(br_a)
    for grp, wb_ref in enumerate((wba0_ref, wba1_ref, wba2_ref)):
        alpha = ex[grp] * inv
        parts = []
        for h in range(HEADS_PER_GROUP):
            hs = slice(h * LANES, (h + 1) * LANES)
            parts.append(a[grp][:, hs] * pltpu.roll(alpha[:, hs], HEAD_DIM, axis=1))
        xg = jnp.concatenate(parts, axis=1).astype(BF16)
        br_b = br_b + jnp.dot(xg, wb_ref[...], preferred_element_type=F32)
    gates = jax.nn.sigmoid(pg_ref[...] + bg_ref[...])
    merged = (gates[:, 0:D_MODEL] * br_a + gates[:, D_MODEL:] * br_b).astype(BF16)
    o_ref[...] = x_ref[...] + jnp.dot(merged, wo_ref[...], preferred_element_type=F32)


def _merge(x, y, bonus, g, att, p_gate, b_gate, ln_w, ln_b, e_heads, w_br, w_ba, w_o):
    tile = lambda n: pl.BlockSpec((MERGE_TM, n), lambda i: (i, 0))
    row = lambda n: pl.BlockSpec((1, n), lambda i: (0, 0))
    att_w = HEADS_PER_GROUP * LANES
    return pl.pallas_call(
        _merge_kernel,
        grid=(M_ALL // MERGE_TM,),
        in_specs=[tile(D_MODEL), tile(D_RWKV), tile(D_RWKV), tile(D_RWKV),
                  tile(att_w), tile(att_w), tile(att_w), tile(2 * D_MODEL), row(2 * D_MODEL),
                  row(D_RWKV), row(D_RWKV), _const_spec(e_heads.shape), _const_spec(w_br.shape),
                  _const_spec(w_ba[0].shape), _const_spec(w_ba[1].shape), _const_spec(w_ba[2].shape),
                  _const_spec(w_o.shape)],
        out_specs=tile(D_MODEL),
        out_shape=jax.ShapeDtypeStruct((M_ALL, D_MODEL), F32),
        compiler_params=_params("parallel"),
        name="merge",
    )(x, y, bonus, g, att[0], att[1], att[2], p_gate, b_gate, ln_w, ln_b, e_heads, w_br,
      w_ba[0], w_ba[1], w_ba[2], w_o)


def _split_bf16(w):
    hi = w.astype(BF16)
    return hi, (w - hi.astype(F32)).astype(BF16)


def _head_ones(n_heads):
    return jnp.kron(jnp.eye(n_heads, dtype=F32), jnp.ones((HEAD_DIM, HEAD_DIM), F32)).astype(BF16)


def _rotary_tables():
    half = HEAD_DIM // 2
    inv_freq = ROPE_THETA ** (-jnp.arange(half, dtype=F32) / half)
    pos = jnp.concatenate([jnp.tile(jnp.arange(SEQ), BATCH), jnp.full((DEC_BATCH,), PAST_LEN)]).astype(F32)
    ang = pos[:, None] * inv_freq[None, :]
    cos, sin = jnp.cos(ang), jnp.sin(ang)
    cos = jnp.concatenate([cos, cos] * (LANES // HEAD_DIM), axis=1)
    sin = jnp.concatenate([-sin, sin] * (LANES // HEAD_DIM), axis=1)
    return cos, sin


def kernel(x_prompt, x_sample, state_rwkv_shift, state_rwkv_wkv, cache_att_w128, cache_att_w512, cache_att_w2048,
           ffn1_norm, ffn1_w_in, ffn1_w_out, mix_norm, w_in, b_gate, rwkv_mu, rwkv_w0, rwkv_w2, rwkv_a0, rwkv_a2,
           rwkv_g2, rwkv_k_k, rwkv_k_a, rwkv_r_k, rwkv_ln_w, rwkv_ln_b, w_branch_rwkv, attn_q_norm, attn_k_norm,
           w_branch_attn, w_out, ffn2_norm, ffn2_w_in, ffn2_w_out):
    caches = (cache_att_w128, cache_att_w512, cache_att_w2048)
    x = jnp.concatenate([x_prompt.reshape(M_PROMPT, D_MODEL), x_sample.reshape(DEC_BATCH, D_MODEL)], axis=0)

    w_mix = w_in[0]
    pad = RWKV_PROJ_PAD - D_RWKV_PROJ
    w_rwkv = jnp.pad(w_mix[:, :D_RWKV_PROJ], ((0, 0), (0, pad))).astype(BF16)
    w_att = w_mix[:, D_RWKV_PROJ:D_RWKV_PROJ + 3 * D_ATT].astype(BF16)
    w_gate = w_mix[:, D_RWKV_PROJ + 3 * D_ATT:].astype(BF16)
    mu = jnp.pad(rwkv_mu, ((0, 0), (0, pad)))
    state_shift = jnp.pad(state_rwkv_shift[0], ((0, 0), (0, pad)))
    zeros = jnp.zeros((D_DECAY_LORA, D_RWKV), F32)
    w_lora = jnp.concatenate([jnp.concatenate([rwkv_w2[0], zeros], axis=1),
                              jnp.concatenate([zeros, rwkv_a2[0]], axis=1)], axis=0)
    w_glora = jnp.pad(rwkv_g2[0], ((0, 2 * LANES - D_GATE_LORA), (0, 0)))
    wl_hi, wl_lo = _split_bf16(w_lora)
    wg_hi, wg_lo = _split_bf16(w_glora)
    e_heads = _head_ones(RWKV_HEADS)
    e_att = _head_ones(ATT_HEADS)
    cos, sin = _rotary_tables()
    q_gain = jnp.tile(attn_q_norm, (1, ATT_HEADS))
    k_gain = jnp.tile(attn_k_norm, (1, ATT_HEADS))
    wba = w_branch_attn[0].reshape(len(ATT_GROUPS), HEADS_PER_GROUP, HEAD_DIM, D_MODEL)
    wba = jnp.pad(wba, ((0, 0), (0, 0), (0, LANES - HEAD_DIM), (0, 0))).astype(BF16)
    wba = wba.reshape(len(ATT_GROUPS), HEADS_PER_GROUP * LANES, D_MODEL)

    x1 = _ffn(x, ffn1_norm, ffn1_w_in[0].astype(BF16), ffn1_w_out[0].astype(BF16))

    p_rwkv = _normproj(x1, mix_norm, w_rwkv, 512, "proj_rwkv")
    p_att = _normproj(x1, mix_norm, w_att, 768, "proj_att")
    p_gate = _normproj(x1, mix_norm, w_gate, 512, "proj_gate")

    kw, v_r, bonus, g_r = _rwkv_prep(p_rwkv, state_shift, mu, rwkv_w0, rwkv_a0, rwkv_k_k, rwkv_k_a,
                                     rwkv_r_k.reshape(1, D_RWKV), wl_hi, wl_lo, wg_hi, wg_lo, e_heads)
    kw_t = kw[:, :M_PROMPT].reshape(5, BATCH, SEQ, RWKV_HEADS, 2, K_HALF)
    kw_t = kw_t.transpose(2, 0, 5, 4, 1, 3).reshape(SEQ, 5, K_HALF, LANES)
    v_t = v_r[:M_PROMPT].reshape(BATCH, SEQ, RWKV_HEADS, HEAD_DIM).transpose(1, 3, 0, 2).reshape(SEQ, HEAD_DIM, BH)
    v_t = jnp.concatenate([v_t, v_t], axis=-1)
    y_t, s_t = _rwkv_seq(kw_t, v_t)
    y_p = y_t[:, :, :BH].reshape(SEQ, HEAD_DIM, BATCH, RWKV_HEADS).transpose(2, 0, 3, 1).reshape(M_PROMPT, D_RWKV)
    wkv_p = s_t.reshape(K_HALF, HEAD_DIM, 2, BATCH, RWKV_HEADS).transpose(3, 4, 1, 2, 0)
    wkv_p = wkv_p.reshape(BATCH, RWKV_HEADS, HEAD_DIM, HEAD_DIM)
    y_s_t, wkv_s_t = _rwkv_step(kw[:, M_PROMPT:].transpose(0, 2, 1), v_r[M_PROMPT:].T,
                                state_rwkv_wkv[0].transpose(1, 2, 3, 0))
    wkv_s = wkv_s_t.transpose(3, 0, 1, 2)
    y_r = jnp.concatenate([y_p, y_s_t.T], axis=0)

    qk = _att_prep(p_att, cos, sin, q_gain, k_gain, e_att)
    att_p, att_s, kv_p, kv_s = [], [], [], []
    for g, (window, dil) in enumerate(ATT_GROUPS):
        ls = SEQ // dil

        def seqs(t):
            t = t.reshape(BATCH, ls, dil, HEADS_PER_GROUP, HEAD_DIM).transpose(0, 2, 3, 1, 4)
            return t.reshape(BATCH * dil * HEADS_PER_GROUP, ls, HEAD_DIM)

        q_g = qk[:M_PROMPT, g * D_GROUP:(g + 1) * D_GROUP]
        k_g = qk[:M_PROMPT, D_ATT + g * D_GROUP:D_ATT + (g + 1) * D_GROUP]
        v_g = p_att[:M_PROMPT, 2 * D_ATT + g * D_GROUP:2 * D_ATT + (g + 1) * D_GROUP]
        o = _att_window(seqs(q_g), seqs(k_g), seqs(v_g))
        o = o.reshape(BATCH, dil, HEADS_PER_GROUP, ls, LANES).transpose(0, 3, 1, 2, 4)
        att_p.append(o.reshape(M_PROMPT, HEADS_PER_GROUP * LANES))
        keep = min(window, SEQ)
        kv = jnp.stack([k_g.reshape(BATCH, SEQ, HEADS_PER_GROUP, HEAD_DIM),
                        v_g.reshape(BATCH, SEQ, HEADS_PER_GROUP, HEAD_DIM)], axis=2)
        kv_p.append(kv[:, SEQ - keep:][None])
        qkv_s = jnp.concatenate([qk[M_PROMPT:, g * D_GROUP:(g + 1) * D_GROUP],
                                 qk[M_PROMPT:, D_ATT + g * D_GROUP:D_ATT + (g + 1) * D_GROUP],
                                 p_att[M_PROMPT:, 2 * D_ATT + g * D_GROUP:2 * D_ATT + (g + 1) * D_GROUP]], axis=1)
        o_s, cache_new = _cache_attn(qkv_s.reshape(DEC_BATCH, 1, 3 * D_GROUP), caches[g][0].transpose(0, 2, 3, 4, 1), dil)
        att_s.append(o_s.reshape(DEC_BATCH, HEADS_PER_GROUP * LANES))
        kv_s.append(cache_new.transpose(0, 4, 1, 2, 3)[None])
    att = [jnp.concatenate([a_p, a_s], axis=0) for a_p, a_s in zip(att_p, att_s)]

    x2 = _merge(x1, y_r, bonus, g_r, att, p_gate, b_gate, rwkv_ln_w, rwkv_ln_b,
                e_heads, w_branch_rwkv[0].astype(BF16), wba, w_out[0].astype(BF16))
    y = _ffn(x2, ffn2_norm, ffn2_w_in[0].astype(BF16), ffn2_w_out[0].astype(BF16))

    shift_p = p_rwkv[SEQ - 1:M_PROMPT:SEQ, :D_RWKV_PROJ][None]
    shift_s = p_rwkv[M_PROMPT:, :D_RWKV_PROJ][None]
    return (y[:M_PROMPT].reshape(BATCH, SEQ, D_MODEL), y[M_PROMPT:].reshape(DEC_BATCH, 1, D_MODEL),
            shift_p, wkv_p[None], kv_p[0], kv_p[1], kv_p[2],
            shift_s, wkv_s[None], kv_s[0], kv_s[1], kv_s[2])
```

```python
import functools

import jax
import jax.numpy as jnp
from jax import lax
from jax.experimental import pallas as pl
from jax.experimental.pallas import tpu as pltpu

D_MODEL = 2048
BATCH = 4
SEQ = 2048
DEC_BATCH = 128
PAST_LEN = 2048
M_PROMPT = BATCH * SEQ
M_ALL = M_PROMPT + DEC_BATCH

RWKV_HEADS = 16
HEAD_DIM = 64
D_RWKV = RWKV_HEADS * HEAD_DIM
D_DECAY_LORA = 64
D_ICLR_LORA = 64
D_GATE_LORA = 160
D_RWKV_PROJ = 3 * D_RWKV + D_DECAY_LORA + D_ICLR_LORA + D_GATE_LORA
RWKV_PROJ_PAD = 3584
RWKV_LN_EPS = 64e-5

ATT_GROUPS = ((128, 1), (512, 4), (2048, 16))
HEADS_PER_GROUP = 4
ATT_HEADS = HEADS_PER_GROUP * len(ATT_GROUPS)
D_ATT = ATT_HEADS * HEAD_DIM
D_GROUP = HEADS_PER_GROUP * HEAD_DIM
N_KEYS_PAST = 128
ROPE_THETA = 10000.0
D_FF = 5632
NORM_EPS = 1e-6
NEG_INF = -1e30

LANES = 128
VMEM_LIMIT = 56 * 1024 * 1024

F32 = jnp.float32
BF16 = jnp.bfloat16


def _params(*sem):
    return pltpu.CompilerParams(dimension_semantics=sem, vmem_limit_bytes=VMEM_LIMIT)


def _const_spec(shape):
    nd = len(shape)
    return pl.BlockSpec(shape, lambda *_: (0,) * nd, pipeline_mode=pl.Buffered(1))


def _rms_to_bf16(x, g):
    ms = jnp.mean(x * x, axis=-1, keepdims=True)
    return (x * lax.rsqrt(ms + NORM_EPS) * g).astype(BF16)


def _segsum(x, e):
    hi = x.astype(BF16)
    lo = (x - hi.astype(F32)).astype(BF16)
    return (jnp.dot(hi, e, preferred_element_type=F32) + jnp.dot(lo, e, preferred_element_type=F32))


def _dot_split(x, w_hi, w_lo):
    hi = x.astype(BF16)
    lo = (x - hi.astype(F32)).astype(BF16)
    return (jnp.dot(hi, w_hi, preferred_element_type=F32) + jnp.dot(lo, w_hi, preferred_element_type=F32)
            + jnp.dot(hi, w_lo, preferred_element_type=F32))


FFN_TM = 640
FFN_TF = 512


def _ffn_kernel(x_ref, g_ref, wg_ref, wu_ref, wo_ref, *rest, emit_norm):
    if emit_norm:
        gn_ref, o_ref, hn_ref, h_ref = rest
    else:
        o_ref, h_ref = rest
    j = pl.program_id(1)

    @pl.when(j == 0)
    def _():
        h_ref[...] = _rms_to_bf16(x_ref[...], g_ref[...])
        o_ref[...] = jnp.zeros_like(o_ref)

    h = h_ref[...]
    gate = jnp.dot(h, wg_ref[...], preferred_element_type=F32)
    up = jnp.dot(h, wu_ref[...], preferred_element_type=F32)
    act = (gate * jax.nn.sigmoid(gate) * up).astype(BF16)
    o_ref[...] += jnp.dot(act, wo_ref[...], preferred_element_type=F32)

    @pl.when(j == pl.num_programs(1) - 1)
    def _():
        out = x_ref[...] + 0.5 * o_ref[...]
        o_ref[...] = out
        if emit_norm:
            hn_ref[...] = _rms_to_bf16(out, gn_ref[...])


def _ffn(x, g, w_in, w_out, next_norm=None):
    m = x.shape[0]
    nj = D_FF // FFN_TF
    emit_norm = next_norm is not None
    row = pl.BlockSpec((1, D_MODEL), lambda i, j: (0, 0))
    tile = pl.BlockSpec((FFN_TM, D_MODEL), lambda i, j: (i, 0))
    in_specs = [tile, row,
                pl.BlockSpec((D_MODEL, FFN_TF), lambda i, j: (0, j)),
                pl.BlockSpec((D_MODEL, FFN_TF), lambda i, j: (0, j + nj)),
                pl.BlockSpec((FFN_TF, D_MODEL), lambda i, j: (j, 0))]
    args = [x, g, w_in, w_in, w_out]
    out_specs, out_shape = tile, jax.ShapeDtypeStruct((m, D_MODEL), F32)
    if emit_norm:
        in_specs.append(row)
        args.append(next_norm)
        out_specs = [tile, tile]
        out_shape = [out_shape, jax.ShapeDtypeStruct((m, D_MODEL), BF16)]
    return pl.pallas_call(
        functools.partial(_ffn_kernel, emit_norm=emit_norm),
        grid=(m // FFN_TM, nj),
        in_specs=in_specs,
        out_specs=out_specs,
        out_shape=out_shape,
        scratch_shapes=[pltpu.VMEM((FFN_TM, D_MODEL), BF16)],
        compiler_params=_params("parallel", "arbitrary"),
        name="ffn",
    )(*args)


PROJ_TM = 1664


def _proj_kernel(h_ref, w_ref, o_ref):
    o_ref[...] = jnp.dot(h_ref[...], w_ref[...], preferred_element_type=F32)


def _proj(h, w, tn, name):
    m, n = h.shape[0], w.shape[1]
    return pl.pallas_call(
        _proj_kernel,
        grid=(m // PROJ_TM, n // tn),
        in_specs=[
            pl.BlockSpec((PROJ_TM, D_MODEL), lambda i, j: (i, 0)),
            pl.BlockSpec((D_MODEL, tn), lambda i, j: (0, j)),
        ],
        out_specs=pl.BlockSpec((PROJ_TM, tn), lambda i, j: (i, j)),
        out_shape=jax.ShapeDtypeStruct((m, n), F32),
        compiler_params=_params("parallel", "arbitrary"),
        name=name,
    )(h, w)


PREP_TM = 128
PREP_TILES_PER_SEQ = SEQ // PREP_TM
PREP_SAMPLE_TILE = M_PROMPT // PREP_TM
KW_R, KW_DECAY, KW_K, KW_KK, KW_B = range(5)


def _prep_tile(i):
    return jnp.where(i == 0, PREP_SAMPLE_TILE, i - 1)


def _rwkv_prep_kernel(p_ref, tail_ref, state_ref, mu_ref, w0_ref, a0_ref, kk_ref, ka_ref, rk_ref,
                      wl_hi_ref, wl_lo_ref, wg_hi_ref, wg_lo_ref, e_ref,
                      kw_ref, v_ref, kws_ref, vs_ref, bonus_ref, g_ref):
    i = pl.program_id(0)
    is_sample = i == 0
    p = p_ref[...]
    tail = jnp.where((i - 1) % PREP_TILES_PER_SEQ == 0, 0.0, tail_ref[7:8, :])
    row = lax.broadcasted_iota(jnp.int32, p.shape, 0)
    prev = jnp.where(row == 0, tail, pltpu.roll(p, 1, axis=0))
    prev = jnp.where(is_sample, state_ref[...], prev)
    pm = p + (prev - p) * mu_ref[...]

    r = pm[:, 0:D_RWKV]
    k = pm[:, D_RWKV:2 * D_RWKV]
    v = pm[:, 2 * D_RWKV:3 * D_RWKV]
    lora = pm[:, 3 * D_RWKV:]
    lane = lax.broadcasted_iota(jnp.int32, (p.shape[0], LANES), 1)
    x_wa = lora[:, 0:LANES]
    x_wa = jnp.where(lane < D_DECAY_LORA, jnp.tanh(x_wa), x_wa)
    x_g = jax.nn.sigmoid(lora[:, LANES:3 * LANES])
    wa = _dot_split(x_wa, wl_hi_ref[...], wl_lo_ref[...])
    g = _dot_split(x_g, wg_hi_ref[...], wg_lo_ref[...])

    z = -(w0_ref[...] + wa[:, 0:D_RWKV])
    softplus = jnp.maximum(z, 0.0) + jnp.log(1.0 + jnp.exp(-jnp.abs(z)))
    decay = jnp.exp(-jnp.exp(-softplus - 0.5))
    a = jax.nn.sigmoid(a0_ref[...] + wa[:, D_RWKV:])
    e = e_ref[...]
    kk = k * kk_ref[...]
    kk = kk / jnp.maximum(jnp.sqrt(_segsum(kk * kk, e)), 1e-12)
    k_hat = k * (1.0 + (a - 1.0) * ka_ref[...])
    kw = (r, decay, k_hat, kk, kk * a)
    bonus_ref[...] = _segsum(r * k_hat * rk_ref[...], e) * v
    g_ref[...] = g

    @pl.when(i > 0)
    def _():
        for idx in range(5):
            kw_ref[idx] = kw[idx]
        v_ref[...] = v

    @pl.when(is_sample)
    def _():
        for idx in range(5):
            kws_ref[idx] = kw[idx].T
        vs_ref[...] = v.T


def _rwkv_prep(p_rwkv, state_shift, mu, w0, a0, k_k, k_a, r_k, wl_hi, wl_lo, wg_hi, wg_lo, e_heads):
    n_tiles = M_ALL // PREP_TM
    row = lambda n: pl.BlockSpec((1, n), lambda i: (0, 0))
    prompt_tile = lambda i: jnp.maximum(i - 1, 0)
    return pl.pallas_call(
        _rwkv_prep_kernel,
        grid=(n_tiles,),
        in_specs=[
            pl.BlockSpec((PREP_TM, RWKV_PROJ_PAD), lambda i: (_prep_tile(i), 0)),
            pl.BlockSpec((8, RWKV_PROJ_PAD), lambda i: (jnp.maximum((i - 1) * (PREP_TM // 8) - 1, 0), 0)),
            pl.BlockSpec((DEC_BATCH, RWKV_PROJ_PAD), lambda i: (0, 0)),
            row(RWKV_PROJ_PAD), row(D_RWKV), row(D_RWKV), row(D_RWKV), row(D_RWKV), row(D_RWKV),
            _const_spec(wl_hi.shape), _const_spec(wl_lo.shape),
            _const_spec(wg_hi.shape), _const_spec(wg_lo.shape),
            _const_spec(e_heads.shape),
        ],
        out_specs=[
            pl.BlockSpec((5, PREP_TM, D_RWKV), lambda i: (0, prompt_tile(i), 0)),
            pl.BlockSpec((PREP_TM, D_RWKV), lambda i: (prompt_tile(i), 0)),
            pl.BlockSpec((5, D_RWKV, DEC_BATCH), lambda i: (0, 0, 0)),
            pl.BlockSpec((D_RWKV, DEC_BATCH), lambda i: (0, 0)),
            pl.BlockSpec((PREP_TM, D_RWKV), lambda i: (_prep_tile(i), 0)),
            pl.BlockSpec((PREP_TM, D_RWKV), lambda i: (_prep_tile(i), 0)),
        ],
        out_shape=[
            jax.ShapeDtypeStruct((5, M_PROMPT, D_RWKV), F32),
            jax.ShapeDtypeStruct((M_PROMPT, D_RWKV), F32),
            jax.ShapeDtypeStruct((5, D_RWKV, DEC_BATCH), F32),
            jax.ShapeDtypeStruct((D_RWKV, DEC_BATCH), F32),
            jax.ShapeDtypeStruct((M_ALL, D_RWKV), F32),
            jax.ShapeDtypeStruct((M_ALL, D_RWKV), F32),
        ],
        compiler_params=_params("arbitrary"),
        name="rwkv_prep",
    )(p_rwkv, p_rwkv, state_shift, mu, w0, a0, k_k, k_a, r_k, wl_hi, wl_lo, wg_hi, wg_lo, e_heads)


SEQ_TB = 64
K_HALF = HEAD_DIM // 2
BH = BATCH * RWKV_HEADS


def _rwkv_seq_kernel(kw_ref, v_ref, y_ref, sout_ref, s_ref):
    @pl.when(pl.program_id(0) == 0)
    def _():
        s_ref[...] = jnp.zeros_like(s_ref)

    sk = jnp.zeros((HEAD_DIM, LANES), F32)
    for kp in range(K_HALF):
        sk = sk + s_ref[kp] * kw_ref[0, KW_KK, pl.ds(kp, 1), :]

    n_vh = 2
    vh_rows = HEAD_DIM // n_vh

    def step(t, sk):
        t_next = jnp.minimum(t + 1, SEQ_TB - 1)
        sk_next = []
        for vh in range(n_vh):
            vs = pl.ds(vh * vh_rows, vh_rows)
            sk_h = sk[vh * vh_rows:(vh + 1) * vh_rows]
            sa = -(sk_h + pltpu.roll(sk_h, BH, axis=1))
            vt = v_ref[t, vs, :]
            y = jnp.zeros((vh_rows, LANES), F32)
            nxt = jnp.zeros((vh_rows, LANES), F32)
            for kp in range(K_HALF):
                s_new = (s_ref[kp, vs, :] * kw_ref[t, KW_DECAY, pl.ds(kp, 1), :]
                         + sa * kw_ref[t, KW_B, pl.ds(kp, 1), :]
                         + vt * kw_ref[t, KW_K, pl.ds(kp, 1), :])
                s_ref[kp, vs, :] = s_new
                y = y + s_new * kw_ref[t, KW_R, pl.ds(kp, 1), :]
                nxt = nxt + s_new * kw_ref[t_next, KW_KK, pl.ds(kp, 1), :]
            y_ref[t, vs, :] = y + pltpu.roll(y, BH, axis=1)
            sk_next.append(nxt)
        return jnp.concatenate(sk_next, axis=0)

    lax.fori_loop(0, SEQ_TB, step, sk)

    @pl.when(pl.program_id(0) == pl.num_programs(0) - 1)
    def _():
        sout_ref[...] = s_ref[...]


def _rwkv_seq(kw_t, v_t):
    return pl.pallas_call(
        _rwkv_seq_kernel,
        grid=(SEQ // SEQ_TB,),
        in_specs=[
            pl.BlockSpec((SEQ_TB, 5, K_HALF, LANES), lambda i: (i, 0, 0, 0)),
            pl.BlockSpec((SEQ_TB, HEAD_DIM, LANES), lambda i: (i, 0, 0)),
        ],
        out_specs=[
            pl.BlockSpec((SEQ_TB, HEAD_DIM, LANES), lambda i: (i, 0, 0)),
            pl.BlockSpec((K_HALF, HEAD_DIM, LANES), lambda i: (0, 0, 0)),
        ],
        out_shape=[
            jax.ShapeDtypeStruct((SEQ, HEAD_DIM, LANES), F32),
            jax.ShapeDtypeStruct((K_HALF, HEAD_DIM, LANES), F32),
        ],
        scratch_shapes=[pltpu.VMEM((K_HALF, HEAD_DIM, LANES), F32)],
        compiler_params=_params("arbitrary"),
        name="rwkv_seq",
    )(kw_t, v_t)


def _rwkv_step_kernel(kw_ref, v_ref, s_ref, y_ref, sout_ref):
    def body(vi, carry):
        s = s_ref[0, vi]
        sa = -jnp.sum(s * kw_ref[KW_KK], axis=0, keepdims=True)
        s_new = s * kw_ref[KW_DECAY] + sa * kw_ref[KW_B] + v_ref[pl.ds(vi, 1), :] * kw_ref[KW_K]
        sout_ref[0, vi] = s_new
        y_ref[pl.ds(vi, 1), :] = jnp.sum(s_new * kw_ref[KW_R], axis=0, keepdims=True)
        return carry

    lax.fori_loop(0, HEAD_DIM, body, 0)


def _rwkv_step(kw_t, v_t, state_t):
    return pl.pallas_call(
        _rwkv_step_kernel,
        grid=(RWKV_HEADS,),
        in_specs=[
            pl.BlockSpec((5, HEAD_DIM, DEC_BATCH), lambda h: (0, h, 0)),
            pl.BlockSpec((HEAD_DIM, DEC_BATCH), lambda h: (h, 0)),
            pl.BlockSpec((1, HEAD_DIM, HEAD_DIM, DEC_BATCH), lambda h: (h, 0, 0, 0)),
        ],
        out_specs=[
            pl.BlockSpec((HEAD_DIM, DEC_BATCH), lambda h: (h, 0)),
            pl.BlockSpec((1, HEAD_DIM, HEAD_DIM, DEC_BATCH), lambda h: (h, 0, 0, 0)),
        ],
        out_shape=[
            jax.ShapeDtypeStruct((D_RWKV, DEC_BATCH), F32),
            jax.ShapeDtypeStruct((RWKV_HEADS, HEAD_DIM, HEAD_DIM, DEC_BATCH), F32),
        ],
        compiler_params=_params("parallel"),
        name="rwkv_step",
    )(kw_t, v_t, state_t)


ATT_PREP_TM = 640


def _swap_halves(x):
    lane = lax.broadcasted_iota(jnp.int32, x.shape, 1)
    first = (lane % HEAD_DIM) < HEAD_DIM // 2
    return jnp.where(first, pltpu.roll(x, LANES - HEAD_DIM // 2, axis=1), pltpu.roll(x, HEAD_DIM // 2, axis=1))


def _att_prep_kernel(p_ref, cos_ref, sin_ref, qg_ref, kg_ref, e_ref, o0_ref, o1_ref, o2_ref):
    cos = cos_ref[...]
    sin = sin_ref[...]
    e = e_ref[...]
    o_refs = (o0_ref, o1_ref, o2_ref)
    tiles_per_group = D_GROUP // LANES
    for part, g_ref in ((0, qg_ref), (1, kg_ref)):
        x = p_ref[:, part * D_ATT:(part + 1) * D_ATT]
        ms = _segsum(x * x, e) * (1.0 / HEAD_DIM)
        xn = x * lax.rsqrt(ms + NORM_EPS) * g_ref[...]
        for c in range(D_ATT // LANES):
            xc = xn[:, c * LANES:(c + 1) * LANES]
            dst = part * D_GROUP + (c % tiles_per_group) * LANES
            o_refs[c // tiles_per_group][:, dst:dst + LANES] = xc * cos + _swap_halves(xc) * sin
    for grp in range(len(ATT_GROUPS)):
        o_refs[grp][:, 2 * D_GROUP:] = p_ref[:, 2 * D_ATT + grp * D_GROUP:2 * D_ATT + (grp + 1) * D_GROUP]


def _att_prep(p_att, cos, sin, q_gain, k_gain, e_att):
    out = jax.ShapeDtypeStruct((M_ALL, 3 * D_GROUP), F32)
    return pl.pallas_call(
        _att_prep_kernel,
        grid=(M_ALL // ATT_PREP_TM,),
        in_specs=[
            pl.BlockSpec((ATT_PREP_TM, 3 * D_ATT), lambda i: (i, 0)),
            pl.BlockSpec((ATT_PREP_TM, LANES), lambda i: (i, 0)),
            pl.BlockSpec((ATT_PREP_TM, LANES), lambda i: (i, 0)),
            pl.BlockSpec((1, D_ATT), lambda i: (0, 0)),
            pl.BlockSpec((1, D_ATT), lambda i: (0, 0)),
            _const_spec(e_att.shape),
        ],
        out_specs=[pl.BlockSpec((ATT_PREP_TM, 3 * D_GROUP), lambda i: (i, 0))] * 3,
        out_shape=[out] * 3,
        compiler_params=_params("parallel"),
        name="att_prep",
    )(p_att, cos, sin, q_gain, k_gain, e_att)


ATT_TQ = 128
ATT_NSB = 8


def _att_window_kernel(prev_ref, cur_ref, o_ref):
    qi = pl.program_id(1)
    prev = prev_ref[...]
    cur = cur_ref[...]
    q = cur[:, :, 0:HEAD_DIM].astype(BF16)
    k = jnp.concatenate([prev[:, :, HEAD_DIM:2 * HEAD_DIM], cur[:, :, HEAD_DIM:2 * HEAD_DIM]], axis=1).astype(BF16)
    v = jnp.concatenate([prev[:, :, 2 * HEAD_DIM:], cur[:, :, 2 * HEAD_DIM:]], axis=1).astype(BF16)
    s = jnp.einsum("sqd,skd->sqk", q, k, preferred_element_type=F32) * (HEAD_DIM ** -0.5)
    qpos = lax.broadcasted_iota(jnp.int32, s.shape, 1) + ATT_TQ
    kpos = lax.broadcasted_iota(jnp.int32, s.shape, 2)
    dist = qpos - kpos
    valid = (dist >= 0) & (dist <= N_KEYS_PAST) & ((kpos >= ATT_TQ) | (qi > 0))
    s = jnp.where(valid, s, NEG_INF)
    m = jnp.max(s, axis=-1, keepdims=True)
    p = jnp.exp(s - m)
    l = jnp.sum(p, axis=-1, keepdims=True)
    lse = m + jnp.log(l)
    pn = (p / l).astype(BF16)
    o = jnp.einsum("sqk,skd->sqd", pn, v, preferred_element_type=F32)
    o_ref[:, :, 0:HEAD_DIM] = o
    o_ref[:, :, HEAD_DIM:] = jnp.broadcast_to(lse, o.shape)


def _att_window(qkv):
    ns, ls, _ = qkv.shape
    blk = (ATT_NSB, ATT_TQ, 3 * HEAD_DIM)
    cur = lambda s, i: (s, i, 0)
    prv = lambda s, i: (s, jnp.maximum(i - 1, 0), 0)
    return pl.pallas_call(
        _att_window_kernel,
        grid=(ns // ATT_NSB, ls // ATT_TQ),
        in_specs=[pl.BlockSpec(blk, prv), pl.BlockSpec(blk, cur)],
        out_specs=pl.BlockSpec((ATT_NSB, ATT_TQ, 2 * HEAD_DIM), cur),
        out_shape=jax.ShapeDtypeStruct((ns, ls, 2 * HEAD_DIM), F32),
        compiler_params=_params("parallel", "parallel"),
        name="att_window",
    )(qkv, qkv)


CACHE_BATCH_PER_STEP = {128: 8, 512: 4, 2048: 1}


def _cache_attn_kernel(qkv_ref, c_ref, o_ref, cout_ref, *, dil, nb):
    n_tiles = c_ref.shape[-1] // LANES
    scale = HEAD_DIM ** -0.5
    r_i = lax.broadcasted_iota(jnp.int32, (HEAD_DIM, HEAD_DIM), 0)
    c_i = lax.broadcasted_iota(jnp.int32, (HEAD_DIM, HEAD_DIM), 1)
    eye = (r_i == c_i).astype(F32)
    on_grid = (lax.broadcasted_iota(jnp.int32, (1, LANES), 1) % dil) == 0
    last_lane = lax.broadcasted_iota(jnp.int32, (HEAD_DIM, LANES), 1) == LANES - 1

    def to_col(rows):
        return jnp.sum(eye * rows, axis=2, keepdims=True)

    rows = qkv_ref[...]
    parts = []
    for h in range(HEADS_PER_GROUP):
        q = rows[:, :, h * HEAD_DIM:(h + 1) * HEAD_DIM]
        k_new = rows[:, :, D_GROUP + h * HEAD_DIM:D_GROUP + (h + 1) * HEAD_DIM]
        v_new = rows[:, :, 2 * D_GROUP + h * HEAD_DIM:2 * D_GROUP + (h + 1) * HEAD_DIM]
        q_col, k_col, v_col = to_col(q), to_col(k_new), to_col(v_new)
        s_tiles = [None] * n_tiles
        nxt = jnp.broadcast_to(k_col, (nb, HEAD_DIM, LANES))
        for j in reversed(range(n_tiles)):
            ts = slice(j * LANES, (j + 1) * LANES)
            kt = c_ref[:, 0, h, :, ts]
            s = jnp.sum(kt * q_col, axis=1, keepdims=True) * scale
            s_tiles[j] = jnp.where(on_grid, s, NEG_INF)
            rolled = pltpu.roll(kt, LANES - 1, axis=2)
            cout_ref[:, 0, h, :, ts] = jnp.where(last_lane, nxt, rolled)
            nxt = rolled
        s_new = jnp.sum(k_new * q, axis=2, keepdims=True) * scale
        m_row = s_tiles[0]
        for j in range(1, n_tiles):
            m_row = jnp.maximum(m_row, s_tiles[j])
        m = jnp.maximum(jnp.max(m_row, axis=2, keepdims=True), s_new)
        p_new = jnp.exp(s_new - m)
        l_row = jnp.zeros((nb, 1, LANES), F32)
        acc = jnp.zeros((nb, HEAD_DIM, LANES), F32)
        nxt = jnp.broadcast_to(v_col, (nb, HEAD_DIM, LANES))
        for j in reversed(range(n_tiles)):
            ts = slice(j * LANES, (j + 1) * LANES)
            p = jnp.exp(s_tiles[j] - m)
            l_row = l_row + p
            vt = c_ref[:, 1, h, :, ts]
            acc = acc + vt * p
            rolled = pltpu.roll(vt, LANES - 1, axis=2)
            cout_ref[:, 1, h, :, ts] = jnp.where(last_lane, nxt, rolled)
            nxt = rolled
        l = jnp.sum(l_row, axis=2, keepdims=True) + p_new
        o_col = (jnp.sum(acc, axis=2, keepdims=True) + p_new * v_col) / l
        parts += [jnp.sum(eye * o_col, axis=1, keepdims=True), jnp.broadcast_to(m + jnp.log(l), (nb, 1, HEAD_DIM))]
    o_ref[...] = jnp.concatenate(parts, axis=2)


def _cache_attn(qkv, cache_t, dil):
    window = cache_t.shape[-1]
    nb = CACHE_BATCH_PER_STEP[window]
    blk = (nb, 2, HEADS_PER_GROUP, HEAD_DIM, window)
    return pl.pallas_call(
        functools.partial(_cache_attn_kernel, dil=dil, nb=nb),
        grid=(DEC_BATCH // nb,),
        in_specs=[pl.BlockSpec((nb, 1, 3 * D_GROUP), lambda i: (i, 0, 0)),
                  pl.BlockSpec(blk, lambda i: (i, 0, 0, 0, 0))],
        out_specs=[pl.BlockSpec((nb, 1, HEADS_PER_GROUP * LANES), lambda i: (i, 0, 0)),
                   pl.BlockSpec(blk, lambda i: (i, 0, 0, 0, 0))],
        out_shape=[jax.ShapeDtypeStruct((DEC_BATCH, 1, HEADS_PER_GROUP * LANES), F32),
                   jax.ShapeDtypeStruct(cache_t.shape, F32)],
        compiler_params=_params("parallel"),
        name=f"cache_attn_w{window}",
    )(qkv, cache_t)


MERGE_TM = 208


def _merge_kernel(x_ref, y_ref, bonus_ref, g_ref, a0_ref, a1_ref, a2_ref, pg_ref, bg_ref,
                  lnw_ref, lnb_ref, e_ref, wbr_ref, wba0_ref, wba1_ref, wba2_ref, wo_ref, o_ref):
    e = e_ref[...]
    inv_n = 1.0 / HEAD_DIM
    y = y_ref[...]
    d = y - _segsum(y, e) * inv_n
    var = _segsum(d * d, e) * inv_n
    yn = d * lax.rsqrt(var + RWKV_LN_EPS) * lnw_ref[...] + lnb_ref[...]
    o_a = ((yn + bonus_ref[...]) * g_ref[...]).astype(BF16)
    br_a = jnp.dot(o_a, wbr_ref[...], preferred_element_type=F32)
    a = (a0_ref[...], a1_ref[...], a2_ref[...])
    m = jnp.maximum(jnp.maximum(a[0], a[1]), a[2])
    ex = [jnp.exp(t - m) for t in a]
    inv = 1.0 / (ex[0] + ex[1] + ex[2])
    br_b = jnp.zeros_like(br_a)
    for grp, wb_ref in enumerate((wba0_ref, wba1_ref, wba2_ref)):
        alpha = ex[grp] * inv
        parts = []
        for h in range(HEADS_PER_GROUP):
            hs = slice(h * LANES, (h + 1) * LANES)
            parts.append(a[grp][:, hs] * pltpu.roll(alpha[:, hs], HEAD_DIM, axis=1))
        xg = jnp.concatenate(parts, axis=1).astype(BF16)
        br_b = br_b + jnp.dot(xg, wb_ref[...], preferred_element_type=F32)
    gates = jax.nn.sigmoid(pg_ref[...] + bg_ref[...])
    merged = (gates[:, 0:D_MODEL] * br_a + gates[:, D_MODEL:] * br_b).astype(BF16)
    o_ref[...] = x_ref[...] + jnp.dot(merged, wo_ref[...], preferred_element_type=F32)


def _merge(x, y, bonus, g, att, p_gate, b_gate, ln_w, ln_b, e_heads, w_br, w_ba, w_o):
    tile = lambda n: pl.BlockSpec((MERGE_TM, n), lambda i: (i, 0))
    row = lambda n: pl.BlockSpec((1, n), lambda i: (0, 0))
    att_w = HEADS_PER_GROUP * LANES
    return pl.pallas_call(
        _merge_kernel,
        grid=(M_ALL // MERGE_TM,),
        in_specs=[tile(D_MODEL), tile(D_RWKV), tile(D_RWKV), tile(D_RWKV),
                  tile(att_w), tile(att_w), tile(att_w), tile(2 * D_MODEL), row(2 * D_MODEL),
                  row(D_RWKV), row(D_RWKV), _const_spec(e_heads.shape), _const_spec(w_br.shape),
                  _const_spec(w_ba[0].shape), _const_spec(w_ba[1].shape), _const_spec(w_ba[2].shape),
                  _const_spec(w_o.shape)],
        out_specs=tile(D_MODEL),
        out_shape=jax.ShapeDtypeStruct((M_ALL, D_MODEL), F32),
        compiler_params=_params("parallel"),
        name="merge",
    )(x, y, bonus, g, att[0], att[1], att[2], p_gate, b_gate, ln_w, ln_b, e_heads, w_br,
      w_ba[0], w_ba[1], w_ba[2], w_o)


def _split_bf16(w):
    hi = w.astype(BF16)
    return hi, (w - hi.astype(F32)).astype(BF16)


def _head_ones(n_heads):
    return jnp.kron(jnp.eye(n_heads, dtype=F32), jnp.ones((HEAD_DIM, HEAD_DIM), F32)).astype(BF16)


def _rotary_tables():
    half = HEAD_DIM // 2
    inv_freq = ROPE_THETA ** (-jnp.arange(half, dtype=F32) / half)
    pos = jnp.concatenate([jnp.tile(jnp.arange(SEQ), BATCH), jnp.full((DEC_BATCH,), PAST_LEN)]).astype(F32)
    ang = pos[:, None] * inv_freq[None, :]
    cos, sin = jnp.cos(ang), jnp.sin(ang)
    cos = jnp.concatenate([cos, cos] * (LANES // HEAD_DIM), axis=1)
    sin = jnp.concatenate([-sin, sin] * (LANES // HEAD_DIM), axis=1)
    return cos, sin


def kernel(x_prompt, x_sample, state_rwkv_shift, state_rwkv_wkv, cache_att_w128, cache_att_w512, cache_att_w2048,
           ffn1_norm, ffn1_w_in, ffn1_w_out, mix_norm, w_in, b_gate, rwkv_mu, rwkv_w0, rwkv_w2, rwkv_a0, rwkv_a2,
           rwkv_g2, rwkv_k_k, rwkv_k_a, rwkv_r_k, rwkv_ln_w, rwkv_ln_b, w_branch_rwkv, attn_q_norm, attn_k_norm,
           w_branch_attn, w_out, ffn2_norm, ffn2_w_in, ffn2_w_out):
    caches = (cache_att_w128, cache_att_w512, cache_att_w2048)
    x = jnp.concatenate([x_prompt.reshape(M_PROMPT, D_MODEL), x_sample.reshape(DEC_BATCH, D_MODEL)], axis=0)

    w_mix = w_in[0]
    pad = RWKV_PROJ_PAD - D_RWKV_PROJ
    w_rwkv = jnp.pad(w_mix[:, :D_RWKV_PROJ], ((0, 0), (0, pad))).astype(BF16)
    w_att = w_mix[:, D_RWKV_PROJ:D_RWKV_PROJ + 3 * D_ATT].astype(BF16)
    w_gate = w_mix[:, D_RWKV_PROJ + 3 * D_ATT:].astype(BF16)
    mu = jnp.pad(rwkv_mu, ((0, 0), (0, pad)))
    state_shift = jnp.pad(state_rwkv_shift[0], ((0, 0), (0, pad)))
    zeros = jnp.zeros((D_DECAY_LORA, D_RWKV), F32)
    w_lora = jnp.concatenate([jnp.concatenate([rwkv_w2[0], zeros], axis=1),
                              jnp.concatenate([zeros, rwkv_a2[0]], axis=1)], axis=0)
    w_glora = jnp.pad(rwkv_g2[0], ((0, 2 * LANES - D_GATE_LORA), (0, 0)))
    wl_hi, wl_lo = _split_bf16(w_lora)
    wg_hi, wg_lo = _split_bf16(w_glora)
    e_heads = _head_ones(RWKV_HEADS)
    e_att = _head_ones(ATT_HEADS)
    cos, sin = _rotary_tables()
    q_gain = jnp.tile(attn_q_norm, (1, ATT_HEADS))
    k_gain = jnp.tile(attn_k_norm, (1, ATT_HEADS))
    wba = w_branch_attn[0].reshape(len(ATT_GROUPS), HEADS_PER_GROUP, HEAD_DIM, D_MODEL)
    wba = jnp.pad(wba, ((0, 0), (0, 0), (0, LANES - HEAD_DIM), (0, 0))).astype(BF16)
    wba = wba.reshape(len(ATT_GROUPS), HEADS_PER_GROUP * LANES, D_MODEL)

    x1, h_mix = _ffn(x, ffn1_norm, ffn1_w_in[0].astype(BF16), ffn1_w_out[0].astype(BF16), next_norm=mix_norm)

    p_rwkv = _proj(h_mix, w_rwkv, 512, "proj_rwkv")
    p_att = _proj(h_mix, w_att, 768, "proj_att")
    p_gate = _proj(h_mix, w_gate, 512, "proj_gate")

    kw, v_r, kw_s_t, v_s_t, bonus, g_r = _rwkv_prep(p_rwkv, state_shift, mu, rwkv_w0, rwkv_a0, rwkv_k_k, rwkv_k_a,
                                                    rwkv_r_k.reshape(1, D_RWKV), wl_hi, wl_lo, wg_hi, wg_lo, e_heads)
    kw_t = kw.reshape(5, BATCH, SEQ, RWKV_HEADS, 2, K_HALF)
    kw_t = kw_t.transpose(2, 0, 5, 4, 1, 3).reshape(SEQ, 5, K_HALF, LANES)
    v_t = v_r.reshape(BATCH, SEQ, RWKV_HEADS, HEAD_DIM).transpose(1, 3, 0, 2).reshape(SEQ, HEAD_DIM, BH)
    v_t = jnp.concatenate([v_t, v_t], axis=-1)
    y_t, s_t = _rwkv_seq(kw_t, v_t)
    y_p = y_t[:, :, :BH].reshape(SEQ, HEAD_DIM, BATCH, RWKV_HEADS).transpose(2, 0, 3, 1).reshape(M_PROMPT, D_RWKV)
    wkv_p = s_t.reshape(K_HALF, HEAD_DIM, 2, BATCH, RWKV_HEADS).transpose(3, 4, 1, 2, 0)
    wkv_p = wkv_p.reshape(BATCH, RWKV_HEADS, HEAD_DIM, HEAD_DIM)
    y_s_t, wkv_s_t = _rwkv_step(kw_s_t, v_s_t, state_rwkv_wkv[0].transpose(1, 2, 3, 0))
    wkv_s = wkv_s_t.transpose(3, 0, 1, 2)
    y_r = jnp.concatenate([y_p, y_s_t.T], axis=0)

    qkv = _att_prep(p_att, cos, sin, q_gain, k_gain, e_att)
    att_p, att_s, kv_p, kv_s = [], [], [], []
    for g, (window, dil) in enumerate(ATT_GROUPS):
        ls = SEQ // dil
        t = qkv[g][:M_PROMPT].reshape(BATCH, ls, dil, 3, HEADS_PER_GROUP, HEAD_DIM).transpose(0, 2, 4, 1, 3, 5)
        o = _att_window(t.reshape(BATCH * dil * HEADS_PER_GROUP, ls, 3 * HEAD_DIM))
        o = o.reshape(BATCH, dil, HEADS_PER_GROUP, ls, LANES).transpose(0, 3, 1, 2, 4)
        att_p.append(o.reshape(M_PROMPT, HEADS_PER_GROUP * LANES))
        keep = min(window, SEQ)
        kv = qkv[g][:M_PROMPT, D_GROUP:].reshape(BATCH, SEQ, 2, HEADS_PER_GROUP, HEAD_DIM)
        kv_p.append(kv[:, SEQ - keep:][None])
        o_s, cache_new = _cache_attn(qkv[g][M_PROMPT:].reshape(DEC_BATCH, 1, 3 * D_GROUP),
                                     caches[g][0].transpose(0, 2, 3, 4, 1), dil)
        att_s.append(o_s.reshape(DEC_BATCH, HEADS_PER_GROUP * LANES))
        kv_s.append(cache_new.transpose(0, 4, 1, 2, 3)[None])
    att = [jnp.concatenate([a_p, a_s], axis=0) for a_p, a_s in zip(att_p, att_s)]

    x2 = _merge(x1, y_r, bonus, g_r, att, p_gate, b_gate, rwkv_ln_w, rwkv_ln_b,
                e_heads, w_branch_rwkv[0].astype(BF16), wba, w_out[0].astype(BF16))
    y = _ffn(x2, ffn2_norm, ffn2_w_in[0].astype(BF16), ffn2_w_out[0].astype(BF16))

    shift_p = p_rwkv[SEQ - 1:M_PROMPT:SEQ, :D_RWKV_PROJ][None]
    shift_s = p_rwkv[M_PROMPT:, :D_RWKV_PROJ][None]
    return (y[:M_PROMPT].reshape(BATCH, SEQ, D_MODEL), y[M_PROMPT:].reshape(DEC_BATCH, 1, D_MODEL),
            shift_p, wkv_p[None], kv_p[0], kv_p[1], kv_p[2],
            shift_s, wkv_s[None], kv_s[0], kv_s[1], kv_s[2])
```

```python
import functools

import jax
import jax.numpy as jnp
from jax import lax
from jax.experimental import pallas as pl
from jax.experimental.pallas import tpu as pltpu

D_MODEL = 2048
BATCH = 4
SEQ = 2048
DEC_BATCH = 128
PAST_LEN = 2048
M_PROMPT = BATCH * SEQ
M_ALL = M_PROMPT + DEC_BATCH

RWKV_HEADS = 16
HEAD_DIM = 64
D_RWKV = RWKV_HEADS * HEAD_DIM
D_DECAY_LORA = 64
D_ICLR_LORA = 64
D_GATE_LORA = 160
D_RWKV_PROJ = 3 * D_RWKV + D_DECAY_LORA + D_ICLR_LORA + D_GATE_LORA
RWKV_PROJ_PAD = 3584
RWKV_LN_EPS = 64e-5

ATT_GROUPS = ((128, 1), (512, 4), (2048, 16))
HEADS_PER_GROUP = 4
ATT_HEADS = HEADS_PER_GROUP * len(ATT_GROUPS)
D_ATT = ATT_HEADS * HEAD_DIM
D_GROUP = HEADS_PER_GROUP * HEAD_DIM
N_KEYS_PAST = 128
ROPE_THETA = 10000.0
D_FF = 5632
NORM_EPS = 1e-6
NEG_INF = -1e30

LANES = 128
VMEM_LIMIT = 56 * 1024 * 1024

F32 = jnp.float32
BF16 = jnp.bfloat16


def _params(*sem):
    return pltpu.CompilerParams(dimension_semantics=sem, vmem_limit_bytes=VMEM_LIMIT)


def _const_spec(shape):
    nd = len(shape)
    return pl.BlockSpec(shape, lambda *_: (0,) * nd, pipeline_mode=pl.Buffered(1))


def _rms_to_bf16(x, g):
    ms = jnp.mean(x * x, axis=-1, keepdims=True)
    return (x * lax.rsqrt(ms + NORM_EPS) * g).astype(BF16)


def _segsum(x, e):
    hi = x.astype(BF16)
    lo = (x - hi.astype(F32)).astype(BF16)
    return (jnp.dot(hi, e, preferred_element_type=F32) + jnp.dot(lo, e, preferred_element_type=F32))


def _dot_split(x, w_hi, w_lo):
    hi = x.astype(BF16)
    lo = (x - hi.astype(F32)).astype(BF16)
    return (jnp.dot(hi, w_hi, preferred_element_type=F32) + jnp.dot(lo, w_hi, preferred_element_type=F32)
            + jnp.dot(hi, w_lo, preferred_element_type=F32))


FFN_TM = 640
FFN_TF = 512


def _ffn_kernel(x_ref, g_ref, wg_ref, wu_ref, wo_ref, *rest, emit_norm):
    if emit_norm:
        gn_ref, o_ref, hn_ref, h_ref = rest
    else:
        o_ref, h_ref = rest
    j = pl.program_id(1)

    @pl.when(j == 0)
    def _():
        h_ref[...] = _rms_to_bf16(x_ref[...], g_ref[...])
        o_ref[...] = jnp.zeros_like(o_ref)

    h = h_ref[...]
    gate = jnp.dot(h, wg_ref[...], preferred_element_type=F32)
    up = jnp.dot(h, wu_ref[...], preferred_element_type=F32)
    act = (gate * jax.nn.sigmoid(gate) * up).astype(BF16)
    o_ref[...] += jnp.dot(act, wo_ref[...], preferred_element_type=F32)

    @pl.when(j == pl.num_programs(1) - 1)
    def _():
        out = x_ref[...] + 0.5 * o_ref[...]
        o_ref[...] = out
        if emit_norm:
            hn_ref[...] = _rms_to_bf16(out, gn_ref[...])


def _ffn(x, g, w_in, w_out, next_norm=None):
    m = x.shape[0]
    nj = D_FF // FFN_TF
    emit_norm = next_norm is not None
    row = pl.BlockSpec((1, D_MODEL), lambda i, j: (0, 0))
    tile = pl.BlockSpec((FFN_TM, D_MODEL), lambda i, j: (i, 0))
    in_specs = [tile, row,
                pl.BlockSpec((D_MODEL, FFN_TF), lambda i, j: (0, j)),
                pl.BlockSpec((D_MODEL, FFN_TF), lambda i, j: (0, j + nj)),
                pl.BlockSpec((FFN_TF, D_MODEL), lambda i, j: (j, 0))]
    args = [x, g, w_in, w_in, w_out]
    out_specs, out_shape = tile, jax.ShapeDtypeStruct((m, D_MODEL), F32)
    if emit_norm:
        in_specs.append(row)
        args.append(next_norm)
        out_specs = [tile, tile]
        out_shape = [out_shape, jax.ShapeDtypeStruct((m, D_MODEL), BF16)]
    return pl.pallas_call(
        functools.partial(_ffn_kernel, emit_norm=emit_norm),
        grid=(m // FFN_TM, nj),
        in_specs=in_specs,
        out_specs=out_specs,
        out_shape=out_shape,
        scratch_shapes=[pltpu.VMEM((FFN_TM, D_MODEL), BF16)],
        compiler_params=_params("parallel", "arbitrary"),
        name="ffn",
    )(*args)


PROJ_TM = 1664


def _proj_kernel(h_ref, w_ref, o_ref):
    o_ref[...] = jnp.dot(h_ref[...], w_ref[...], preferred_element_type=F32)


def _proj(h, w, tn, name):
    m, n = h.shape[0], w.shape[1]
    return pl.pallas_call(
        _proj_kernel,
        grid=(m // PROJ_TM, n // tn),
        in_specs=[
            pl.BlockSpec((PROJ_TM, D_MODEL), lambda i, j: (i, 0)),
            pl.BlockSpec((D_MODEL, tn), lambda i, j: (0, j)),
        ],
        out_specs=pl.BlockSpec((PROJ_TM, tn), lambda i, j: (i, j)),
        out_shape=jax.ShapeDtypeStruct((m, n), F32),
        compiler_params=_params("parallel", "arbitrary"),
        name=name,
    )(h, w)


PREP_TM = 128
PREP_TILES_PER_SEQ = SEQ // PREP_TM
PREP_SAMPLE_TILE = M_PROMPT // PREP_TM
KW_R, KW_DECAY, KW_K, KW_KK, KW_B = range(5)


def _prep_tile(i):
    return jnp.where(i == 0, PREP_SAMPLE_TILE, i - 1)


def _rwkv_prep_kernel(p_ref, tail_ref, state_ref, mu_ref, w0_ref, a0_ref, kk_ref, ka_ref, rk_ref,
                      wl_hi_ref, wl_lo_ref, wg_hi_ref, wg_lo_ref, e_ref,
                      kw_ref, v_ref, kws_ref, vs_ref, bonus_ref, g_ref):
    i = pl.program_id(0)
    is_sample = i == 0
    p = p_ref[...]
    tail = jnp.where((i - 1) % PREP_TILES_PER_SEQ == 0, 0.0, tail_ref[7:8, :])
    row = lax.broadcasted_iota(jnp.int32, p.shape, 0)
    prev = jnp.where(row == 0, tail, pltpu.roll(p, 1, axis=0))
    prev = jnp.where(is_sample, state_ref[...], prev)
    pm = p + (prev - p) * mu_ref[...]

    r = pm[:, 0:D_RWKV]
    k = pm[:, D_RWKV:2 * D_RWKV]
    v = pm[:, 2 * D_RWKV:3 * D_RWKV]
    lora = pm[:, 3 * D_RWKV:]
    lane = lax.broadcasted_iota(jnp.int32, (p.shape[0], LANES), 1)
    x_wa = lora[:, 0:LANES]
    x_wa = jnp.where(lane < D_DECAY_LORA, jnp.tanh(x_wa), x_wa)
    x_g = jax.nn.sigmoid(lora[:, LANES:3 * LANES])
    wa = _dot_split(x_wa, wl_hi_ref[...], wl_lo_ref[...])
    g = _dot_split(x_g, wg_hi_ref[...], wg_lo_ref[...])

    z = -(w0_ref[...] + wa[:, 0:D_RWKV])
    softplus = jnp.maximum(z, 0.0) + jnp.log(1.0 + jnp.exp(-jnp.abs(z)))
    decay = jnp.exp(-jnp.exp(-softplus - 0.5))
    a = jax.nn.sigmoid(a0_ref[...] + wa[:, D_RWKV:])
    e = e_ref[...]
    kk = k * kk_ref[...]
    kk = kk / jnp.maximum(jnp.sqrt(_segsum(kk * kk, e)), 1e-12)
    k_hat = k * (1.0 + (a - 1.0) * ka_ref[...])
    kw = (r, decay, k_hat, kk, kk * a)
    bonus_ref[...] = _segsum(r * k_hat * rk_ref[...], e) * v
    g_ref[...] = g

    @pl.when(i > 0)
    def _():
        for idx in range(5):
            kw_ref[idx] = kw[idx]
        v_ref[...] = v

    @pl.when(is_sample)
    def _():
        for idx in range(5):
            kws_ref[idx] = kw[idx].T
        vs_ref[...] = v.T


def _rwkv_prep(p_rwkv, state_shift, mu, w0, a0, k_k, k_a, r_k, wl_hi, wl_lo, wg_hi, wg_lo, e_heads):
    n_tiles = M_ALL // PREP_TM
    row = lambda n: pl.BlockSpec((1, n), lambda i: (0, 0))
    prompt_tile = lambda i: jnp.maximum(i - 1, 0)
    return pl.pallas_call(
        _rwkv_prep_kernel,
        grid=(n_tiles,),
        in_specs=[
            pl.BlockSpec((PREP_TM, RWKV_PROJ_PAD), lambda i: (_prep_tile(i), 0)),
            pl.BlockSpec((8, RWKV_PROJ_PAD), lambda i: (jnp.maximum((i - 1) * (PREP_TM // 8) - 1, 0), 0)),
            pl.BlockSpec((DEC_BATCH, RWKV_PROJ_PAD), lambda i: (0, 0)),
            row(RWKV_PROJ_PAD), row(D_RWKV), row(D_RWKV), row(D_RWKV), row(D_RWKV), row(D_RWKV),
            _const_spec(wl_hi.shape), _const_spec(wl_lo.shape),
            _const_spec(wg_hi.shape), _const_spec(wg_lo.shape),
            _const_spec(e_heads.shape),
        ],
        out_specs=[
            pl.BlockSpec((5, PREP_TM, D_RWKV), lambda i: (0, prompt_tile(i), 0)),
            pl.BlockSpec((PREP_TM, D_RWKV), lambda i: (prompt_tile(i), 0)),
            pl.BlockSpec((5, D_RWKV, DEC_BATCH), lambda i: (0, 0, 0)),
            pl.BlockSpec((D_RWKV, DEC_BATCH), lambda i: (0, 0)),
            pl.BlockSpec((PREP_TM, D_RWKV), lambda i: (_prep_tile(i), 0)),
            pl.BlockSpec((PREP_TM, D_RWKV), lambda i: (_prep_tile(i), 0)),
        ],
        out_shape=[
            jax.ShapeDtypeStruct((5, M_PROMPT, D_RWKV), F32),
            jax.ShapeDtypeStruct((M_PROMPT, D_RWKV), F32),
            jax.ShapeDtypeStruct((5, D_RWKV, DEC_BATCH), F32),
            jax.ShapeDtypeStruct((D_RWKV, DEC_BATCH), F32),
            jax.ShapeDtypeStruct((M_ALL, D_RWKV), F32),
            jax.ShapeDtypeStruct((M_ALL, D_RWKV), F32),
        ],
        compiler_params=_params("arbitrary"),
        name="rwkv_prep",
    )(p_rwkv, p_rwkv, state_shift, mu, w0, a0, k_k, k_a, r_k, wl_hi, wl_lo, wg_hi, wg_lo, e_heads)


SEQ_TB = 64
K_HALF = HEAD_DIM // 2
BH = BATCH * RWKV_HEADS


def _rwkv_seq_kernel(kw_ref, v_ref, y_ref, sout_ref, s_ref):
    @pl.when(pl.program_id(0) == 0)
    def _():
        s_ref[...] = jnp.zeros_like(s_ref)

    sk = jnp.zeros((HEAD_DIM, LANES), F32)
    for kp in range(K_HALF):
        sk = sk + s_ref[kp] * kw_ref[0, KW_KK, pl.ds(kp, 1), :]

    n_vh = 4
    vh_rows = HEAD_DIM // n_vh

    def step(t, sk):
        t_next = jnp.minimum(t + 1, SEQ_TB - 1)
        sk_next = []
        for vh in range(n_vh):
            vs = pl.ds(vh * vh_rows, vh_rows)
            sk_h = sk[vh * vh_rows:(vh + 1) * vh_rows]
            sa = -(sk_h + pltpu.roll(sk_h, BH, axis=1))
            vt = v_ref[t, vs, :]
            y = jnp.zeros((vh_rows, LANES), F32)
            nxt = jnp.zeros((vh_rows, LANES), F32)
            for kp in range(K_HALF):
                s_new = (s_ref[kp, vs, :] * kw_ref[t, KW_DECAY, pl.ds(kp, 1), :]
                         + sa * kw_ref[t, KW_B, pl.ds(kp, 1), :]
                         + vt * kw_ref[t, KW_K, pl.ds(kp, 1), :])
                s_ref[kp, vs, :] = s_new
                y = y + s_new * kw_ref[t, KW_R, pl.ds(kp, 1), :]
                nxt = nxt + s_new * kw_ref[t_next, KW_KK, pl.ds(kp, 1), :]
            y_ref[t, vs, :] = y + pltpu.roll(y, BH, axis=1)
            sk_next.append(nxt)
        return jnp.concatenate(sk_next, axis=0)

    lax.fori_loop(0, SEQ_TB, step, sk)

    @pl.when(pl.program_id(0) == pl.num_programs(0) - 1)
    def _():
        sout_ref[...] = s_ref[...]


def _rwkv_seq(kw_t, v_t):
    return pl.pallas_call(
        _rwkv_seq_kernel,
        grid=(SEQ // SEQ_TB,),
        in_specs=[
            pl.BlockSpec((SEQ_TB, 5, K_HALF, LANES), lambda i: (i, 0, 0, 0)),
            pl.BlockSpec((SEQ_TB, HEAD_DIM, LANES), lambda i: (i, 0, 0)),
        ],
        out_specs=[
            pl.BlockSpec((SEQ_TB, HEAD_DIM, LANES), lambda i: (i, 0, 0)),
            pl.BlockSpec((K_HALF, HEAD_DIM, LANES), lambda i: (0, 0, 0)),
        ],
        out_shape=[
            jax.ShapeDtypeStruct((SEQ, HEAD_DIM, LANES), F32),
            jax.ShapeDtypeStruct((K_HALF, HEAD_DIM, LANES), F32),
        ],
        scratch_shapes=[pltpu.VMEM((K_HALF, HEAD_DIM, LANES), F32)],
        compiler_params=_params("arbitrary"),
        name="rwkv_seq",
    )(kw_t, v_t)


def _rwkv_step_kernel(kw_ref, v_ref, s_ref, y_ref, sout_ref):
    def body(vi, carry):
        s = s_ref[0, vi]
        sa = -jnp.sum(s * kw_ref[KW_KK], axis=0, keepdims=True)
        s_new = s * kw_ref[KW_DECAY] + sa * kw_ref[KW_B] + v_ref[pl.ds(vi, 1), :] * kw_ref[KW_K]
        sout_ref[0, vi] = s_new
        y_ref[pl.ds(vi, 1), :] = jnp.sum(s_new * kw_ref[KW_R], axis=0, keepdims=True)
        return carry

    lax.fori_loop(0, HEAD_DIM, body, 0)


def _rwkv_step(kw_t, v_t, state_t):
    return pl.pallas_call(
        _rwkv_step_kernel,
        grid=(RWKV_HEADS,),
        in_specs=[
            pl.BlockSpec((5, HEAD_DIM, DEC_BATCH), lambda h: (0, h, 0)),
            pl.BlockSpec((HEAD_DIM, DEC_BATCH), lambda h: (h, 0)),
            pl.BlockSpec((1, HEAD_DIM, HEAD_DIM, DEC_BATCH), lambda h: (h, 0, 0, 0)),
        ],
        out_specs=[
            pl.BlockSpec((HEAD_DIM, DEC_BATCH), lambda h: (h, 0)),
            pl.BlockSpec((1, HEAD_DIM, HEAD_DIM, DEC_BATCH), lambda h: (h, 0, 0, 0)),
        ],
        out_shape=[
            jax.ShapeDtypeStruct((D_RWKV, DEC_BATCH), F32),
            jax.ShapeDtypeStruct((RWKV_HEADS, HEAD_DIM, HEAD_DIM, DEC_BATCH), F32),
        ],
        compiler_params=_params("parallel"),
        name="rwkv_step",
    )(kw_t, v_t, state_t)


ATT_PREP_TM = 640


def _swap_halves(x):
    lane = lax.broadcasted_iota(jnp.int32, x.shape, 1)
    first = (lane % HEAD_DIM) < HEAD_DIM // 2
    return jnp.where(first, pltpu.roll(x, LANES - HEAD_DIM // 2, axis=1), pltpu.roll(x, HEAD_DIM // 2, axis=1))


PAIRS = D_GROUP // LANES
QKV_TILES = 3 * PAIRS


def _att_prep_kernel(p_ref, cos_ref, sin_ref, qg_ref, kg_ref, e_ref, o0_ref, o1_ref, o2_ref):
    cos = cos_ref[...]
    sin = sin_ref[...]
    e = e_ref[...]
    o_refs = (o0_ref, o1_ref, o2_ref)
    for part, g_ref in ((0, qg_ref), (1, kg_ref)):
        x = p_ref[:, part * D_ATT:(part + 1) * D_ATT]
        ms = _segsum(x * x, e) * (1.0 / HEAD_DIM)
        xn = x * lax.rsqrt(ms + NORM_EPS) * g_ref[...]
        for c in range(D_ATT // LANES):
            xc = xn[:, c * LANES:(c + 1) * LANES]
            o_refs[c // PAIRS][part * PAIRS + c % PAIRS] = xc * cos + _swap_halves(xc) * sin
    for c in range(D_ATT // LANES):
        o_refs[c // PAIRS][2 * PAIRS + c % PAIRS] = p_ref[:, 2 * D_ATT + c * LANES:2 * D_ATT + (c + 1) * LANES]


def _att_prep(p_att, cos, sin, q_gain, k_gain, e_att):
    out = jax.ShapeDtypeStruct((QKV_TILES, M_ALL, LANES), F32)
    return pl.pallas_call(
        _att_prep_kernel,
        grid=(M_ALL // ATT_PREP_TM,),
        in_specs=[
            pl.BlockSpec((ATT_PREP_TM, 3 * D_ATT), lambda i: (i, 0)),
            pl.BlockSpec((ATT_PREP_TM, LANES), lambda i: (i, 0)),
            pl.BlockSpec((ATT_PREP_TM, LANES), lambda i: (i, 0)),
            pl.BlockSpec((1, D_ATT), lambda i: (0, 0)),
            pl.BlockSpec((1, D_ATT), lambda i: (0, 0)),
            _const_spec(e_att.shape),
        ],
        out_specs=[pl.BlockSpec((QKV_TILES, ATT_PREP_TM, LANES), lambda i: (0, i, 0))] * 3,
        out_shape=[out] * 3,
        compiler_params=_params("parallel"),
        name="att_prep",
    )(p_att, cos, sin, q_gain, k_gain, e_att)


ATT_TQ = 128


def _att_prompt_kernel(qkv_ref, o_ref, *, dil):
    scale = HEAD_DIM ** -0.5
    ls = SEQ // dil

    def heads(part, rows):
        pairs = [qkv_ref[part * PAIRS + pr, rows, :] for pr in range(PAIRS)]
        return jnp.stack([x[:, h * HEAD_DIM:(h + 1) * HEAD_DIM] for x in pairs for h in range(LANES // HEAD_DIM)],
                         axis=0)

    for c in range(dil):
        for qi in range(ls // ATT_TQ):
            rows = pl.ds(c + dil * qi * ATT_TQ, ATT_TQ, stride=dil)
            q = heads(0, rows).astype(BF16)
            k = heads(1, rows)
            v = heads(2, rows)
            q_off = 0
            if qi > 0:
                prev = pl.ds(c + dil * (qi - 1) * ATT_TQ, ATT_TQ, stride=dil)
                k = jnp.concatenate([heads(1, prev), k], axis=1)
                v = jnp.concatenate([heads(2, prev), v], axis=1)
                q_off = ATT_TQ
            s = jnp.einsum("hqd,hkd->hqk", q, k.astype(BF16), preferred_element_type=F32) * scale
            dist = (lax.broadcasted_iota(jnp.int32, s.shape, 1) + q_off) - lax.broadcasted_iota(jnp.int32, s.shape, 2)
            s = jnp.where((dist >= 0) & (dist <= N_KEYS_PAST), s, NEG_INF)
            m = jnp.max(s, axis=-1, keepdims=True)
            p = jnp.exp(s - m)
            l = jnp.sum(p, axis=-1, keepdims=True)
            lse = jnp.broadcast_to(m + jnp.log(l), (HEADS_PER_GROUP, ATT_TQ, HEAD_DIM))
            o = jnp.einsum("hqk,hkd->hqd", (p / l).astype(BF16), v.astype(BF16), preferred_element_type=F32)
            for h in range(HEADS_PER_GROUP):
                o_ref[h, rows, :] = jnp.concatenate([o[h], lse[h]], axis=1)


def _att_prompt(qkv, dil):
    return pl.pallas_call(
        functools.partial(_att_prompt_kernel, dil=dil),
        grid=(BATCH,),
        in_specs=[pl.BlockSpec((QKV_TILES, SEQ, LANES), lambda b: (0, b, 0))],
        out_specs=pl.BlockSpec((HEADS_PER_GROUP, SEQ, LANES), lambda b: (0, b, 0)),
        out_shape=jax.ShapeDtypeStruct((HEADS_PER_GROUP, M_PROMPT, LANES), F32),
        compiler_params=_params("parallel"),
        name=f"att_prompt_d{dil}",
    )(qkv)


CACHE_BATCH_PER_STEP = {128: 8, 512: 4, 2048: 2}


def _cache_attn_kernel(qkv_ref, c_ref, o_ref, cout_ref, *, dil, nb):
    n_tiles = c_ref.shape[-1] // LANES
    scale = HEAD_DIM ** -0.5
    r_i = lax.broadcasted_iota(jnp.int32, (HEAD_DIM, HEAD_DIM), 0)
    c_i = lax.broadcasted_iota(jnp.int32, (HEAD_DIM, HEAD_DIM), 1)
    eye = (r_i == c_i).astype(F32)
    on_grid = (lax.broadcasted_iota(jnp.int32, (1, LANES), 1) % dil) == 0
    last_lane = lax.broadcasted_iota(jnp.int32, (HEAD_DIM, LANES), 1) == LANES - 1

    def to_col(rows):
        return jnp.sum(eye * rows, axis=2, keepdims=True)

    rows = qkv_ref[...]
    parts = []
    for h in range(HEADS_PER_GROUP):
        q = rows[:, :, h * HEAD_DIM:(h + 1) * HEAD_DIM]
        k_new = rows[:, :, D_GROUP + h * HEAD_DIM:D_GROUP + (h + 1) * HEAD_DIM]
        v_new = rows[:, :, 2 * D_GROUP + h * HEAD_DIM:2 * D_GROUP + (h + 1) * HEAD_DIM]
        q_col, k_col, v_col = to_col(q), to_col(k_new), to_col(v_new)
        s_tiles = [None] * n_tiles
        nxt = jnp.broadcast_to(k_col, (nb, HEAD_DIM, LANES))
        for j in reversed(range(n_tiles)):
            ts = slice(j * LANES, (j + 1) * LANES)
            kt = c_ref[:, 0, h, :, ts]
            s = jnp.sum(kt * q_col, axis=1, keepdims=True) * scale
            s_tiles[j] = jnp.where(on_grid, s, NEG_INF)
            rolled = pltpu.roll(kt, LANES - 1, axis=2)
            cout_ref[:, 0, h, :, ts] = jnp.where(last_lane, nxt, rolled)
            nxt = rolled
        s_new = jnp.sum(k_new * q, axis=2, keepdims=True) * scale
        m_row = s_tiles[0]
        for j in range(1, n_tiles):
            m_row = jnp.maximum(m_row, s_tiles[j])
        m = jnp.maximum(jnp.max(m_row, axis=2, keepdims=True), s_new)
        p_new = jnp.exp(s_new - m)
        l_row = jnp.zeros((nb, 1, LANES), F32)
        acc = jnp.zeros((nb, HEAD_DIM, LANES), F32)
        nxt = jnp.broadcast_to(v_col, (nb, HEAD_DIM, LANES))
        for j in reversed(range(n_tiles)):
            ts = slice(j * LANES, (j + 1) * LANES)
            p = jnp.exp(s_tiles[j] - m)
            l_row = l_row + p
            vt = c_ref[:, 1, h, :, ts]
            acc = acc + vt * p
            rolled = pltpu.roll(vt, LANES - 1, axis=2)
            cout_ref[:, 1, h, :, ts] = jnp.where(last_lane, nxt, rolled)
            nxt = rolled
        l = jnp.sum(l_row, axis=2, keepdims=True) + p_new
        o_col = (jnp.sum(acc, axis=2, keepdims=True) + p_new * v_col) / l
        parts += [jnp.sum(eye * o_col, axis=1, keepdims=True), jnp.broadcast_to(m + jnp.log(l), (nb, 1, HEAD_DIM))]
    o_ref[...] = jnp.concatenate(parts, axis=2)


def _cache_attn(qkv, cache_t, dil):
    window = cache_t.shape[-1]
    nb = CACHE_BATCH_PER_STEP[window]
    blk = (nb, 2, HEADS_PER_GROUP, HEAD_DIM, window)
    return pl.pallas_call(
        functools.partial(_cache_attn_kernel, dil=dil, nb=nb),
        grid=(DEC_BATCH // nb,),
        in_specs=[pl.BlockSpec((nb, 1, 3 * D_GROUP), lambda i: (i, 0, 0)),
                  pl.BlockSpec(blk, lambda i: (i, 0, 0, 0, 0))],
        out_specs=[pl.BlockSpec((nb, 1, HEADS_PER_GROUP * LANES), lambda i: (i, 0, 0)),
                   pl.BlockSpec(blk, lambda i: (i, 0, 0, 0, 0))],
        out_shape=[jax.ShapeDtypeStruct((DEC_BATCH, 1, HEADS_PER_GROUP * LANES), F32),
                   jax.ShapeDtypeStruct(cache_t.shape, F32)],
        compiler_params=_params("parallel"),
        name=f"cache_attn_w{window}",
    )(qkv, cache_t)


MERGE_TM = 208


def _merge_kernel(x_ref, y_ref, bonus_ref, g_ref, a0_ref, a1_ref, a2_ref, pg_ref, bg_ref,
                  lnw_ref, lnb_ref, e_ref, wbr_ref, wba0_ref, wba1_ref, wba2_ref, wo_ref, o_ref):
    e = e_ref[...]
    inv_n = 1.0 / HEAD_DIM
    y = y_ref[...]
    d = y - _segsum(y, e) * inv_n
    var = _segsum(d * d, e) * inv_n
    yn = d * lax.rsqrt(var + RWKV_LN_EPS) * lnw_ref[...] + lnb_ref[...]
    o_a = ((yn + bonus_ref[...]) * g_ref[...]).astype(BF16)
    br_a = jnp.dot(o_a, wbr_ref[...], preferred_element_type=F32)
    a = (a0_ref[...], a1_ref[...], a2_ref[...])
    m = jnp.maximum(jnp.maximum(a[0], a[1]), a[2])
    ex = [jnp.exp(t - m) for t in a]
    inv = 1.0 / (ex[0] + ex[1] + ex[2])
    br_b = jnp.zeros_like(br_a)
    for grp, wb_ref in enumerate((wba0_ref, wba1_ref, wba2_ref)):
        alpha = ex[grp] * inv
        parts = [a[grp][h] * pltpu.roll(alpha[h], HEAD_DIM, axis=1) for h in range(HEADS_PER_GROUP)]
        xg = jnp.concatenate(parts, axis=1).astype(BF16)
        br_b = br_b + jnp.dot(xg, wb_ref[...], preferred_element_type=F32)
    gates = jax.nn.sigmoid(pg_ref[...] + bg_ref[...])
    merged = (gates[:, 0:D_MODEL] * br_a + gates[:, D_MODEL:] * br_b).astype(BF16)
    o_ref[...] = x_ref[...] + jnp.dot(merged, wo_ref[...], preferred_element_type=F32)


def _merge(x, y, bonus, g, att, p_gate, b_gate, ln_w, ln_b, e_heads, w_br, w_ba, w_o):
    tile = lambda n: pl.BlockSpec((MERGE_TM, n), lambda i: (i, 0))
    row = lambda n: pl.BlockSpec((1, n), lambda i: (0, 0))
    att_tile = pl.BlockSpec((HEADS_PER_GROUP, MERGE_TM, LANES), lambda i: (0, i, 0))
    return pl.pallas_call(
        _merge_kernel,
        grid=(M_ALL // MERGE_TM,),
        in_specs=[tile(D_MODEL), tile(D_RWKV), tile(D_RWKV), tile(D_RWKV),
                  att_tile, att_tile, att_tile, tile(2 * D_MODEL), row(2 * D_MODEL),
                  row(D_RWKV), row(D_RWKV), _const_spec(e_heads.shape), _const_spec(w_br.shape),
                  _const_spec(w_ba[0].shape), _const_spec(w_ba[1].shape), _const_spec(w_ba[2].shape),
                  _const_spec(w_o.shape)],
        out_specs=tile(D_MODEL),
        out_shape=jax.ShapeDtypeStruct((M_ALL, D_MODEL), F32),
        compiler_params=_params("parallel"),
        name="merge",
    )(x, y, bonus, g, att[0], att[1], att[2], p_gate, b_gate, ln_w, ln_b, e_heads, w_br,
      w_ba[0], w_ba[1], w_ba[2], w_o)


def _split_bf16(w):
    hi = w.astype(BF16)
    return hi, (w - hi.astype(F32)).astype(BF16)


def _head_ones(n_heads):
    return jnp.kron(jnp.eye(n_heads, dtype=F32), jnp.ones((HEAD_DIM, HEAD_DIM), F32)).astype(BF16)


def _rotary_tables():
    half = HEAD_DIM // 2
    inv_freq = ROPE_THETA ** (-jnp.arange(half, dtype=F32) / half)
    pos = jnp.concatenate([jnp.tile(jnp.arange(SEQ), BATCH), jnp.full((DEC_BATCH,), PAST_LEN)]).astype(F32)
    ang = pos[:, None] * inv_freq[None, :]
    cos, sin = jnp.cos(ang), jnp.sin(ang)
    cos = jnp.concatenate([cos, cos] * (LANES // HEAD_DIM), axis=1)
    sin = jnp.concatenate([-sin, sin] * (LANES // HEAD_DIM), axis=1)
    return cos, sin


def kernel(x_prompt, x_sample, state_rwkv_shift, state_rwkv_wkv, cache_att_w128, cache_att_w512, cache_att_w2048,
           ffn1_norm, ffn1_w_in, ffn1_w_out, mix_norm, w_in, b_gate, rwkv_mu, rwkv_w0, rwkv_w2, rwkv_a0, rwkv_a2,
           rwkv_g2, rwkv_k_k, rwkv_k_a, rwkv_r_k, rwkv_ln_w, rwkv_ln_b, w_branch_rwkv, attn_q_norm, attn_k_norm,
           w_branch_attn, w_out, ffn2_norm, ffn2_w_in, ffn2_w_out):
    caches = (cache_att_w128, cache_att_w512, cache_att_w2048)
    x = jnp.concatenate([x_prompt.reshape(M_PROMPT, D_MODEL), x_sample.reshape(DEC_BATCH, D_MODEL)], axis=0)

    w_mix = w_in[0]
    pad = RWKV_PROJ_PAD - D_RWKV_PROJ
    w_rwkv = jnp.pad(w_mix[:, :D_RWKV_PROJ], ((0, 0), (0, pad))).astype(BF16)
    w_att = w_mix[:, D_RWKV_PROJ:D_RWKV_PROJ + 3 * D_ATT].astype(BF16)
    w_gate = w_mix[:, D_RWKV_PROJ + 3 * D_ATT:].astype(BF16)
    mu = jnp.pad(rwkv_mu, ((0, 0), (0, pad)))
    state_shift = jnp.pad(state_rwkv_shift[0], ((0, 0), (0, pad)))
    zeros = jnp.zeros((D_DECAY_LORA, D_RWKV), F32)
    w_lora = jnp.concatenate([jnp.concatenate([rwkv_w2[0], zeros], axis=1),
                              jnp.concatenate([zeros, rwkv_a2[0]], axis=1)], axis=0)
    w_glora = jnp.pad(rwkv_g2[0], ((0, 2 * LANES - D_GATE_LORA), (0, 0)))
    wl_hi, wl_lo = _split_bf16(w_lora)
    wg_hi, wg_lo = _split_bf16(w_glora)
    e_heads = _head_ones(RWKV_HEADS)
    e_att = _head_ones(ATT_HEADS)
    cos, sin = _rotary_tables()
    q_gain = jnp.tile(attn_q_norm, (1, ATT_HEADS))
    k_gain = jnp.tile(attn_k_norm, (1, ATT_HEADS))
    wba = w_branch_attn[0].reshape(len(ATT_GROUPS), HEADS_PER_GROUP, HEAD_DIM, D_MODEL)
    wba = jnp.pad(wba, ((0, 0), (0, 0), (0, LANES - HEAD_DIM), (0, 0))).astype(BF16)
    wba = wba.reshape(len(ATT_GROUPS), HEADS_PER_GROUP * LANES, D_MODEL)

    x1, h_mix = _ffn(x, ffn1_norm, ffn1_w_in[0].astype(BF16), ffn1_w_out[0].astype(BF16), next_norm=mix_norm)

    p_rwkv = _proj(h_mix, w_rwkv, 512, "proj_rwkv")
    p_att = _proj(h_mix, w_att, 768, "proj_att")
    p_gate = _proj(h_mix, w_gate, 512, "proj_gate")

    kw, v_r, kw_s_t, v_s_t, bonus, g_r = _rwkv_prep(p_rwkv, state_shift, mu, rwkv_w0, rwkv_a0, rwkv_k_k, rwkv_k_a,
                                                    rwkv_r_k.reshape(1, D_RWKV), wl_hi, wl_lo, wg_hi, wg_lo, e_heads)
    kw_t = kw.reshape(5, BATCH, SEQ, RWKV_HEADS, 2, K_HALF)
    kw_t = kw_t.transpose(2, 0, 5, 4, 1, 3).reshape(SEQ, 5, K_HALF, LANES)
    v_t = v_r.reshape(BATCH, SEQ, RWKV_HEADS, HEAD_DIM).transpose(1, 3, 0, 2).reshape(SEQ, HEAD_DIM, BH)
    v_t = jnp.concatenate([v_t, v_t], axis=-1)
    y_t, s_t = _rwkv_seq(kw_t, v_t)
    y_p = y_t[:, :, :BH].reshape(SEQ, HEAD_DIM, BATCH, RWKV_HEADS).transpose(2, 0, 3, 1).reshape(M_PROMPT, D_RWKV)
    wkv_p = s_t.reshape(K_HALF, HEAD_DIM, 2, BATCH, RWKV_HEADS).transpose(3, 4, 1, 2, 0)
    wkv_p = wkv_p.reshape(BATCH, RWKV_HEADS, HEAD_DIM, HEAD_DIM)
    y_s_t, wkv_s_t = _rwkv_step(kw_s_t, v_s_t, state_rwkv_wkv[0].transpose(1, 2, 3, 0))
    wkv_s = wkv_s_t.transpose(3, 0, 1, 2)
    y_r = jnp.concatenate([y_p, y_s_t.T], axis=0)

    qkv = _att_prep(p_att, cos, sin, q_gain, k_gain, e_att)
    att, kv_p, kv_s = [], [], []
    for g, (window, dil) in enumerate(ATT_GROUPS):
        att_p = _att_prompt(qkv[g], dil)
        keep = min(window, SEQ)
        kv = qkv[g][PAIRS:, :M_PROMPT].reshape(2, PAIRS, BATCH, SEQ, LANES // HEAD_DIM, HEAD_DIM)
        kv = kv.transpose(2, 3, 0, 1, 4, 5).reshape(BATCH, SEQ, 2, HEADS_PER_GROUP, HEAD_DIM)
        kv_p.append(kv[:, SEQ - keep:][None])
        qkv_s = qkv[g][:, M_PROMPT:].transpose(1, 0, 2).reshape(DEC_BATCH, 1, 3 * D_GROUP)
        o_s, cache_new = _cache_attn(qkv_s, caches[g][0].transpose(0, 2, 3, 4, 1), dil)
        att_s = o_s.reshape(DEC_BATCH, HEADS_PER_GROUP, LANES).transpose(1, 0, 2)
        att.append(jnp.concatenate([att_p, att_s], axis=1))
        kv_s.append(cache_new.transpose(0, 4, 1, 2, 3)[None])

    x2 = _merge(x1, y_r, bonus, g_r, att, p_gate, b_gate, rwkv_ln_w, rwkv_ln_b,
                e_heads, w_branch_rwkv[0].astype(BF16), wba, w_out[0].astype(BF16))
    y = _ffn(x2, ffn2_norm, ffn2_w_in[0].astype(BF16), ffn2_w_out[0].astype(BF16))

    shift_p = p_rwkv[SEQ - 1:M_PROMPT:SEQ, :D_RWKV_PROJ][None]
    shift_s = p_rwkv[M_PROMPT:, :D_RWKV_PROJ][None]
    return (y[:M_PROMPT].reshape(BATCH, SEQ, D_MODEL), y[M_PROMPT:].reshape(DEC_BATCH, 1, D_MODEL),
            shift_p, wkv_p[None], kv_p[0], kv_p[1], kv_p[2],
            shift_s, wkv_s[None], kv_s[0], kv_s[1], kv_s[2])
```

```python
import functools

import jax
import jax.numpy as jnp
from jax import lax
from jax.experimental import pallas as pl
from jax.experimental.pallas import tpu as pltpu

D_MODEL = 2048
BATCH = 4
SEQ = 2048
DEC_BATCH = 128
PAST_LEN = 2048
M_PROMPT = BATCH * SEQ
M_ALL = M_PROMPT + DEC_BATCH

RWKV_HEADS = 16
HEAD_DIM = 64
D_RWKV = RWKV_HEADS * HEAD_DIM
D_DECAY_LORA = 64
D_ICLR_LORA = 64
D_GATE_LORA = 160
D_RWKV_PROJ = 3 * D_RWKV + D_DECAY_LORA + D_ICLR_LORA + D_GATE_LORA
RWKV_PROJ_PAD = 3584
RWKV_LN_EPS = 64e-5

ATT_GROUPS = ((128, 1), (512, 4), (2048, 16))
HEADS_PER_GROUP = 4
ATT_HEADS = HEADS_PER_GROUP * len(ATT_GROUPS)
D_ATT = ATT_HEADS * HEAD_DIM
D_GROUP = HEADS_PER_GROUP * HEAD_DIM
N_KEYS_PAST = 128
ROPE_THETA = 10000.0
D_FF = 5632
NORM_EPS = 1e-6
NEG_INF = -1e30

LANES = 128
VMEM_LIMIT = 56 * 1024 * 1024

F32 = jnp.float32
BF16 = jnp.bfloat16


def _params(*sem):
    return pltpu.CompilerParams(dimension_semantics=sem, vmem_limit_bytes=VMEM_LIMIT)


def _const_spec(shape):
    nd = len(shape)
    return pl.BlockSpec(shape, lambda *_: (0,) * nd, pipeline_mode=pl.Buffered(1))


def _rms_to_bf16(x, g):
    ms = jnp.mean(x * x, axis=-1, keepdims=True)
    return (x * lax.rsqrt(ms + NORM_EPS) * g).astype(BF16)


def _segsum(x, e):
    hi = x.astype(BF16)
    lo = (x - hi.astype(F32)).astype(BF16)
    return (jnp.dot(hi, e, preferred_element_type=F32) + jnp.dot(lo, e, preferred_element_type=F32))


def _dot_split(x, w_hi, w_lo):
    hi = x.astype(BF16)
    lo = (x - hi.astype(F32)).astype(BF16)
    return (jnp.dot(hi, w_hi, preferred_element_type=F32) + jnp.dot(lo, w_hi, preferred_element_type=F32)
            + jnp.dot(hi, w_lo, preferred_element_type=F32))


FFN_TM = 640
FFN_TF = 512


FFN_SAMPLE_ROW0 = M_PROMPT % FFN_TM
assert FFN_TM - FFN_SAMPLE_ROW0 == DEC_BATCH


def _ffn_kernel(*refs, split_in, split_out):
    refs = list(refs)
    x_ref = refs.pop(0)
    xs_ref = refs.pop(0) if split_in else None
    g_ref, wg_ref, wu_ref, wo_ref = refs[:4]
    refs = refs[4:]
    gn_ref = refs.pop(0) if split_in else None
    o_ref = refs.pop(0)
    os_ref = refs.pop(0) if split_out else None
    hn_ref = refs.pop(0) if split_in else None
    h_ref = refs.pop(0)
    xin_ref = refs.pop(0) if split_in else x_ref
    j = pl.program_id(1)
    last_tile = pl.program_id(0) == pl.num_programs(0) - 1

    @pl.when(j == 0)
    def _():
        if split_in:
            @pl.when(jnp.logical_not(last_tile))
            def _():
                xin_ref[...] = x_ref[...]

            @pl.when(last_tile)
            def _():
                xin_ref[0:FFN_SAMPLE_ROW0] = x_ref[0:FFN_SAMPLE_ROW0]
                xin_ref[FFN_SAMPLE_ROW0:] = xs_ref[...]

        h_ref[...] = _rms_to_bf16(xin_ref[...], g_ref[...])
        o_ref[...] = jnp.zeros_like(o_ref)

    h = h_ref[...]
    gate = jnp.dot(h, wg_ref[...], preferred_element_type=F32)
    up = jnp.dot(h, wu_ref[...], preferred_element_type=F32)
    act = (gate * jax.nn.sigmoid(gate) * up).astype(BF16)
    o_ref[...] += jnp.dot(act, wo_ref[...], preferred_element_type=F32)

    @pl.when(j == pl.num_programs(1) - 1)
    def _():
        out = xin_ref[...] + 0.5 * o_ref[...]
        o_ref[...] = out
        if split_in:
            hn_ref[...] = _rms_to_bf16(out, gn_ref[...])
        if split_out:
            @pl.when(last_tile)
            def _():
                os_ref[...] = out[FFN_SAMPLE_ROW0:]


def _ffn(x, g, w_in, w_out, x_sample=None, next_norm=None, split_out=False):
    nj = D_FF // FFN_TF
    split_in = x_sample is not None
    row = pl.BlockSpec((1, D_MODEL), lambda i, j: (0, 0))
    tile = pl.BlockSpec((FFN_TM, D_MODEL), lambda i, j: (i, 0))
    sample = pl.BlockSpec((DEC_BATCH, D_MODEL), lambda i, j: (0, 0))
    weights = [pl.BlockSpec((D_MODEL, FFN_TF), lambda i, j: (0, j)),
               pl.BlockSpec((D_MODEL, FFN_TF), lambda i, j: (0, j + nj)),
               pl.BlockSpec((FFN_TF, D_MODEL), lambda i, j: (j, 0))]
    in_specs = [tile] + ([sample] if split_in else []) + [row] + weights + ([row] if split_in else [])
    args = [x] + ([x_sample] if split_in else []) + [g, w_in, w_in, w_out] + ([next_norm] if split_in else [])
    scratch = [pltpu.VMEM((FFN_TM, D_MODEL), BF16)] + ([pltpu.VMEM((FFN_TM, D_MODEL), F32)] if split_in else [])
    if split_out:
        out_specs = [tile, sample]
        out_shape = [jax.ShapeDtypeStruct((M_PROMPT, D_MODEL), F32), jax.ShapeDtypeStruct((DEC_BATCH, D_MODEL), F32)]
    else:
        out_specs = [tile, tile]
        out_shape = [jax.ShapeDtypeStruct((M_ALL, D_MODEL), F32), jax.ShapeDtypeStruct((M_ALL, D_MODEL), BF16)]
    return pl.pallas_call(
        functools.partial(_ffn_kernel, split_in=split_in, split_out=split_out),
        grid=(M_ALL // FFN_TM, nj),
        in_specs=in_specs,
        out_specs=out_specs,
        out_shape=out_shape,
        scratch_shapes=scratch,
        compiler_params=_params("arbitrary" if split_out else "parallel", "arbitrary"),
        name="ffn",
    )(*args)


PROJ_TM = 1664


def _proj_kernel(h_ref, w_ref, o_ref):
    o_ref[...] = jnp.dot(h_ref[...], w_ref[...], preferred_element_type=F32)


def _proj(h, w, tn, name):
    m, n = h.shape[0], w.shape[1]
    return pl.pallas_call(
        _proj_kernel,
        grid=(m // PROJ_TM, n // tn),
        in_specs=[
            pl.BlockSpec((PROJ_TM, D_MODEL), lambda i, j: (i, 0)),
            pl.BlockSpec((D_MODEL, tn), lambda i, j: (0, j)),
        ],
        out_specs=pl.BlockSpec((PROJ_TM, tn), lambda i, j: (i, j)),
        out_shape=jax.ShapeDtypeStruct((m, n), F32),
        compiler_params=_params("parallel", "arbitrary"),
        name=name,
    )(h, w)


PREP_TM = 128
PREP_TILES_PER_SEQ = SEQ // PREP_TM
PREP_SAMPLE_TILE = M_PROMPT // PREP_TM
KW_R, KW_DECAY, KW_K, KW_KK, KW_B = range(5)


def _prep_tile(i):
    return jnp.where(i == 0, PREP_SAMPLE_TILE, i - 1)


def _rwkv_prep_kernel(p_ref, tail_ref, state_ref, mu_ref, w0_ref, a0_ref, kk_ref, ka_ref, rk_ref,
                      wl_hi_ref, wl_lo_ref, wg_hi_ref, wg_lo_ref, e_ref,
                      kw_ref, v_ref, kws_ref, vs_ref, bonus_ref, g_ref):
    i = pl.program_id(0)
    is_sample = i == 0
    p = p_ref[...]
    tail = jnp.where((i - 1) % PREP_TILES_PER_SEQ == 0, 0.0, tail_ref[7:8, :])
    row = lax.broadcasted_iota(jnp.int32, p.shape, 0)
    prev = jnp.where(row == 0, tail, pltpu.roll(p, 1, axis=0))
    prev = jnp.where(is_sample, state_ref[...], prev)
    pm = p + (prev - p) * mu_ref[...]

    r = pm[:, 0:D_RWKV]
    k = pm[:, D_RWKV:2 * D_RWKV]
    v = pm[:, 2 * D_RWKV:3 * D_RWKV]
    lora = pm[:, 3 * D_RWKV:]
    lane = lax.broadcasted_iota(jnp.int32, (p.shape[0], LANES), 1)
    x_wa = lora[:, 0:LANES]
    x_wa = jnp.where(lane < D_DECAY_LORA, jnp.tanh(x_wa), x_wa)
    x_g = jax.nn.sigmoid(lora[:, LANES:3 * LANES])
    wa = _dot_split(x_wa, wl_hi_ref[...], wl_lo_ref[...])
    g = _dot_split(x_g, wg_hi_ref[...], wg_lo_ref[...])

    z = -(w0_ref[...] + wa[:, 0:D_RWKV])
    softplus = jnp.maximum(z, 0.0) + jnp.log(1.0 + jnp.exp(-jnp.abs(z)))
    decay = jnp.exp(-jnp.exp(-softplus - 0.5))
    a = jax.nn.sigmoid(a0_ref[...] + wa[:, D_RWKV:])
    e = e_ref[...]
    kk = k * kk_ref[...]
    kk = kk * lax.rsqrt(jnp.maximum(_segsum(kk * kk, e), 1e-24))
    k_hat = k * (1.0 + (a - 1.0) * ka_ref[...])
    kw = (r, decay, k_hat, kk, kk * a)
    bonus_ref[...] = _segsum(r * k_hat * rk_ref[...], e) * v
    g_ref[...] = g

    @pl.when(i > 0)
    def _():
        for idx in range(5):
            kw_ref[idx] = kw[idx]
        v_ref[...] = v

    @pl.when(is_sample)
    def _():
        for idx in range(5):
            kws_ref[idx] = kw[idx].T
        vs_ref[...] = v.T


def _rwkv_prep(p_rwkv, state_shift, mu, w0, a0, k_k, k_a, r_k, wl_hi, wl_lo, wg_hi, wg_lo, e_heads):
    n_tiles = M_ALL // PREP_TM
    row = lambda n: pl.BlockSpec((1, n), lambda i: (0, 0))
    prompt_tile = lambda i: jnp.maximum(i - 1, 0)
    return pl.pallas_call(
        _rwkv_prep_kernel,
        grid=(n_tiles,),
        in_specs=[
            pl.BlockSpec((PREP_TM, RWKV_PROJ_PAD), lambda i: (_prep_tile(i), 0)),
            pl.BlockSpec((8, RWKV_PROJ_PAD), lambda i: (jnp.maximum((i - 1) * (PREP_TM // 8) - 1, 0), 0)),
            pl.BlockSpec((DEC_BATCH, RWKV_PROJ_PAD), lambda i: (0, 0)),
            row(RWKV_PROJ_PAD), row(D_RWKV), row(D_RWKV), row(D_RWKV), row(D_RWKV), row(D_RWKV),
            _const_spec(wl_hi.shape), _const_spec(wl_lo.shape),
            _const_spec(wg_hi.shape), _const_spec(wg_lo.shape),
            _const_spec(e_heads.shape),
        ],
        out_specs=[
            pl.BlockSpec((5, PREP_TM, D_RWKV), lambda i: (0, prompt_tile(i), 0)),
            pl.BlockSpec((PREP_TM, D_RWKV), lambda i: (prompt_tile(i), 0)),
            pl.BlockSpec((5, D_RWKV, DEC_BATCH), lambda i: (0, 0, 0)),
            pl.BlockSpec((D_RWKV, DEC_BATCH), lambda i: (0, 0)),
            pl.BlockSpec((PREP_TM, D_RWKV), lambda i: (_prep_tile(i), 0)),
            pl.BlockSpec((PREP_TM, D_RWKV), lambda i: (_prep_tile(i), 0)),
        ],
        out_shape=[
            jax.ShapeDtypeStruct((5, M_PROMPT, D_RWKV), F32),
            jax.ShapeDtypeStruct((M_PROMPT, D_RWKV), F32),
            jax.ShapeDtypeStruct((5, D_RWKV, DEC_BATCH), F32),
            jax.ShapeDtypeStruct((D_RWKV, DEC_BATCH), F32),
            jax.ShapeDtypeStruct((M_ALL, D_RWKV), F32),
            jax.ShapeDtypeStruct((M_ALL, D_RWKV), F32),
        ],
        compiler_params=_params("arbitrary"),
        name="rwkv_prep",
    )(p_rwkv, p_rwkv, state_shift, mu, w0, a0, k_k, k_a, r_k, wl_hi, wl_lo, wg_hi, wg_lo, e_heads)


SEQ_TB = 64
K_HALF = HEAD_DIM // 2
BH = BATCH * RWKV_HEADS


def _rwkv_seq_kernel(kw_ref, v_ref, y_ref, sout_ref, s_ref):
    @pl.when(pl.program_id(0) == 0)
    def _():
        s_ref[...] = jnp.zeros_like(s_ref)

    sk = jnp.zeros((HEAD_DIM, LANES), F32)
    for kp in range(K_HALF):
        sk = sk + s_ref[kp] * kw_ref[0, KW_KK, pl.ds(kp, 1), :]

    n_vh = 4
    vh_rows = HEAD_DIM // n_vh

    def step(t, sk):
        t_next = jnp.minimum(t + 1, SEQ_TB - 1)
        sk_next = []
        for vh in range(n_vh):
            vs = pl.ds(vh * vh_rows, vh_rows)
            sk_h = sk[vh * vh_rows:(vh + 1) * vh_rows]
            sa = -(sk_h + pltpu.roll(sk_h, BH, axis=1))
            vt = v_ref[t, vs, :]
            y = jnp.zeros((vh_rows, LANES), F32)
            nxt = jnp.zeros((vh_rows, LANES), F32)
            for kp in range(K_HALF):
                s_new = (s_ref[kp, vs, :] * kw_ref[t, KW_DECAY, pl.ds(kp, 1), :]
                         + sa * kw_ref[t, KW_B, pl.ds(kp, 1), :]
                         + vt * kw_ref[t, KW_K, pl.ds(kp, 1), :])
                s_ref[kp, vs, :] = s_new
                y = y + s_new * kw_ref[t, KW_R, pl.ds(kp, 1), :]
                nxt = nxt + s_new * kw_ref[t_next, KW_KK, pl.ds(kp, 1), :]
            y_ref[t, vs, :] = y + pltpu.roll(y, BH, axis=1)
            sk_next.append(nxt)
        return jnp.concatenate(sk_next, axis=0)

    lax.fori_loop(0, SEQ_TB, step, sk)

    @pl.when(pl.program_id(0) == pl.num_programs(0) - 1)
    def _():
        sout_ref[...] = s_ref[...]


def _rwkv_seq(kw_t, v_t):
    return pl.pallas_call(
        _rwkv_seq_kernel,
        grid=(SEQ // SEQ_TB,),
        in_specs=[
            pl.BlockSpec((SEQ_TB, 5, K_HALF, LANES), lambda i: (i, 0, 0, 0)),
            pl.BlockSpec((SEQ_TB, HEAD_DIM, LANES), lambda i: (i, 0, 0)),
        ],
        out_specs=[
            pl.BlockSpec((SEQ_TB, HEAD_DIM, LANES), lambda i: (i, 0, 0)),
            pl.BlockSpec((K_HALF, HEAD_DIM, LANES), lambda i: (0, 0, 0)),
        ],
        out_shape=[
            jax.ShapeDtypeStruct((SEQ, HEAD_DIM, LANES), F32),
            jax.ShapeDtypeStruct((K_HALF, HEAD_DIM, LANES), F32),
        ],
        scratch_shapes=[pltpu.VMEM((K_HALF, HEAD_DIM, LANES), F32)],
        compiler_params=_params("arbitrary"),
        name="rwkv_seq",
    )(kw_t, v_t)


def _rwkv_step_kernel(kw_ref, v_ref, s_ref, y_ref, sout_ref):
    def body(vi, carry):
        s = s_ref[0, vi]
        sa = -jnp.sum(s * kw_ref[KW_KK], axis=0, keepdims=True)
        s_new = s * kw_ref[KW_DECAY] + sa * kw_ref[KW_B] + v_ref[pl.ds(vi, 1), :] * kw_ref[KW_K]
        sout_ref[0, vi] = s_new
        y_ref[pl.ds(vi, 1), :] = jnp.sum(s_new * kw_ref[KW_R], axis=0, keepdims=True)
        return carry

    lax.fori_loop(0, HEAD_DIM, body, 0)


def _rwkv_step(kw_t, v_t, state_t):
    return pl.pallas_call(
        _rwkv_step_kernel,
        grid=(RWKV_HEADS,),
        in_specs=[
            pl.BlockSpec((5, HEAD_DIM, DEC_BATCH), lambda h: (0, h, 0)),
            pl.BlockSpec((HEAD_DIM, DEC_BATCH), lambda h: (h, 0)),
            pl.BlockSpec((1, HEAD_DIM, HEAD_DIM, DEC_BATCH), lambda h: (h, 0, 0, 0)),
        ],
        out_specs=[
            pl.BlockSpec((HEAD_DIM, DEC_BATCH), lambda h: (h, 0)),
            pl.BlockSpec((1, HEAD_DIM, HEAD_DIM, DEC_BATCH), lambda h: (h, 0, 0, 0)),
        ],
        out_shape=[
            jax.ShapeDtypeStruct((D_RWKV, DEC_BATCH), F32),
            jax.ShapeDtypeStruct((RWKV_HEADS, HEAD_DIM, HEAD_DIM, DEC_BATCH), F32),
        ],
        compiler_params=_params("parallel"),
        name="rwkv_step",
    )(kw_t, v_t, state_t)


ATT_PREP_TM = 640


def _swap_halves(x):
    lane = lax.broadcasted_iota(jnp.int32, x.shape, 1)
    first = (lane % HEAD_DIM) < HEAD_DIM // 2
    return jnp.where(first, pltpu.roll(x, LANES - HEAD_DIM // 2, axis=1), pltpu.roll(x, HEAD_DIM // 2, axis=1))


PAIRS = D_GROUP // LANES
QKV_TILES = 3 * PAIRS


def _att_prep_kernel(p_ref, cos_ref, sin_ref, qg_ref, kg_ref, e_ref, o0_ref, o1_ref, o2_ref):
    cos = cos_ref[...]
    sin = sin_ref[...]
    e = e_ref[...]
    o_refs = (o0_ref, o1_ref, o2_ref)
    for part, g_ref in ((0, qg_ref), (1, kg_ref)):
        x = p_ref[:, part * D_ATT:(part + 1) * D_ATT]
        ms = _segsum(x * x, e) * (1.0 / HEAD_DIM)
        xn = x * lax.rsqrt(ms + NORM_EPS) * g_ref[...]
        for c in range(D_ATT // LANES):
            xc = xn[:, c * LANES:(c + 1) * LANES]
            o_refs[c // PAIRS][part * PAIRS + c % PAIRS] = xc * cos + _swap_halves(xc) * sin
    for c in range(D_ATT // LANES):
        o_refs[c // PAIRS][2 * PAIRS + c % PAIRS] = p_ref[:, 2 * D_ATT + c * LANES:2 * D_ATT + (c + 1) * LANES]


def _att_prep(p_att, cos, sin, q_gain, k_gain, e_att):
    out = jax.ShapeDtypeStruct((QKV_TILES, M_ALL, LANES), F32)
    return pl.pallas_call(
        _att_prep_kernel,
        grid=(M_ALL // ATT_PREP_TM,),
        in_specs=[
            pl.BlockSpec((ATT_PREP_TM, 3 * D_ATT), lambda i: (i, 0)),
            pl.BlockSpec((ATT_PREP_TM, LANES), lambda i: (i, 0)),
            pl.BlockSpec((ATT_PREP_TM, LANES), lambda i: (i, 0)),
            pl.BlockSpec((1, D_ATT), lambda i: (0, 0)),
            pl.BlockSpec((1, D_ATT), lambda i: (0, 0)),
            _const_spec(e_att.shape),
        ],
        out_specs=[pl.BlockSpec((QKV_TILES, ATT_PREP_TM, LANES), lambda i: (0, i, 0))] * 3,
        out_shape=[out] * 3,
        compiler_params=_params("parallel"),
        name="att_prep",
    )(p_att, cos, sin, q_gain, k_gain, e_att)


ATT_TQ = 128


def _att_prompt_kernel(qkv_ref, o_ref, *, dil):
    scale = HEAD_DIM ** -0.5
    ls = SEQ // dil

    def heads(part, rows):
        pairs = [qkv_ref[part * PAIRS + pr, rows, :] for pr in range(PAIRS)]
        return jnp.stack([x[:, h * HEAD_DIM:(h + 1) * HEAD_DIM] for x in pairs for h in range(LANES // HEAD_DIM)],
                         axis=0)

    for c in range(dil):
        for qi in range(ls // ATT_TQ):
            rows = pl.ds(c + dil * qi * ATT_TQ, ATT_TQ, stride=dil)
            q = heads(0, rows).astype(BF16)
            k = heads(1, rows)
            v = heads(2, rows)
            q_off = 0
            if qi > 0:
                prev = pl.ds(c + dil * (qi - 1) * ATT_TQ, ATT_TQ, stride=dil)
                k = jnp.concatenate([heads(1, prev), k], axis=1)
                v = jnp.concatenate([heads(2, prev), v], axis=1)
                q_off = ATT_TQ
            s = jnp.einsum("hqd,hkd->hqk", q, k.astype(BF16), preferred_element_type=F32) * scale
            dist = (lax.broadcasted_iota(jnp.int32, s.shape, 1) + q_off) - lax.broadcasted_iota(jnp.int32, s.shape, 2)
            s = jnp.where((dist >= 0) & (dist <= N_KEYS_PAST), s, NEG_INF)
            m = jnp.max(s, axis=-1, keepdims=True)
            p = jnp.exp(s - m)
            l = jnp.sum(p, axis=-1, keepdims=True)
            lse = jnp.broadcast_to(m + jnp.log(l), (HEADS_PER_GROUP, ATT_TQ, HEAD_DIM))
            o = jnp.einsum("hqk,hkd->hqd", (p / l).astype(BF16), v.astype(BF16), preferred_element_type=F32)
            for h in range(HEADS_PER_GROUP):
                o_ref[h, rows, :] = jnp.concatenate([o[h], lse[h]], axis=1)


def _att_prompt(qkv, dil):
    return pl.pallas_call(
        functools.partial(_att_prompt_kernel, dil=dil),
        grid=(BATCH,),
        in_specs=[pl.BlockSpec((QKV_TILES, SEQ, LANES), lambda b: (0, b, 0))],
        out_specs=pl.BlockSpec((HEADS_PER_GROUP, SEQ, LANES), lambda b: (0, b, 0)),
        out_shape=jax.ShapeDtypeStruct((HEADS_PER_GROUP, M_PROMPT, LANES), F32),
        compiler_params=_params("parallel"),
        name=f"att_prompt_d{dil}",
    )(qkv)


CACHE_BATCH_PER_STEP = {128: 16, 512: 8, 2048: 2}


def _cache_attn_kernel(qkv_ref, c_ref, o_ref, cout_ref, *, dil, nb):
    n_tiles = c_ref.shape[-1] // LANES
    scale = HEAD_DIM ** -0.5
    r_i = lax.broadcasted_iota(jnp.int32, (HEAD_DIM, HEAD_DIM), 0)
    c_i = lax.broadcasted_iota(jnp.int32, (HEAD_DIM, HEAD_DIM), 1)
    eye = (r_i == c_i).astype(F32)
    on_grid = (lax.broadcasted_iota(jnp.int32, (1, LANES), 1) % dil) == 0
    last_lane = lax.broadcasted_iota(jnp.int32, (HEAD_DIM, LANES), 1) == LANES - 1

    def to_col(rows):
        return jnp.sum(eye * rows, axis=2, keepdims=True)

    rows = qkv_ref[...]
    parts = []
    for h in range(HEADS_PER_GROUP):
        q = rows[:, :, h * HEAD_DIM:(h + 1) * HEAD_DIM]
        k_new = rows[:, :, D_GROUP + h * HEAD_DIM:D_GROUP + (h + 1) * HEAD_DIM]
        v_new = rows[:, :, 2 * D_GROUP + h * HEAD_DIM:2 * D_GROUP + (h + 1) * HEAD_DIM]
        q_col, k_col, v_col = to_col(q), to_col(k_new), to_col(v_new)
        s_tiles = [None] * n_tiles
        nxt = jnp.broadcast_to(k_col, (nb, HEAD_DIM, LANES))
        for j in reversed(range(n_tiles)):
            ts = slice(j * LANES, (j + 1) * LANES)
            kt = c_ref[:, 0, h, :, ts]
            s = jnp.sum(kt * q_col, axis=1, keepdims=True) * scale
            s_tiles[j] = jnp.where(on_grid, s, NEG_INF)
            rolled = pltpu.roll(kt, LANES - 1, axis=2)
            cout_ref[:, 0, h, :, ts] = jnp.where(last_lane, nxt, rolled)
            nxt = rolled
        s_new = jnp.sum(k_new * q, axis=2, keepdims=True) * scale
        m_row = s_tiles[0]
        for j in range(1, n_tiles):
            m_row = jnp.maximum(m_row, s_tiles[j])
        m = jnp.maximum(jnp.max(m_row, axis=2, keepdims=True), s_new)
        p_new = jnp.exp(s_new - m)
        l_row = jnp.zeros((nb, 1, LANES), F32)
        acc = jnp.zeros((nb, HEAD_DIM, LANES), F32)
        nxt = jnp.broadcast_to(v_col, (nb, HEAD_DIM, LANES))
        for j in reversed(range(n_tiles)):
            ts = slice(j * LANES, (j + 1) * LANES)
            p = jnp.exp(s_tiles[j] - m)
            l_row = l_row + p
            vt = c_ref[:, 1, h, :, ts]
            acc = acc + vt * p
            rolled = pltpu.roll(vt, LANES - 1, axis=2)
            cout_ref[:, 1, h, :, ts] = jnp.where(last_lane, nxt, rolled)
            nxt = rolled
        l = jnp.sum(l_row, axis=2, keepdims=True) + p_new
        o_col = (jnp.sum(acc, axis=2, keepdims=True) + p_new * v_col) / l
        parts += [jnp.sum(eye * o_col, axis=1, keepdims=True), jnp.broadcast_to(m + jnp.log(l), (nb, 1, HEAD_DIM))]
    o_ref[...] = jnp.concatenate(parts, axis=2)


def _cache_attn(qkv, cache_t, dil):
    window = cache_t.shape[-1]
    nb = CACHE_BATCH_PER_STEP[window]
    blk = (nb, 2, HEADS_PER_GROUP, HEAD_DIM, window)
    return pl.pallas_call(
        functools.partial(_cache_attn_kernel, dil=dil, nb=nb),
        grid=(DEC_BATCH // nb,),
        in_specs=[pl.BlockSpec((nb, 1, 3 * D_GROUP), lambda i: (i, 0, 0)),
                  pl.BlockSpec(blk, lambda i: (i, 0, 0, 0, 0))],
        out_specs=[pl.BlockSpec((nb, 1, HEADS_PER_GROUP * LANES), lambda i: (i, 0, 0)),
                   pl.BlockSpec(blk, lambda i: (i, 0, 0, 0, 0))],
        out_shape=[jax.ShapeDtypeStruct((DEC_BATCH, 1, HEADS_PER_GROUP * LANES), F32),
                   jax.ShapeDtypeStruct(cache_t.shape, F32)],
        compiler_params=_params("parallel"),
        name=f"cache_attn_w{window}",
    )(qkv, cache_t)


MERGE_TM = 208


def _merge_kernel(x_ref, y_ref, bonus_ref, g_ref, a0_ref, a1_ref, a2_ref, pg_ref, bg_ref,
                  lnw_ref, lnb_ref, e_ref, wbr_ref, wba0_ref, wba1_ref, wba2_ref, wo_ref, o_ref):
    e = e_ref[...]
    inv_n = 1.0 / HEAD_DIM
    y = y_ref[...]
    d = y - _segsum(y, e) * inv_n
    var = _segsum(d * d, e) * inv_n
    yn = d * lax.rsqrt(var + RWKV_LN_EPS) * lnw_ref[...] + lnb_ref[...]
    o_a = ((yn + bonus_ref[...]) * g_ref[...]).astype(BF16)
    br_a = jnp.dot(o_a, wbr_ref[...], preferred_element_type=F32)
    a = (a0_ref[...], a1_ref[...], a2_ref[...])
    m = jnp.maximum(jnp.maximum(a[0], a[1]), a[2])
    ex = [jnp.exp(t - m) for t in a]
    inv = 1.0 / (ex[0] + ex[1] + ex[2])
    br_b = jnp.zeros_like(br_a)
    for grp, wb_ref in enumerate((wba0_ref, wba1_ref, wba2_ref)):
        alpha = ex[grp] * inv
        parts = [a[grp][h] * pltpu.roll(alpha[h], HEAD_DIM, axis=1) for h in range(HEADS_PER_GROUP)]
        xg = jnp.concatenate(parts, axis=1).astype(BF16)
        br_b = br_b + jnp.dot(xg, wb_ref[...], preferred_element_type=F32)
    gates = jax.nn.sigmoid(pg_ref[...] + bg_ref[...])
    merged = (gates[:, 0:D_MODEL] * br_a + gates[:, D_MODEL:] * br_b).astype(BF16)
    o_ref[...] = x_ref[...] + jnp.dot(merged, wo_ref[...], preferred_element_type=F32)


def _merge(x, y, bonus, g, att, p_gate, b_gate, ln_w, ln_b, e_heads, w_br, w_ba, w_o):
    tile = lambda n: pl.BlockSpec((MERGE_TM, n), lambda i: (i, 0))
    row = lambda n: pl.BlockSpec((1, n), lambda i: (0, 0))
    att_tile = pl.BlockSpec((HEADS_PER_GROUP, MERGE_TM, LANES), lambda i: (0, i, 0))
    return pl.pallas_call(
        _merge_kernel,
        grid=(M_ALL // MERGE_TM,),
        in_specs=[tile(D_MODEL), tile(D_RWKV), tile(D_RWKV), tile(D_RWKV),
                  att_tile, att_tile, att_tile, tile(2 * D_MODEL), row(2 * D_MODEL),
                  row(D_RWKV), row(D_RWKV), _const_spec(e_heads.shape), _const_spec(w_br.shape),
                  _const_spec(w_ba[0].shape), _const_spec(w_ba[1].shape), _const_spec(w_ba[2].shape),
                  _const_spec(w_o.shape)],
        out_specs=tile(D_MODEL),
        out_shape=jax.ShapeDtypeStruct((M_ALL, D_MODEL), F32),
        compiler_params=_params("parallel"),
        name="merge",
    )(x, y, bonus, g, att[0], att[1], att[2], p_gate, b_gate, ln_w, ln_b, e_heads, w_br,
      w_ba[0], w_ba[1], w_ba[2], w_o)


def _split_bf16(w):
    hi = w.astype(BF16)
    return hi, (w - hi.astype(F32)).astype(BF16)


def _head_ones(n_heads):
    return jnp.kron(jnp.eye(n_heads, dtype=F32), jnp.ones((HEAD_DIM, HEAD_DIM), F32)).astype(BF16)


def _rotary_tables():
    half = HEAD_DIM // 2
    inv_freq = ROPE_THETA ** (-jnp.arange(half, dtype=F32) / half)
    pos = jnp.concatenate([jnp.tile(jnp.arange(SEQ), BATCH), jnp.full((DEC_BATCH,), PAST_LEN)]).astype(F32)
    ang = pos[:, None] * inv_freq[None, :]
    cos, sin = jnp.cos(ang), jnp.sin(ang)
    cos = jnp.concatenate([cos, cos] * (LANES // HEAD_DIM), axis=1)
    sin = jnp.concatenate([-sin, sin] * (LANES // HEAD_DIM), axis=1)
    return cos, sin


def kernel(x_prompt, x_sample, state_rwkv_shift, state_rwkv_wkv, cache_att_w128, cache_att_w512, cache_att_w2048,
           ffn1_norm, ffn1_w_in, ffn1_w_out, mix_norm, w_in, b_gate, rwkv_mu, rwkv_w0, rwkv_w2, rwkv_a0, rwkv_a2,
           rwkv_g2, rwkv_k_k, rwkv_k_a, rwkv_r_k, rwkv_ln_w, rwkv_ln_b, w_branch_rwkv, attn_q_norm, attn_k_norm,
           w_branch_attn, w_out, ffn2_norm, ffn2_w_in, ffn2_w_out):
    caches = (cache_att_w128, cache_att_w512, cache_att_w2048)

    w_mix = w_in[0]
    pad = RWKV_PROJ_PAD - D_RWKV_PROJ
    w_rwkv = jnp.pad(w_mix[:, :D_RWKV_PROJ], ((0, 0), (0, pad))).astype(BF16)
    w_att = w_mix[:, D_RWKV_PROJ:D_RWKV_PROJ + 3 * D_ATT].astype(BF16)
    w_gate = w_mix[:, D_RWKV_PROJ + 3 * D_ATT:].astype(BF16)
    mu = jnp.pad(rwkv_mu, ((0, 0), (0, pad)))
    state_shift = jnp.pad(state_rwkv_shift[0], ((0, 0), (0, pad)))
    zeros = jnp.zeros((D_DECAY_LORA, D_RWKV), F32)
    w_lora = jnp.concatenate([jnp.concatenate([rwkv_w2[0], zeros], axis=1),
                              jnp.concatenate([zeros, rwkv_a2[0]], axis=1)], axis=0)
    w_glora = jnp.pad(rwkv_g2[0], ((0, 2 * LANES - D_GATE_LORA), (0, 0)))
    wl_hi, wl_lo = _split_bf16(w_lora)
    wg_hi, wg_lo = _split_bf16(w_glora)
    e_heads = _head_ones(RWKV_HEADS)
    e_att = _head_ones(ATT_HEADS)
    cos, sin = _rotary_tables()
    q_gain = jnp.tile(attn_q_norm, (1, ATT_HEADS))
    k_gain = jnp.tile(attn_k_norm, (1, ATT_HEADS))
    wba = w_branch_attn[0].reshape(len(ATT_GROUPS), HEADS_PER_GROUP, HEAD_DIM, D_MODEL)
    wba = jnp.pad(wba, ((0, 0), (0, 0), (0, LANES - HEAD_DIM), (0, 0))).astype(BF16)
    wba = wba.reshape(len(ATT_GROUPS), HEADS_PER_GROUP * LANES, D_MODEL)

    x1, h_mix = _ffn(x_prompt.reshape(M_PROMPT, D_MODEL), ffn1_norm, ffn1_w_in[0].astype(BF16),
                     ffn1_w_out[0].astype(BF16), x_sample=x_sample.reshape(DEC_BATCH, D_MODEL), next_norm=mix_norm)

    p_rwkv = _proj(h_mix, w_rwkv, 512, "proj_rwkv")
    kw, v_r, kw_s_t, v_s_t, bonus, g_r = _rwkv_prep(p_rwkv, state_shift, mu, rwkv_w0, rwkv_a0, rwkv_k_k, rwkv_k_a,
                                                    rwkv_r_k.reshape(1, D_RWKV), wl_hi, wl_lo, wg_hi, wg_lo, e_heads)
    kw, v_r, h_att = lax.optimization_barrier((kw, v_r, h_mix))
    kw_t = kw.reshape(5, BATCH, SEQ, RWKV_HEADS, 2, K_HALF)
    kw_t = kw_t.transpose(2, 0, 5, 4, 1, 3).reshape(SEQ, 5, K_HALF, LANES)
    v_t = v_r.reshape(BATCH, SEQ, RWKV_HEADS, HEAD_DIM).transpose(1, 3, 0, 2).reshape(SEQ, HEAD_DIM, BH)
    v_t = jnp.concatenate([v_t, v_t], axis=-1)

    p_att = _proj(h_att, w_att, 768, "proj_att")
    p_gate = _proj(h_att, w_gate, 512, "proj_gate")
    qkv = _att_prep(p_att, cos, sin, q_gain, k_gain, e_att)
    att, kv_p, kv_s = [], [], []
    for g, (window, dil) in enumerate(ATT_GROUPS):
        att_p = _att_prompt(qkv[g], dil)
        keep = min(window, SEQ)
        kv = qkv[g][PAIRS:, :M_PROMPT].reshape(2, PAIRS, BATCH, SEQ, LANES // HEAD_DIM, HEAD_DIM)
        kv = kv.transpose(2, 3, 0, 1, 4, 5).reshape(BATCH, SEQ, 2, HEADS_PER_GROUP, HEAD_DIM)
        kv_p.append(kv[:, SEQ - keep:][None])
        qkv_s = qkv[g][:, M_PROMPT:].transpose(1, 0, 2).reshape(DEC_BATCH, 1, 3 * D_GROUP)
        o_s, cache_new = _cache_attn(qkv_s, caches[g][0].transpose(0, 2, 3, 4, 1), dil)
        att_s = o_s.reshape(DEC_BATCH, HEADS_PER_GROUP, LANES).transpose(1, 0, 2)
        att.append(jnp.concatenate([att_p, att_s], axis=1))
        kv_s.append(cache_new.transpose(0, 4, 1, 2, 3)[None])

    kw_t, v_t, att = lax.optimization_barrier((kw_t, v_t, att))
    y_t, s_t = _rwkv_seq(kw_t, v_t)
    y_p = y_t[:, :, :BH].reshape(SEQ, HEAD_DIM, BATCH, RWKV_HEADS).transpose(2, 0, 3, 1).reshape(M_PROMPT, D_RWKV)
    wkv_p = s_t.reshape(K_HALF, HEAD_DIM, 2, BATCH, RWKV_HEADS).transpose(3, 4, 1, 2, 0)
    wkv_p = wkv_p.reshape(BATCH, RWKV_HEADS, HEAD_DIM, HEAD_DIM)
    y_s_t, wkv_s_t = _rwkv_step(kw_s_t, v_s_t, state_rwkv_wkv[0].transpose(1, 2, 3, 0))
    wkv_s = wkv_s_t.transpose(3, 0, 1, 2)
    y_r = jnp.concatenate([y_p, y_s_t.T], axis=0)

    x2 = _merge(x1, y_r, bonus, g_r, att, p_gate, b_gate, rwkv_ln_w, rwkv_ln_b,
                e_heads, w_branch_rwkv[0].astype(BF16), wba, w_out[0].astype(BF16))
    y_prompt, y_sample = _ffn(x2, ffn2_norm, ffn2_w_in[0].astype(BF16), ffn2_w_out[0].astype(BF16), split_out=True)

    shift_p = p_rwkv[SEQ - 1:M_PROMPT:SEQ, :D_RWKV_PROJ][None]
    shift_s = p_rwkv[M_PROMPT:, :D_RWKV_PROJ][None]
    return (y_prompt.reshape(BATCH, SEQ, D_MODEL), y_sample.reshape(DEC_BATCH, 1, D_MODEL),
            shift_p, wkv_p[None], kv_p[0], kv_p[1], kv_p[2],
            shift_s, wkv_s[None], kv_s[0], kv_s[1], kv_s[2])
```

```python
import functools

import jax
import jax.numpy as jnp
from jax import lax
from jax.experimental import pallas as pl
from jax.experimental.pallas import tpu as pltpu

D_MODEL = 2048
BATCH = 4
SEQ = 2048
DEC_BATCH = 128
PAST_LEN = 2048
M_PROMPT = BATCH * SEQ
M_ALL = M_PROMPT + DEC_BATCH

RWKV_HEADS = 16
HEAD_DIM = 64
D_RWKV = RWKV_HEADS * HEAD_DIM
D_DECAY_LORA = 64
D_ICLR_LORA = 64
D_GATE_LORA = 160
D_RWKV_PROJ = 3 * D_RWKV + D_DECAY_LORA + D_ICLR_LORA + D_GATE_LORA
RWKV_PROJ_PAD = 3584
RWKV_LN_EPS = 64e-5

ATT_GROUPS = ((128, 1), (512, 4), (2048, 16))
HEADS_PER_GROUP = 4
ATT_HEADS = HEADS_PER_GROUP * len(ATT_GROUPS)
D_ATT = ATT_HEADS * HEAD_DIM
D_GROUP = HEADS_PER_GROUP * HEAD_DIM
N_KEYS_PAST = 128
ROPE_THETA = 10000.0
D_FF = 5632
NORM_EPS = 1e-6
NEG_INF = -1e30

LANES = 128
VMEM_LIMIT = 56 * 1024 * 1024

F32 = jnp.float32
BF16 = jnp.bfloat16


def _params(*sem):
    return pltpu.CompilerParams(dimension_semantics=sem, vmem_limit_bytes=VMEM_LIMIT)


def _const_spec(shape):
    nd = len(shape)
    return pl.BlockSpec(shape, lambda *_: (0,) * nd, pipeline_mode=pl.Buffered(1))


def _rms_to_bf16(x, g):
    ms = jnp.mean(x * x, axis=-1, keepdims=True)
    return (x * lax.rsqrt(ms + NORM_EPS) * g).astype(BF16)


def _split_dot(x, w):
    hi = x.astype(BF16)
    lo = (x - hi.astype(F32)).astype(BF16)
    return jnp.dot(hi, w, preferred_element_type=F32) + jnp.dot(lo, w, preferred_element_type=F32)


def _segsum(x, e):
    if isinstance(e, tuple):
        return _split_dot(_split_dot(x, e[0]), e[1])
    return _split_dot(x, e)


def _dot_split(x, w_hi, w_lo):
    hi = x.astype(BF16)
    lo = (x - hi.astype(F32)).astype(BF16)
    return (jnp.dot(hi, w_hi, preferred_element_type=F32) + jnp.dot(lo, w_hi, preferred_element_type=F32)
            + jnp.dot(hi, w_lo, preferred_element_type=F32))


FFN_TM = 640
FFN_TF = 512


FFN_SAMPLE_ROW0 = M_PROMPT % FFN_TM
assert FFN_TM - FFN_SAMPLE_ROW0 == DEC_BATCH


def _ffn_kernel(*refs, split_in, split_out):
    refs = list(refs)
    x_ref = refs.pop(0)
    xs_ref = refs.pop(0) if split_in else None
    g_ref, wg_ref, wu_ref, wo_ref = refs[:4]
    refs = refs[4:]
    gn_ref = refs.pop(0) if split_in else None
    o_ref = refs.pop(0)
    os_ref = refs.pop(0) if split_out else None
    hn_ref = refs.pop(0) if split_in else None
    h_ref = refs.pop(0)
    xin_ref = refs.pop(0) if split_in else x_ref
    j = pl.program_id(1)
    last_tile = pl.program_id(0) == pl.num_programs(0) - 1

    @pl.when(j == 0)
    def _():
        if split_in:
            @pl.when(jnp.logical_not(last_tile))
            def _():
                xin_ref[...] = x_ref[...]

            @pl.when(last_tile)
            def _():
                xin_ref[0:FFN_SAMPLE_ROW0] = x_ref[0:FFN_SAMPLE_ROW0]
                xin_ref[FFN_SAMPLE_ROW0:] = xs_ref[...]

        h_ref[...] = _rms_to_bf16(xin_ref[...], g_ref[...])
        o_ref[...] = jnp.zeros_like(o_ref)

    h = h_ref[...]
    gate = jnp.dot(h, wg_ref[...], preferred_element_type=F32)
    up = jnp.dot(h, wu_ref[...], preferred_element_type=F32)
    act = (gate * jax.nn.sigmoid(gate) * up).astype(BF16)
    o_ref[...] += jnp.dot(act, wo_ref[...], preferred_element_type=F32)

    @pl.when(j == pl.num_programs(1) - 1)
    def _():
        out = xin_ref[...] + 0.5 * o_ref[...]
        o_ref[...] = out
        if split_in:
            hn_ref[...] = _rms_to_bf16(out, gn_ref[...])
        if split_out:
            @pl.when(last_tile)
            def _():
                os_ref[...] = out[FFN_SAMPLE_ROW0:]


def _ffn(x, g, w_in, w_out, x_sample=None, next_norm=None, split_out=False):
    nj = D_FF // FFN_TF
    split_in = x_sample is not None
    row = pl.BlockSpec((1, D_MODEL), lambda i, j: (0, 0))
    tile = pl.BlockSpec((FFN_TM, D_MODEL), lambda i, j: (i, 0))
    sample = pl.BlockSpec((DEC_BATCH, D_MODEL), lambda i, j: (0, 0))
    weights = [pl.BlockSpec((D_MODEL, FFN_TF), lambda i, j: (0, j)),
               pl.BlockSpec((D_MODEL, FFN_TF), lambda i, j: (0, j + nj)),
               pl.BlockSpec((FFN_TF, D_MODEL), lambda i, j: (j, 0))]
    in_specs = [tile] + ([sample] if split_in else []) + [row] + weights + ([row] if split_in else [])
    args = [x] + ([x_sample] if split_in else []) + [g, w_in, w_in, w_out] + ([next_norm] if split_in else [])
    scratch = [pltpu.VMEM((FFN_TM, D_MODEL), BF16)] + ([pltpu.VMEM((FFN_TM, D_MODEL), F32)] if split_in else [])
    if split_out:
        out_specs = [tile, sample]
        out_shape = [jax.ShapeDtypeStruct((M_PROMPT, D_MODEL), F32), jax.ShapeDtypeStruct((DEC_BATCH, D_MODEL), F32)]
    else:
        out_specs = [tile, tile]
        out_shape = [jax.ShapeDtypeStruct((M_ALL, D_MODEL), F32), jax.ShapeDtypeStruct((M_ALL, D_MODEL), BF16)]
    return pl.pallas_call(
        functools.partial(_ffn_kernel, split_in=split_in, split_out=split_out),
        grid=(M_ALL // FFN_TM, nj),
        in_specs=in_specs,
        out_specs=out_specs,
        out_shape=out_shape,
        scratch_shapes=scratch,
        compiler_params=_params("arbitrary" if split_out else "parallel", "arbitrary"),
        name="ffn",
    )(*args)


PROJ_TM = 1664


def _proj_kernel(h_ref, w_ref, o_ref):
    o_ref[...] = jnp.dot(h_ref[...], w_ref[...], preferred_element_type=F32)


def _proj(h, w, tn, name):
    m, n = h.shape[0], w.shape[1]
    return pl.pallas_call(
        _proj_kernel,
        grid=(m // PROJ_TM, n // tn),
        in_specs=[
            pl.BlockSpec((PROJ_TM, D_MODEL), lambda i, j: (i, 0)),
            pl.BlockSpec((D_MODEL, tn), lambda i, j: (0, j)),
        ],
        out_specs=pl.BlockSpec((PROJ_TM, tn), lambda i, j: (i, j)),
        out_shape=jax.ShapeDtypeStruct((m, n), F32),
        compiler_params=_params("parallel", "arbitrary"),
        name=name,
    )(h, w)


PREP_TM = 128
PREP_TILES_PER_SEQ = SEQ // PREP_TM
PREP_SAMPLE_TILE = M_PROMPT // PREP_TM
KW_R, KW_DECAY, KW_K, KW_KK, KW_B = range(5)


def _prep_tile(i):
    return jnp.where(i == 0, PREP_SAMPLE_TILE, i - 1)


def _rwkv_prep_kernel(p_ref, tail_ref, state_ref, mu_ref, w0_ref, a0_ref, kk_ref, ka_ref, rk_ref,
                      wl_hi_ref, wl_lo_ref, wg_hi_ref, wg_lo_ref, er_ref, eb_ref,
                      kw_ref, v_ref, kws_ref, vs_ref, bonus_ref, g_ref):
    i = pl.program_id(0)
    is_sample = i == 0
    p = p_ref[...]
    tail = jnp.where((i - 1) % PREP_TILES_PER_SEQ == 0, 0.0, tail_ref[7:8, :])
    row = lax.broadcasted_iota(jnp.int32, p.shape, 0)
    prev = jnp.where(row == 0, tail, pltpu.roll(p, 1, axis=0))
    prev = jnp.where(is_sample, state_ref[...], prev)
    pm = p + (prev - p) * mu_ref[...]

    r = pm[:, 0:D_RWKV]
    k = pm[:, D_RWKV:2 * D_RWKV]
    v = pm[:, 2 * D_RWKV:3 * D_RWKV]
    lora = pm[:, 3 * D_RWKV:]
    lane = lax.broadcasted_iota(jnp.int32, (p.shape[0], LANES), 1)
    x_wa = lora[:, 0:LANES]
    x_wa = jnp.where(lane < D_DECAY_LORA, jnp.tanh(x_wa), x_wa)
    x_g = jax.nn.sigmoid(lora[:, LANES:3 * LANES])
    wa = _dot_split(x_wa, wl_hi_ref[...], wl_lo_ref[...])
    g = _dot_split(x_g, wg_hi_ref[...], wg_lo_ref[...])

    z = -(w0_ref[...] + wa[:, 0:D_RWKV])
    softplus = jnp.maximum(z, 0.0) + jnp.log(1.0 + jnp.exp(-jnp.abs(z)))
    decay = jnp.exp(-jnp.exp(-softplus - 0.5))
    a = jax.nn.sigmoid(a0_ref[...] + wa[:, D_RWKV:])
    e = (er_ref[...], eb_ref[...])
    kk = k * kk_ref[...]
    kk = kk * lax.rsqrt(jnp.maximum(_segsum(kk * kk, e), 1e-24))
    k_hat = k * (1.0 + (a - 1.0) * ka_ref[...])
    kw = (r, decay, k_hat, kk, kk * a)
    bonus_ref[...] = _segsum(r * k_hat * rk_ref[...], e) * v
    g_ref[...] = g

    @pl.when(i > 0)
    def _():
        for idx in range(5):
            kw_ref[idx] = kw[idx]
        v_ref[...] = v

    @pl.when(is_sample)
    def _():
        for idx in range(5):
            kws_ref[idx] = kw[idx].T
        vs_ref[...] = v.T


def _rwkv_prep(p_rwkv, state_shift, mu, w0, a0, k_k, k_a, r_k, wl_hi, wl_lo, wg_hi, wg_lo, e_heads):
    n_tiles = M_ALL // PREP_TM
    row = lambda n: pl.BlockSpec((1, n), lambda i: (0, 0))
    prompt_tile = lambda i: jnp.maximum(i - 1, 0)
    return pl.pallas_call(
        _rwkv_prep_kernel,
        grid=(n_tiles,),
        in_specs=[
            pl.BlockSpec((PREP_TM, RWKV_PROJ_PAD), lambda i: (_prep_tile(i), 0)),
            pl.BlockSpec((8, RWKV_PROJ_PAD), lambda i: (jnp.maximum((i - 1) * (PREP_TM // 8) - 1, 0), 0)),
            pl.BlockSpec((DEC_BATCH, RWKV_PROJ_PAD), lambda i: (0, 0)),
            row(RWKV_PROJ_PAD), row(D_RWKV), row(D_RWKV), row(D_RWKV), row(D_RWKV), row(D_RWKV),
            _const_spec(wl_hi.shape), _const_spec(wl_lo.shape),
            _const_spec(wg_hi.shape), _const_spec(wg_lo.shape),
            _const_spec(e_heads[0].shape), _const_spec(e_heads[1].shape),
        ],
        out_specs=[
            pl.BlockSpec((5, PREP_TM, D_RWKV), lambda i: (0, prompt_tile(i), 0)),
            pl.BlockSpec((PREP_TM, D_RWKV), lambda i: (prompt_tile(i), 0)),
            pl.BlockSpec((5, D_RWKV, DEC_BATCH), lambda i: (0, 0, 0)),
            pl.BlockSpec((D_RWKV, DEC_BATCH), lambda i: (0, 0)),
            pl.BlockSpec((PREP_TM, D_RWKV), lambda i: (_prep_tile(i), 0)),
            pl.BlockSpec((PREP_TM, D_RWKV), lambda i: (_prep_tile(i), 0)),
        ],
        out_shape=[
            jax.ShapeDtypeStruct((5, M_PROMPT, D_RWKV), F32),
            jax.ShapeDtypeStruct((M_PROMPT, D_RWKV), F32),
            jax.ShapeDtypeStruct((5, D_RWKV, DEC_BATCH), F32),
            jax.ShapeDtypeStruct((D_RWKV, DEC_BATCH), F32),
            jax.ShapeDtypeStruct((M_ALL, D_RWKV), F32),
            jax.ShapeDtypeStruct((M_ALL, D_RWKV), F32),
        ],
        compiler_params=_params("arbitrary"),
        name="rwkv_prep",
    )(p_rwkv, p_rwkv, state_shift, mu, w0, a0, k_k, k_a, r_k, wl_hi, wl_lo, wg_hi, wg_lo, *e_heads)


SEQ_TB = 64
K_HALF = HEAD_DIM // 2
BH = BATCH * RWKV_HEADS


def _rwkv_seq_kernel(kw_ref, vin_ref, y_ref, sout_ref, s_ref, v_ref):
    @pl.when(pl.program_id(0) == 0)
    def _():
        s_ref[...] = jnp.zeros_like(s_ref)

    vin = vin_ref[...]
    v_ref[...] = jnp.concatenate([vin, vin], axis=-1)

    sk = jnp.zeros((HEAD_DIM, LANES), F32)
    for kp in range(K_HALF):
        sk = sk + s_ref[kp] * kw_ref[0, KW_KK, pl.ds(kp, 1), :]

    n_vh = 4
    vh_rows = HEAD_DIM // n_vh

    def step(t, sk):
        t_next = jnp.minimum(t + 1, SEQ_TB - 1)
        sk_next = []
        for vh in range(n_vh):
            vs = pl.ds(vh * vh_rows, vh_rows)
            sk_h = sk[vh * vh_rows:(vh + 1) * vh_rows]
            sa = -(sk_h + pltpu.roll(sk_h, BH, axis=1))
            vt = v_ref[t, vs, :]
            y = jnp.zeros((vh_rows, LANES), F32)
            nxt = jnp.zeros((vh_rows, LANES), F32)
            for kp in range(K_HALF):
                s_new = (s_ref[kp, vs, :] * kw_ref[t, KW_DECAY, pl.ds(kp, 1), :]
                         + sa * kw_ref[t, KW_B, pl.ds(kp, 1), :]
                         + vt * kw_ref[t, KW_K, pl.ds(kp, 1), :])
                s_ref[kp, vs, :] = s_new
                y = y + s_new * kw_ref[t, KW_R, pl.ds(kp, 1), :]
                nxt = nxt + s_new * kw_ref[t_next, KW_KK, pl.ds(kp, 1), :]
            y_ref[t, vs, :] = y + pltpu.roll(y, BH, axis=1)
            sk_next.append(nxt)
        return jnp.concatenate(sk_next, axis=0)

    lax.fori_loop(0, SEQ_TB, step, sk)

    @pl.when(pl.program_id(0) == pl.num_programs(0) - 1)
    def _():
        sout_ref[...] = s_ref[...]


def _rwkv_seq(kw_t, v_t):
    return pl.pallas_call(
        _rwkv_seq_kernel,
        grid=(SEQ // SEQ_TB,),
        in_specs=[
            pl.BlockSpec((SEQ_TB, 5, K_HALF, LANES), lambda i: (i, 0, 0, 0)),
            pl.BlockSpec((SEQ_TB, HEAD_DIM, BH), lambda i: (i, 0, 0)),
        ],
        out_specs=[
            pl.BlockSpec((SEQ_TB, HEAD_DIM, LANES), lambda i: (i, 0, 0)),
            pl.BlockSpec((K_HALF, HEAD_DIM, LANES), lambda i: (0, 0, 0)),
        ],
        out_shape=[
            jax.ShapeDtypeStruct((SEQ, HEAD_DIM, LANES), F32),
            jax.ShapeDtypeStruct((K_HALF, HEAD_DIM, LANES), F32),
        ],
        scratch_shapes=[pltpu.VMEM((K_HALF, HEAD_DIM, LANES), F32), pltpu.VMEM((SEQ_TB, HEAD_DIM, LANES), F32)],
        compiler_params=_params("arbitrary"),
        name="rwkv_seq",
    )(kw_t, v_t)


def _rwkv_step_kernel(kw_ref, v_ref, s_ref, y_ref, sout_ref):
    def body(vi, carry):
        s = s_ref[0, vi]
        sa = -jnp.sum(s * kw_ref[KW_KK], axis=0, keepdims=True)
        s_new = s * kw_ref[KW_DECAY] + sa * kw_ref[KW_B] + v_ref[pl.ds(vi, 1), :] * kw_ref[KW_K]
        sout_ref[0, vi] = s_new
        y_ref[pl.ds(vi, 1), :] = jnp.sum(s_new * kw_ref[KW_R], axis=0, keepdims=True)
        return carry

    lax.fori_loop(0, HEAD_DIM, body, 0)


def _rwkv_step(kw_t, v_t, state_t):
    return pl.pallas_call(
        _rwkv_step_kernel,
        grid=(RWKV_HEADS,),
        in_specs=[
            pl.BlockSpec((5, HEAD_DIM, DEC_BATCH), lambda h: (0, h, 0)),
            pl.BlockSpec((HEAD_DIM, DEC_BATCH), lambda h: (h, 0)),
            pl.BlockSpec((1, HEAD_DIM, HEAD_DIM, DEC_BATCH), lambda h: (h, 0, 0, 0)),
        ],
        out_specs=[
            pl.BlockSpec((HEAD_DIM, DEC_BATCH), lambda h: (h, 0)),
            pl.BlockSpec((1, HEAD_DIM, HEAD_DIM, DEC_BATCH), lambda h: (h, 0, 0, 0)),
        ],
        out_shape=[
            jax.ShapeDtypeStruct((D_RWKV, DEC_BATCH), F32),
            jax.ShapeDtypeStruct((RWKV_HEADS, HEAD_DIM, HEAD_DIM, DEC_BATCH), F32),
        ],
        compiler_params=_params("parallel"),
        name="rwkv_step",
    )(kw_t, v_t, state_t)


ATT_PREP_TM = 640


def _swap_halves(x):
    lane = lax.broadcasted_iota(jnp.int32, x.shape, 1)
    first = (lane % HEAD_DIM) < HEAD_DIM // 2
    return jnp.where(first, pltpu.roll(x, LANES - HEAD_DIM // 2, axis=1), pltpu.roll(x, HEAD_DIM // 2, axis=1))


PAIRS = D_GROUP // LANES
QKV_TILES = 3 * PAIRS


def _att_prep_kernel(p_ref, cos_ref, sin_ref, qg_ref, kg_ref, e_ref, o0_ref, o1_ref, o2_ref):
    cos = cos_ref[...]
    sin = sin_ref[...]
    e = e_ref[...]
    o_refs = (o0_ref, o1_ref, o2_ref)
    for part, g_ref in ((0, qg_ref), (1, kg_ref)):
        x = p_ref[:, part * D_ATT:(part + 1) * D_ATT]
        ms = _segsum(x * x, e) * (1.0 / HEAD_DIM)
        xn = x * lax.rsqrt(ms + NORM_EPS) * g_ref[...]
        for c in range(D_ATT // LANES):
            xc = xn[:, c * LANES:(c + 1) * LANES]
            o_refs[c // PAIRS][part * PAIRS + c % PAIRS] = xc * cos + _swap_halves(xc) * sin
    for c in range(D_ATT // LANES):
        o_refs[c // PAIRS][2 * PAIRS + c % PAIRS] = p_ref[:, 2 * D_ATT + c * LANES:2 * D_ATT + (c + 1) * LANES]


def _att_prep(p_att, cos, sin, q_gain, k_gain, e_att):
    out = jax.ShapeDtypeStruct((QKV_TILES, M_ALL, LANES), F32)
    return pl.pallas_call(
        _att_prep_kernel,
        grid=(M_ALL // ATT_PREP_TM,),
        in_specs=[
            pl.BlockSpec((ATT_PREP_TM, 3 * D_ATT), lambda i: (i, 0)),
            pl.BlockSpec((ATT_PREP_TM, LANES), lambda i: (i, 0)),
            pl.BlockSpec((ATT_PREP_TM, LANES), lambda i: (i, 0)),
            pl.BlockSpec((1, D_ATT), lambda i: (0, 0)),
            pl.BlockSpec((1, D_ATT), lambda i: (0, 0)),
            _const_spec(e_att.shape),
        ],
        out_specs=[pl.BlockSpec((QKV_TILES, ATT_PREP_TM, LANES), lambda i: (0, i, 0))] * 3,
        out_shape=[out] * 3,
        compiler_params=_params("parallel"),
        name="att_prep",
    )(p_att, cos, sin, q_gain, k_gain, e_att)


ATT_TQ = 128


def _att_prompt_kernel(qkv_ref, o_ref, *, dil):
    scale = HEAD_DIM ** -0.5
    ls = SEQ // dil

    def heads(part, rows):
        pairs = [qkv_ref[part * PAIRS + pr, rows, :] for pr in range(PAIRS)]
        return jnp.stack([x[:, h * HEAD_DIM:(h + 1) * HEAD_DIM] for x in pairs for h in range(LANES // HEAD_DIM)],
                         axis=0)

    for c in range(dil):
        for qi in range(ls // ATT_TQ):
            rows = pl.ds(c + dil * qi * ATT_TQ, ATT_TQ, stride=dil)
            q = heads(0, rows).astype(BF16)
            k = heads(1, rows)
            v = heads(2, rows)
            q_off = 0
            if qi > 0:
                prev = pl.ds(c + dil * (qi - 1) * ATT_TQ, ATT_TQ, stride=dil)
                k = jnp.concatenate([heads(1, prev), k], axis=1)
                v = jnp.concatenate([heads(2, prev), v], axis=1)
                q_off = ATT_TQ
            s = jnp.einsum("hqd,hkd->hqk", q, k.astype(BF16), preferred_element_type=F32) * scale
            dist = (lax.broadcasted_iota(jnp.int32, s.shape, 1) + q_off) - lax.broadcasted_iota(jnp.int32, s.shape, 2)
            s = jnp.where((dist >= 0) & (dist <= N_KEYS_PAST), s, NEG_INF)
            m = jnp.max(s, axis=-1, keepdims=True)
            p = jnp.exp(s - m)
            l = jnp.sum(p, axis=-1, keepdims=True)
            lse = jnp.broadcast_to(m + jnp.log(l), (HEADS_PER_GROUP, ATT_TQ, HEAD_DIM))
            o = jnp.einsum("hqk,hkd->hqd", (p / l).astype(BF16), v.astype(BF16), preferred_element_type=F32)
            for h in range(HEADS_PER_GROUP):
                o_ref[h, rows, :] = jnp.concatenate([o[h], lse[h]], axis=1)


def _att_prompt(qkv, dil):
    return pl.pallas_call(
        functools.partial(_att_prompt_kernel, dil=dil),
        grid=(BATCH,),
        in_specs=[pl.BlockSpec((QKV_TILES, SEQ, LANES), lambda b: (0, b, 0))],
        out_specs=pl.BlockSpec((HEADS_PER_GROUP, SEQ, LANES), lambda b: (0, b, 0)),
        out_shape=jax.ShapeDtypeStruct((HEADS_PER_GROUP, M_PROMPT, LANES), F32),
        compiler_params=_params("parallel"),
        name=f"att_prompt_d{dil}",
    )(qkv)


CACHE_BATCH_PER_STEP = {128: 16, 512: 8, 2048: 2}


def _cache_attn_kernel(qkv_ref, c_ref, o_ref, cout_ref, *, dil, nb):
    n_tiles = c_ref.shape[-1] // LANES
    scale = HEAD_DIM ** -0.5
    r_i = lax.broadcasted_iota(jnp.int32, (HEAD_DIM, HEAD_DIM), 0)
    c_i = lax.broadcasted_iota(jnp.int32, (HEAD_DIM, HEAD_DIM), 1)
    eye = (r_i == c_i).astype(F32)
    on_grid = (lax.broadcasted_iota(jnp.int32, (1, LANES), 1) % dil) == 0
    last_lane = lax.broadcasted_iota(jnp.int32, (HEAD_DIM, LANES), 1) == LANES - 1

    def to_col(rows):
        return jnp.sum(eye * rows, axis=2, keepdims=True)

    rows = qkv_ref[...]
    parts = []
    for h in range(HEADS_PER_GROUP):
        q = rows[:, :, h * HEAD_DIM:(h + 1) * HEAD_DIM]
        k_new = rows[:, :, D_GROUP + h * HEAD_DIM:D_GROUP + (h + 1) * HEAD_DIM]
        v_new = rows[:, :, 2 * D_GROUP + h * HEAD_DIM:2 * D_GROUP + (h + 1) * HEAD_DIM]
        q_col, k_col, v_col = to_col(q), to_col(k_new), to_col(v_new)
        s_tiles = [None] * n_tiles
        nxt = jnp.broadcast_to(k_col, (nb, HEAD_DIM, LANES))
        for j in reversed(range(n_tiles)):
            ts = slice(j * LANES, (j + 1) * LANES)
            kt = c_ref[:, 0, h, :, ts]
            s = jnp.sum(kt * q_col, axis=1, keepdims=True) * scale
            s_tiles[j] = jnp.where(on_grid, s, NEG_INF)
            rolled = pltpu.roll(kt, LANES - 1, axis=2)
            cout_ref[:, 0, h, :, ts] = jnp.where(last_lane, nxt, rolled)
            nxt = rolled
        s_new = jnp.sum(k_new * q, axis=2, keepdims=True) * scale
        m_row = s_tiles[0]
        for j in range(1, n_tiles):
            m_row = jnp.maximum(m_row, s_tiles[j])
        m = jnp.maximum(jnp.max(m_row, axis=2, keepdims=True), s_new)
        p_new = jnp.exp(s_new - m)
        l_row = jnp.zeros((nb, 1, LANES), F32)
        acc = jnp.zeros((nb, HEAD_DIM, LANES), F32)
        nxt = jnp.broadcast_to(v_col, (nb, HEAD_DIM, LANES))
        for j in reversed(range(n_tiles)):
            ts = slice(j * LANES, (j + 1) * LANES)
            p = jnp.exp(s_tiles[j] - m)
            l_row = l_row + p
            vt = c_ref[:, 1, h, :, ts]
            acc = acc + vt * p
            rolled = pltpu.roll(vt, LANES - 1, axis=2)
            cout_ref[:, 1, h, :, ts] = jnp.where(last_lane, nxt, rolled)
            nxt = rolled
        l = jnp.sum(l_row, axis=2, keepdims=True) + p_new
        o_col = (jnp.sum(acc, axis=2, keepdims=True) + p_new * v_col) / l
        parts += [jnp.sum(eye * o_col, axis=1, keepdims=True), jnp.broadcast_to(m + jnp.log(l), (nb, 1, HEAD_DIM))]
    o_ref[...] = jnp.concatenate(parts, axis=2)


def _cache_attn(qkv, cache_t, dil):
    window = cache_t.shape[-1]
    nb = CACHE_BATCH_PER_STEP[window]
    blk = (nb, 2, HEADS_PER_GROUP, HEAD_DIM, window)
    return pl.pallas_call(
        functools.partial(_cache_attn_kernel, dil=dil, nb=nb),
        grid=(DEC_BATCH // nb,),
        in_specs=[pl.BlockSpec((nb, 1, 3 * D_GROUP), lambda i: (i, 0, 0)),
                  pl.BlockSpec(blk, lambda i: (i, 0, 0, 0, 0))],
        out_specs=[pl.BlockSpec((nb, 1, HEADS_PER_GROUP * LANES), lambda i: (i, 0, 0)),
                   pl.BlockSpec(blk, lambda i: (i, 0, 0, 0, 0))],
        out_shape=[jax.ShapeDtypeStruct((DEC_BATCH, 1, HEADS_PER_GROUP * LANES), F32),
                   jax.ShapeDtypeStruct(cache_t.shape, F32)],
        compiler_params=_params("parallel"),
        name=f"cache_attn_w{window}",
    )(qkv, cache_t)


MERGE_TM = 208


def _merge_kernel(x_ref, y_ref, bonus_ref, g_ref, a0_ref, a1_ref, a2_ref, pg_ref, bg_ref,
                  lnw_ref, lnb_ref, er_ref, eb_ref, wbr_ref, wba0_ref, wba1_ref, wba2_ref, wo_ref, o_ref):
    e = (er_ref[...], eb_ref[...])
    inv_n = 1.0 / HEAD_DIM
    y = y_ref[...]
    d = y - _segsum(y, e) * inv_n
    var = _segsum(d * d, e) * inv_n
    yn = d * lax.rsqrt(var + RWKV_LN_EPS) * lnw_ref[...] + lnb_ref[...]
    o_a = ((yn + bonus_ref[...]) * g_ref[...]).astype(BF16)
    br_a = jnp.dot(o_a, wbr_ref[...], preferred_element_type=F32)
    a = (a0_ref[...], a1_ref[...], a2_ref[...])
    m = jnp.maximum(jnp.maximum(a[0], a[1]), a[2])
    ex = [jnp.exp(t - m) for t in a]
    inv = 1.0 / (ex[0] + ex[1] + ex[2])
    br_b = jnp.zeros_like(br_a)
    for grp, wb_ref in enumerate((wba0_ref, wba1_ref, wba2_ref)):
        alpha = ex[grp] * inv
        parts = [a[grp][h] * pltpu.roll(alpha[h], HEAD_DIM, axis=1) for h in range(HEADS_PER_GROUP)]
        xg = jnp.concatenate(parts, axis=1).astype(BF16)
        br_b = br_b + jnp.dot(xg, wb_ref[...], preferred_element_type=F32)
    gates = jax.nn.sigmoid(pg_ref[...] + bg_ref[...])
    merged = (gates[:, 0:D_MODEL] * br_a + gates[:, D_MODEL:] * br_b).astype(BF16)
    o_ref[...] = x_ref[...] + jnp.dot(merged, wo_ref[...], preferred_element_type=F32)


def _merge(x, y, bonus, g, att, p_gate, b_gate, ln_w, ln_b, e_heads, w_br, w_ba, w_o):
    tile = lambda n: pl.BlockSpec((MERGE_TM, n), lambda i: (i, 0))
    row = lambda n: pl.BlockSpec((1, n), lambda i: (0, 0))
    att_tile = pl.BlockSpec((HEADS_PER_GROUP, MERGE_TM, LANES), lambda i: (0, i, 0))
    return pl.pallas_call(
        _merge_kernel,
        grid=(M_ALL // MERGE_TM,),
        in_specs=[tile(D_MODEL), tile(D_RWKV), tile(D_RWKV), tile(D_RWKV),
                  att_tile, att_tile, att_tile, tile(2 * D_MODEL), row(2 * D_MODEL),
                  row(D_RWKV), row(D_RWKV), _const_spec(e_heads[0].shape), _const_spec(e_heads[1].shape),
                  _const_spec(w_br.shape),
                  _const_spec(w_ba[0].shape), _const_spec(w_ba[1].shape), _const_spec(w_ba[2].shape),
                  _const_spec(w_o.shape)],
        out_specs=tile(D_MODEL),
        out_shape=jax.ShapeDtypeStruct((M_ALL, D_MODEL), F32),
        compiler_params=_params("parallel"),
        name="merge",
    )(x, y, bonus, g, att[0], att[1], att[2], p_gate, b_gate, ln_w, ln_b, *e_heads, w_br,
      w_ba[0], w_ba[1], w_ba[2], w_o)


def _split_bf16(w):
    hi = w.astype(BF16)
    return hi, (w - hi.astype(F32)).astype(BF16)


def _head_ones(n_heads):
    reduce = (jnp.arange(n_heads * HEAD_DIM)[:, None] // HEAD_DIM == jnp.arange(LANES)[None, :]).astype(BF16)
    return reduce, reduce.T


def _rotary_tables():
    half = HEAD_DIM // 2
    inv_freq = ROPE_THETA ** (-jnp.arange(half, dtype=F32) / half)
    pos = jnp.concatenate([jnp.tile(jnp.arange(SEQ), BATCH), jnp.full((DEC_BATCH,), PAST_LEN)]).astype(F32)
    ang = pos[:, None] * inv_freq[None, :]
    cos, sin = jnp.cos(ang), jnp.sin(ang)
    cos = jnp.concatenate([cos, cos] * (LANES // HEAD_DIM), axis=1)
    sin = jnp.concatenate([-sin, sin] * (LANES // HEAD_DIM), axis=1)
    return cos, sin


def kernel(x_prompt, x_sample, state_rwkv_shift, state_rwkv_wkv, cache_att_w128, cache_att_w512, cache_att_w2048,
           ffn1_norm, ffn1_w_in, ffn1_w_out, mix_norm, w_in, b_gate, rwkv_mu, rwkv_w0, rwkv_w2, rwkv_a0, rwkv_a2,
           rwkv_g2, rwkv_k_k, rwkv_k_a, rwkv_r_k, rwkv_ln_w, rwkv_ln_b, w_branch_rwkv, attn_q_norm, attn_k_norm,
           w_branch_attn, w_out, ffn2_norm, ffn2_w_in, ffn2_w_out):
    caches = (cache_att_w128, cache_att_w512, cache_att_w2048)

    w_mix = w_in[0]
    pad = RWKV_PROJ_PAD - D_RWKV_PROJ
    w_rwkv = jnp.pad(w_mix[:, :D_RWKV_PROJ], ((0, 0), (0, pad))).astype(BF16)
    w_att = w_mix[:, D_RWKV_PROJ:D_RWKV_PROJ + 3 * D_ATT].astype(BF16)
    w_gate = w_mix[:, D_RWKV_PROJ + 3 * D_ATT:].astype(BF16)
    mu = jnp.pad(rwkv_mu, ((0, 0), (0, pad)))
    state_shift = jnp.pad(state_rwkv_shift[0], ((0, 0), (0, pad)))
    zeros = jnp.zeros((D_DECAY_LORA, D_RWKV), F32)
    w_lora = jnp.concatenate([jnp.concatenate([rwkv_w2[0], zeros], axis=1),
                              jnp.concatenate([zeros, rwkv_a2[0]], axis=1)], axis=0)
    w_glora = jnp.pad(rwkv_g2[0], ((0, 2 * LANES - D_GATE_LORA), (0, 0)))
    wl_hi, wl_lo = _split_bf16(w_lora)
    wg_hi, wg_lo = _split_bf16(w_glora)
    e_heads = _head_ones(RWKV_HEADS)
    e_att = jnp.kron(jnp.eye(ATT_HEADS, dtype=F32), jnp.ones((HEAD_DIM, HEAD_DIM), F32)).astype(BF16)
    cos, sin = _rotary_tables()
    q_gain = jnp.tile(attn_q_norm, (1, ATT_HEADS))
    k_gain = jnp.tile(attn_k_norm, (1, ATT_HEADS))
    wba = w_branch_attn[0].reshape(len(ATT_GROUPS), HEADS_PER_GROUP, HEAD_DIM, D_MODEL)
    wba = jnp.pad(wba, ((0, 0), (0, 0), (0, LANES - HEAD_DIM), (0, 0))).astype(BF16)
    wba = wba.reshape(len(ATT_GROUPS), HEADS_PER_GROUP * LANES, D_MODEL)

    x1, h_mix = _ffn(x_prompt.reshape(M_PROMPT, D_MODEL), ffn1_norm, ffn1_w_in[0].astype(BF16),
                     ffn1_w_out[0].astype(BF16), x_sample=x_sample.reshape(DEC_BATCH, D_MODEL), next_norm=mix_norm)

    p_rwkv = _proj(h_mix, w_rwkv, 512, "proj_rwkv")
    kw, v_r, kw_s_t, v_s_t, bonus, g_r = _rwkv_prep(p_rwkv, state_shift, mu, rwkv_w0, rwkv_a0, rwkv_k_k, rwkv_k_a,
                                                    rwkv_r_k.reshape(1, D_RWKV), wl_hi, wl_lo, wg_hi, wg_lo, e_heads)
    kw, v_r, h_att = lax.optimization_barrier((kw, v_r, h_mix))
    kw_t = kw.reshape(5, BATCH, SEQ, RWKV_HEADS, 2, K_HALF)
    kw_t = kw_t.transpose(2, 0, 5, 4, 1, 3).reshape(SEQ, 5, K_HALF, LANES)
    v_t = v_r.reshape(BATCH, SEQ, RWKV_HEADS, HEAD_DIM).transpose(1, 3, 0, 2).reshape(SEQ, HEAD_DIM, BH)

    p_att = _proj(h_att, w_att, 768, "proj_att")
    p_gate = _proj(h_att, w_gate, 512, "proj_gate")
    qkv = _att_prep(p_att, cos, sin, q_gain, k_gain, e_att)
    att, kv_p, kv_s = [], [], []
    for g, (window, dil) in enumerate(ATT_GROUPS):
        att_p = _att_prompt(qkv[g], dil)
        keep = min(window, SEQ)
        kv = qkv[g][PAIRS:, :M_PROMPT].reshape(2, PAIRS, BATCH, SEQ, LANES // HEAD_DIM, HEAD_DIM)
        kv = kv.transpose(2, 3, 0, 1, 4, 5).reshape(BATCH, SEQ, 2, HEADS_PER_GROUP, HEAD_DIM)
        kv_p.append(kv[:, SEQ - keep:][None])
        qkv_s = qkv[g][:, M_PROMPT:].transpose(1, 0, 2).reshape(DEC_BATCH, 1, 3 * D_GROUP)
        o_s, cache_new = _cache_attn(qkv_s, caches[g][0].transpose(0, 2, 3, 4, 1), dil)
        att_s = o_s.reshape(DEC_BATCH, HEADS_PER_GROUP, LANES).transpose(1, 0, 2)
        att.append(jnp.concatenate([att_p, att_s], axis=1))
        kv_s.append(cache_new.transpose(0, 4, 1, 2, 3)[None])

    kw_t, v_t, att = lax.optimization_barrier((kw_t, v_t, att))
    y_t, s_t = _rwkv_seq(kw_t, v_t)
    y_p = y_t[:, :, :BH].reshape(SEQ, HEAD_DIM, BATCH, RWKV_HEADS).transpose(2, 0, 3, 1).reshape(M_PROMPT, D_RWKV)
    wkv_p = s_t.reshape(K_HALF, HEAD_DIM, 2, BATCH, RWKV_HEADS).transpose(3, 4, 1, 2, 0)
    wkv_p = wkv_p.reshape(BATCH, RWKV_HEADS, HEAD_DIM, HEAD_DIM)
    y_s_t, wkv_s_t = _rwkv_step(kw_s_t, v_s_t, state_rwkv_wkv[0].transpose(1, 2, 3, 0))
    wkv_s = wkv_s_t.transpose(3, 0, 1, 2)
    y_r = jnp.concatenate([y_p, y_s_t.T], axis=0)

    x2 = _merge(x1, y_r, bonus, g_r, att, p_gate, b_gate, rwkv_ln_w, rwkv_ln_b,
                e_heads, w_branch_rwkv[0].astype(BF16), wba, w_out[0].astype(BF16))
    y_prompt, y_sample = _ffn(x2, ffn2_norm, ffn2_w_in[0].astype(BF16), ffn2_w_out[0].astype(BF16), split_out=True)

    shift_p = jnp.concatenate([p_rwkv[(b + 1) * SEQ - 1:(b + 1) * SEQ, :D_RWKV_PROJ] for b in range(BATCH)])[None]
    shift_s = p_rwkv[M_PROMPT:, :D_RWKV_PROJ][None]
    return (y_prompt.reshape(BATCH, SEQ, D_MODEL), y_sample.reshape(DEC_BATCH, 1, D_MODEL),
            shift_p, wkv_p[None], kv_p[0], kv_p[1], kv_p[2],
            shift_s, wkv_s[None], kv_s[0], kv_s[1], kv_s[2])
```

```python
import functools

import jax
import jax.numpy as jnp
from jax import lax
from jax.experimental import pallas as pl
from jax.experimental.pallas import tpu as pltpu

D_MODEL = 2048
BATCH = 4
SEQ = 2048
DEC_BATCH = 128
PAST_LEN = 2048
M_PROMPT = BATCH * SEQ
M_ALL = M_PROMPT + DEC_BATCH

RWKV_HEADS = 16
HEAD_DIM = 64
D_RWKV = RWKV_HEADS * HEAD_DIM
D_DECAY_LORA = 64
D_ICLR_LORA = 64
D_GATE_LORA = 160
D_RWKV_PROJ = 3 * D_RWKV + D_DECAY_LORA + D_ICLR_LORA + D_GATE_LORA
RWKV_PROJ_PAD = 3584
RWKV_LN_EPS = 64e-5

ATT_GROUPS = ((128, 1), (512, 4), (2048, 16))
HEADS_PER_GROUP = 4
ATT_HEADS = HEADS_PER_GROUP * len(ATT_GROUPS)
D_ATT = ATT_HEADS * HEAD_DIM
D_GROUP = HEADS_PER_GROUP * HEAD_DIM
N_KEYS_PAST = 128
ROPE_THETA = 10000.0
D_FF = 5632
NORM_EPS = 1e-6
NEG_INF = -1e30

LANES = 128
VMEM_LIMIT = 56 * 1024 * 1024

F32 = jnp.float32
BF16 = jnp.bfloat16


def _params(*sem):
    return pltpu.CompilerParams(dimension_semantics=sem, vmem_limit_bytes=VMEM_LIMIT)


def _const_spec(shape):
    nd = len(shape)
    return pl.BlockSpec(shape, lambda *_: (0,) * nd, pipeline_mode=pl.Buffered(1))


def _rms_to_bf16(x, g):
    ms = jnp.mean(x * x, axis=-1, keepdims=True)
    return (x * lax.rsqrt(ms + NORM_EPS) * g).astype(BF16)


def _split_dot(x, w):
    hi = x.astype(BF16)
    lo = (x - hi.astype(F32)).astype(BF16)
    return jnp.dot(hi, w, preferred_element_type=F32) + jnp.dot(lo, w, preferred_element_type=F32)


def _segsum(x, e):
    if isinstance(e, tuple):
        return _split_dot(_split_dot(x, e[0]), e[1])
    return _split_dot(x, e)


def _dot_split(x, w_hi, w_lo):
    hi = x.astype(BF16)
    lo = (x - hi.astype(F32)).astype(BF16)
    return (jnp.dot(hi, w_hi, preferred_element_type=F32) + jnp.dot(lo, w_hi, preferred_element_type=F32)
            + jnp.dot(hi, w_lo, preferred_element_type=F32))


FFN_TM = 640
FFN_TF = 512


FFN_SAMPLE_ROW0 = M_PROMPT % FFN_TM
assert FFN_TM - FFN_SAMPLE_ROW0 == DEC_BATCH


def _ffn_kernel(*refs, split_in, split_out):
    refs = list(refs)
    x_ref = refs.pop(0)
    xs_ref = refs.pop(0) if split_in else None
    g_ref, wg_ref, wu_ref, wo_ref = refs[:4]
    refs = refs[4:]
    gn_ref = refs.pop(0) if split_in else None
    o_ref = refs.pop(0)
    os_ref = refs.pop(0) if split_out else None
    hn_ref = refs.pop(0) if split_in else None
    h_ref = refs.pop(0)
    xin_ref = refs.pop(0) if split_in else x_ref
    j = pl.program_id(1)
    last_tile = pl.program_id(0) == pl.num_programs(0) - 1

    @pl.when(j == 0)
    def _():
        if split_in:
            @pl.when(jnp.logical_not(last_tile))
            def _():
                xin_ref[...] = x_ref[...]

            @pl.when(last_tile)
            def _():
                xin_ref[0:FFN_SAMPLE_ROW0] = x_ref[0:FFN_SAMPLE_ROW0]
                xin_ref[FFN_SAMPLE_ROW0:] = xs_ref[...]

        h_ref[...] = _rms_to_bf16(xin_ref[...], g_ref[...])
        o_ref[...] = jnp.zeros_like(o_ref)

    h = h_ref[...]
    gate = jnp.dot(h, wg_ref[...], preferred_element_type=F32)
    up = jnp.dot(h, wu_ref[...], preferred_element_type=F32)
    act = (gate * jax.nn.sigmoid(gate) * up).astype(BF16)
    o_ref[...] += jnp.dot(act, wo_ref[...], preferred_element_type=F32)

    @pl.when(j == pl.num_programs(1) - 1)
    def _():
        out = xin_ref[...] + 0.5 * o_ref[...]
        o_ref[...] = out
        if split_in:
            hn_ref[...] = _rms_to_bf16(out, gn_ref[...])
        if split_out:
            @pl.when(last_tile)
            def _():
                os_ref[...] = out[FFN_SAMPLE_ROW0:]


def _ffn(x, g, w_in, w_out, x_sample=None, next_norm=None, split_out=False):
    nj = D_FF // FFN_TF
    split_in = x_sample is not None
    row = pl.BlockSpec((1, D_MODEL), lambda i, j: (0, 0))
    tile = pl.BlockSpec((FFN_TM, D_MODEL), lambda i, j: (i, 0))
    sample = pl.BlockSpec((DEC_BATCH, D_MODEL), lambda i, j: (0, 0))
    weights = [pl.BlockSpec((D_MODEL, FFN_TF), lambda i, j: (0, j)),
               pl.BlockSpec((D_MODEL, FFN_TF), lambda i, j: (0, j + nj)),
               pl.BlockSpec((FFN_TF, D_MODEL), lambda i, j: (j, 0))]
    in_specs = [tile] + ([sample] if split_in else []) + [row] + weights + ([row] if split_in else [])
    args = [x] + ([x_sample] if split_in else []) + [g, w_in, w_in, w_out] + ([next_norm] if split_in else [])
    scratch = [pltpu.VMEM((FFN_TM, D_MODEL), BF16)] + ([pltpu.VMEM((FFN_TM, D_MODEL), F32)] if split_in else [])
    if split_out:
        out_specs = [tile, sample]
        out_shape = [jax.ShapeDtypeStruct((M_PROMPT, D_MODEL), F32), jax.ShapeDtypeStruct((DEC_BATCH, D_MODEL), F32)]
    else:
        out_specs = [tile, tile]
        out_shape = [jax.ShapeDtypeStruct((M_ALL, D_MODEL), F32), jax.ShapeDtypeStruct((M_ALL, D_MODEL), BF16)]
    return pl.pallas_call(
        functools.partial(_ffn_kernel, split_in=split_in, split_out=split_out),
        grid=(M_ALL // FFN_TM, nj),
        in_specs=in_specs,
        out_specs=out_specs,
        out_shape=out_shape,
        scratch_shapes=scratch,
        compiler_params=_params("arbitrary" if split_out else "parallel", "arbitrary"),
        name="ffn",
    )(*args)


PROJ_TM = 1664


def _proj_kernel(h_ref, w_ref, o_ref):
    o_ref[...] = jnp.dot(h_ref[...], w_ref[...], preferred_element_type=F32)


def _proj(h, w, tn, name):
    m, n = h.shape[0], w.shape[1]
    return pl.pallas_call(
        _proj_kernel,
        grid=(m // PROJ_TM, n // tn),
        in_specs=[
            pl.BlockSpec((PROJ_TM, D_MODEL), lambda i, j: (i, 0)),
            pl.BlockSpec((D_MODEL, tn), lambda i, j: (0, j)),
        ],
        out_specs=pl.BlockSpec((PROJ_TM, tn), lambda i, j: (i, j)),
        out_shape=jax.ShapeDtypeStruct((m, n), F32),
        compiler_params=_params("parallel", "arbitrary"),
        name=name,
    )(h, w)


PREP_TM = 128
PREP_TILES_PER_SEQ = SEQ // PREP_TM
PREP_SAMPLE_TILE = M_PROMPT // PREP_TM
KW_R, KW_DECAY, KW_K, KW_KK, KW_B = range(5)


def _prep_tile(i):
    return jnp.where(i == 0, PREP_SAMPLE_TILE, i - 1)


def _rwkv_prep_kernel(p_ref, tail_ref, state_ref, mu_ref, w0_ref, a0_ref, kk_ref, ka_ref, rk_ref,
                      wl_hi_ref, wl_lo_ref, wg_hi_ref, wg_lo_ref, er_ref, eb_ref,
                      kw_ref, v_ref, kws_ref, vs_ref, bonus_ref, g_ref):
    i = pl.program_id(0)
    is_sample = i == 0
    p = p_ref[...]
    tail = jnp.where((i - 1) % PREP_TILES_PER_SEQ == 0, 0.0, tail_ref[7:8, :])
    row = lax.broadcasted_iota(jnp.int32, p.shape, 0)
    prev = jnp.where(row == 0, tail, pltpu.roll(p, 1, axis=0))
    prev = jnp.where(is_sample, state_ref[...], prev)
    pm = p + (prev - p) * mu_ref[...]

    r = pm[:, 0:D_RWKV]
    k = pm[:, D_RWKV:2 * D_RWKV]
    v = pm[:, 2 * D_RWKV:3 * D_RWKV]
    lora = pm[:, 3 * D_RWKV:]
    lane = lax.broadcasted_iota(jnp.int32, (p.shape[0], LANES), 1)
    x_wa = lora[:, 0:LANES]
    x_wa = jnp.where(lane < D_DECAY_LORA, jnp.tanh(x_wa), x_wa)
    x_g = jax.nn.sigmoid(lora[:, LANES:3 * LANES])
    wa = _dot_split(x_wa, wl_hi_ref[...], wl_lo_ref[...])
    g = _dot_split(x_g, wg_hi_ref[...], wg_lo_ref[...])

    z = -(w0_ref[...] + wa[:, 0:D_RWKV])
    softplus = jnp.maximum(z, 0.0) + jnp.log(1.0 + jnp.exp(-jnp.abs(z)))
    decay = jnp.exp(-jnp.exp(-softplus - 0.5))
    a = jax.nn.sigmoid(a0_ref[...] + wa[:, D_RWKV:])
    e = (er_ref[...], eb_ref[...])
    kk = k * kk_ref[...]
    kk = kk * lax.rsqrt(jnp.maximum(_segsum(kk * kk, e), 1e-24))
    k_hat = k * (1.0 + (a - 1.0) * ka_ref[...])
    kw = (r, decay, k_hat, kk, kk * a)
    bonus_ref[...] = _segsum(r * k_hat * rk_ref[...], e) * v
    g_ref[...] = g

    @pl.when(i > 0)
    def _():
        for idx in range(5):
            kw_ref[idx] = kw[idx]
        v_ref[...] = v

    @pl.when(is_sample)
    def _():
        for idx in range(5):
            kws_ref[idx] = kw[idx].T
        vs_ref[...] = v.T


def _rwkv_prep(p_rwkv, state_shift, mu, w0, a0, k_k, k_a, r_k, wl_hi, wl_lo, wg_hi, wg_lo, e_heads):
    n_tiles = M_ALL // PREP_TM
    row = lambda n: pl.BlockSpec((1, n), lambda i: (0, 0))
    prompt_tile = lambda i: jnp.maximum(i - 1, 0)
    return pl.pallas_call(
        _rwkv_prep_kernel,
        grid=(n_tiles,),
        in_specs=[
            pl.BlockSpec((PREP_TM, RWKV_PROJ_PAD), lambda i: (_prep_tile(i), 0)),
            pl.BlockSpec((8, RWKV_PROJ_PAD), lambda i: (jnp.maximum((i - 1) * (PREP_TM // 8) - 1, 0), 0)),
            pl.BlockSpec((DEC_BATCH, RWKV_PROJ_PAD), lambda i: (0, 0)),
            row(RWKV_PROJ_PAD), row(D_RWKV), row(D_RWKV), row(D_RWKV), row(D_RWKV), row(D_RWKV),
            _const_spec(wl_hi.shape), _const_spec(wl_lo.shape),
            _const_spec(wg_hi.shape), _const_spec(wg_lo.shape),
            _const_spec(e_heads[0].shape), _const_spec(e_heads[1].shape),
        ],
        out_specs=[
            pl.BlockSpec((5, PREP_TM, D_RWKV), lambda i: (0, prompt_tile(i), 0)),
            pl.BlockSpec((PREP_TM, D_RWKV), lambda i: (prompt_tile(i), 0)),
            pl.BlockSpec((5, D_RWKV, DEC_BATCH), lambda i: (0, 0, 0)),
            pl.BlockSpec((D_RWKV, DEC_BATCH), lambda i: (0, 0)),
            pl.BlockSpec((PREP_TM, D_RWKV), lambda i: (_prep_tile(i), 0)),
            pl.BlockSpec((PREP_TM, D_RWKV), lambda i: (_prep_tile(i), 0)),
        ],
        out_shape=[
            jax.ShapeDtypeStruct((5, M_PROMPT, D_RWKV), F32),
            jax.ShapeDtypeStruct((M_PROMPT, D_RWKV), F32),
            jax.ShapeDtypeStruct((5, D_RWKV, DEC_BATCH), F32),
            jax.ShapeDtypeStruct((D_RWKV, DEC_BATCH), F32),
            jax.ShapeDtypeStruct((M_ALL, D_RWKV), F32),
            jax.ShapeDtypeStruct((M_ALL, D_RWKV), F32),
        ],
        compiler_params=_params("arbitrary"),
        name="rwkv_prep",
    )(p_rwkv, p_rwkv, state_shift, mu, w0, a0, k_k, k_a, r_k, wl_hi, wl_lo, wg_hi, wg_lo, *e_heads)


SEQ_TB = 64
K_HALF = HEAD_DIM // 2
BH = BATCH * RWKV_HEADS


def _rwkv_seq_kernel(kw_ref, vin_ref, y_ref, sout_ref, s_ref, v_ref):
    @pl.when(pl.program_id(0) == 0)
    def _():
        s_ref[...] = jnp.zeros_like(s_ref)

    vin = vin_ref[...]
    v_ref[...] = jnp.concatenate([vin, vin], axis=-1)

    sk = jnp.zeros((HEAD_DIM, LANES), F32)
    for kp in range(K_HALF):
        sk = sk + s_ref[kp] * kw_ref[0, KW_KK, pl.ds(kp, 1), :]

    n_vh = 4
    vh_rows = HEAD_DIM // n_vh

    def step(t, sk):
        t_next = jnp.minimum(t + 1, SEQ_TB - 1)
        sk_next = []
        for vh in range(n_vh):
            vs = pl.ds(vh * vh_rows, vh_rows)
            sk_h = sk[vh * vh_rows:(vh + 1) * vh_rows]
            sa = -(sk_h + pltpu.roll(sk_h, BH, axis=1))
            vt = v_ref[t, vs, :]
            y = jnp.zeros((vh_rows, LANES), F32)
            nxt = jnp.zeros((vh_rows, LANES), F32)
            for kp in range(K_HALF):
                s_new = (s_ref[kp, vs, :] * kw_ref[t, KW_DECAY, pl.ds(kp, 1), :]
                         + sa * kw_ref[t, KW_B, pl.ds(kp, 1), :]
                         + vt * kw_ref[t, KW_K, pl.ds(kp, 1), :])
                s_ref[kp, vs, :] = s_new
                y = y + s_new * kw_ref[t, KW_R, pl.ds(kp, 1), :]
                nxt = nxt + s_new * kw_ref[t_next, KW_KK, pl.ds(kp, 1), :]
            y_ref[t, vs, :] = y + pltpu.roll(y, BH, axis=1)
            sk_next.append(nxt)
        return jnp.concatenate(sk_next, axis=0)

    lax.fori_loop(0, SEQ_TB, step, sk)

    @pl.when(pl.program_id(0) == pl.num_programs(0) - 1)
    def _():
        sout_ref[...] = s_ref[...]


def _rwkv_seq(kw_t, v_t):
    return pl.pallas_call(
        _rwkv_seq_kernel,
        grid=(SEQ // SEQ_TB,),
        in_specs=[
            pl.BlockSpec((SEQ_TB, 5, K_HALF, LANES), lambda i: (i, 0, 0, 0)),
            pl.BlockSpec((SEQ_TB, HEAD_DIM, BH), lambda i: (i, 0, 0)),
        ],
        out_specs=[
            pl.BlockSpec((SEQ_TB, HEAD_DIM, LANES), lambda i: (i, 0, 0)),
            pl.BlockSpec((K_HALF, HEAD_DIM, LANES), lambda i: (0, 0, 0)),
        ],
        out_shape=[
            jax.ShapeDtypeStruct((SEQ, HEAD_DIM, LANES), F32),
            jax.ShapeDtypeStruct((K_HALF, HEAD_DIM, LANES), F32),
        ],
        scratch_shapes=[pltpu.VMEM((K_HALF, HEAD_DIM, LANES), F32), pltpu.VMEM((SEQ_TB, HEAD_DIM, LANES), F32)],
        compiler_params=_params("arbitrary"),
        name="rwkv_seq",
    )(kw_t, v_t)


def _rwkv_step_kernel(kw_ref, v_ref, s_ref, y_ref, sout_ref):
    def body(vi, carry):
        s = s_ref[0, vi]
        sa = -jnp.sum(s * kw_ref[KW_KK], axis=0, keepdims=True)
        s_new = s * kw_ref[KW_DECAY] + sa * kw_ref[KW_B] + v_ref[pl.ds(vi, 1), :] * kw_ref[KW_K]
        sout_ref[0, vi] = s_new
        y_ref[pl.ds(vi, 1), :] = jnp.sum(s_new * kw_ref[KW_R], axis=0, keepdims=True)
        return carry

    lax.fori_loop(0, HEAD_DIM, body, 0)


def _rwkv_step(kw_t, v_t, state_t):
    return pl.pallas_call(
        _rwkv_step_kernel,
        grid=(RWKV_HEADS,),
        in_specs=[
            pl.BlockSpec((5, HEAD_DIM, DEC_BATCH), lambda h: (0, h, 0)),
            pl.BlockSpec((HEAD_DIM, DEC_BATCH), lambda h: (h, 0)),
            pl.BlockSpec((1, HEAD_DIM, HEAD_DIM, DEC_BATCH), lambda h: (h, 0, 0, 0)),
        ],
        out_specs=[
            pl.BlockSpec((HEAD_DIM, DEC_BATCH), lambda h: (h, 0)),
            pl.BlockSpec((1, HEAD_DIM, HEAD_DIM, DEC_BATCH), lambda h: (h, 0, 0, 0)),
        ],
        out_shape=[
            jax.ShapeDtypeStruct((D_RWKV, DEC_BATCH), F32),
            jax.ShapeDtypeStruct((RWKV_HEADS, HEAD_DIM, HEAD_DIM, DEC_BATCH), F32),
        ],
        compiler_params=_params("parallel"),
        name="rwkv_step",
    )(kw_t, v_t, state_t)


ATT_PREP_TM = 640


def _swap_halves(x):
    lane = lax.broadcasted_iota(jnp.int32, x.shape, 1)
    first = (lane % HEAD_DIM) < HEAD_DIM // 2
    return jnp.where(first, pltpu.roll(x, LANES - HEAD_DIM // 2, axis=1), pltpu.roll(x, HEAD_DIM // 2, axis=1))


PAIRS = D_GROUP // LANES
QKV_TILES = 3 * PAIRS


def _att_prep_kernel(p_ref, cos_ref, sin_ref, qg_ref, kg_ref, e_ref, o0_ref, o1_ref, o2_ref):
    cos = cos_ref[...]
    sin = sin_ref[...]
    e = e_ref[...]
    o_refs = (o0_ref, o1_ref, o2_ref)
    for part, g_ref in ((0, qg_ref), (1, kg_ref)):
        x = p_ref[:, part * D_ATT:(part + 1) * D_ATT]
        ms = _segsum(x * x, e) * (1.0 / HEAD_DIM)
        xn = x * lax.rsqrt(ms + NORM_EPS) * g_ref[...]
        for c in range(D_ATT // LANES):
            xc = xn[:, c * LANES:(c + 1) * LANES]
            o_refs[c // PAIRS][part * PAIRS + c % PAIRS] = xc * cos + _swap_halves(xc) * sin
    for c in range(D_ATT // LANES):
        o_refs[c // PAIRS][2 * PAIRS + c % PAIRS] = p_ref[:, 2 * D_ATT + c * LANES:2 * D_ATT + (c + 1) * LANES]


def _att_prep(p_att, cos, sin, q_gain, k_gain, e_att):
    out = jax.ShapeDtypeStruct((QKV_TILES, M_ALL, LANES), F32)
    return pl.pallas_call(
        _att_prep_kernel,
        grid=(M_ALL // ATT_PREP_TM,),
        in_specs=[
            pl.BlockSpec((ATT_PREP_TM, 3 * D_ATT), lambda i: (i, 0)),
            pl.BlockSpec((ATT_PREP_TM, LANES), lambda i: (i, 0)),
            pl.BlockSpec((ATT_PREP_TM, LANES), lambda i: (i, 0)),
            pl.BlockSpec((1, D_ATT), lambda i: (0, 0)),
            pl.BlockSpec((1, D_ATT), lambda i: (0, 0)),
            _const_spec(e_att.shape),
        ],
        out_specs=[pl.BlockSpec((QKV_TILES, ATT_PREP_TM, LANES), lambda i: (0, i, 0))] * 3,
        out_shape=[out] * 3,
        compiler_params=_params("parallel"),
        name="att_prep",
    )(p_att, cos, sin, q_gain, k_gain, e_att)


ATT_TQ = 128


def _att_prompt_kernel(qkv_ref, o_ref, *, dil):
    scale = HEAD_DIM ** -0.5
    ls = SEQ // dil

    def heads(part, rows):
        pairs = [qkv_ref[part * PAIRS + pr, rows, :] for pr in range(PAIRS)]
        return jnp.stack([x[:, h * HEAD_DIM:(h + 1) * HEAD_DIM] for x in pairs for h in range(LANES // HEAD_DIM)],
                         axis=0)

    for c in range(dil):
        for qi in range(ls // ATT_TQ):
            rows = pl.ds(c + dil * qi * ATT_TQ, ATT_TQ, stride=dil)
            q = heads(0, rows).astype(BF16)
            k = heads(1, rows)
            v = heads(2, rows)
            q_off = 0
            if qi > 0:
                prev = pl.ds(c + dil * (qi - 1) * ATT_TQ, ATT_TQ, stride=dil)
                k = jnp.concatenate([heads(1, prev), k], axis=1)
                v = jnp.concatenate([heads(2, prev), v], axis=1)
                q_off = ATT_TQ
            s = jnp.einsum("hqd,hkd->hqk", q, k.astype(BF16), preferred_element_type=F32) * scale
            dist = (lax.broadcasted_iota(jnp.int32, s.shape, 1) + q_off) - lax.broadcasted_iota(jnp.int32, s.shape, 2)
            s = jnp.where((dist >= 0) & (dist <= N_KEYS_PAST), s, NEG_INF)
            m = jnp.max(s, axis=-1, keepdims=True)
            p = jnp.exp(s - m)
            l = jnp.sum(p, axis=-1, keepdims=True)
            lse = jnp.broadcast_to(m + jnp.log(l), (HEADS_PER_GROUP, ATT_TQ, HEAD_DIM))
            o = jnp.einsum("hqk,hkd->hqd", (p / l).astype(BF16), v.astype(BF16), preferred_element_type=F32)
            for h in range(HEADS_PER_GROUP):
                o_ref[h, rows, :] = jnp.concatenate([o[h], lse[h]], axis=1)


def _att_prompt(qkv, dil):
    return pl.pallas_call(
        functools.partial(_att_prompt_kernel, dil=dil),
        grid=(BATCH,),
        in_specs=[pl.BlockSpec((QKV_TILES, SEQ, LANES), lambda b: (0, b, 0))],
        out_specs=pl.BlockSpec((HEADS_PER_GROUP, SEQ, LANES), lambda b: (0, b, 0)),
        out_shape=jax.ShapeDtypeStruct((HEADS_PER_GROUP, M_PROMPT, LANES), F32),
        compiler_params=_params("parallel"),
        name=f"att_prompt_d{dil}",
    )(qkv)


CACHE_BATCH_PER_STEP = {128: 16, 512: 8, 2048: 2}


def _cache_attn_kernel(qkv_ref, c_ref, o_ref, cout_ref, *, dil, nb):
    n_tiles = c_ref.shape[-1] // LANES
    scale = HEAD_DIM ** -0.5
    r_i = lax.broadcasted_iota(jnp.int32, (HEAD_DIM, HEAD_DIM), 0)
    c_i = lax.broadcasted_iota(jnp.int32, (HEAD_DIM, HEAD_DIM), 1)
    eye = (r_i == c_i).astype(F32)
    on_grid = (lax.broadcasted_iota(jnp.int32, (1, LANES), 1) % dil) == 0
    last_lane = lax.broadcasted_iota(jnp.int32, (HEAD_DIM, LANES), 1) == LANES - 1

    def to_col(rows):
        return jnp.sum(eye * rows, axis=2, keepdims=True)

    rows = qkv_ref[...]
    parts = []
    for h in range(HEADS_PER_GROUP):
        q = rows[:, :, h * HEAD_DIM:(h + 1) * HEAD_DIM]
        k_new = rows[:, :, D_GROUP + h * HEAD_DIM:D_GROUP + (h + 1) * HEAD_DIM]
        v_new = rows[:, :, 2 * D_GROUP + h * HEAD_DIM:2 * D_GROUP + (h + 1) * HEAD_DIM]
        q_col, k_col, v_col = to_col(q), to_col(k_new), to_col(v_new)
        s_tiles = [None] * n_tiles
        nxt = jnp.broadcast_to(k_col, (nb, HEAD_DIM, LANES))
        for j in reversed(range(n_tiles)):
            ts = slice(j * LANES, (j + 1) * LANES)
            kt = c_ref[:, 0, h, :, ts]
            s = jnp.sum(kt * q_col, axis=1, keepdims=True) * scale
            s_tiles[j] = jnp.where(on_grid, s, NEG_INF)
            rolled = pltpu.roll(kt, LANES - 1, axis=2)
            cout_ref[:, 0, h, :, ts] = jnp.where(last_lane, nxt, rolled)
            nxt = rolled
        s_new = jnp.sum(k_new * q, axis=2, keepdims=True) * scale
        m_row = s_tiles[0]
        for j in range(1, n_tiles):
            m_row = jnp.maximum(m_row, s_tiles[j])
        m = jnp.maximum(jnp.max(m_row, axis=2, keepdims=True), s_new)
        p_new = jnp.exp(s_new - m)
        l_row = jnp.zeros((nb, 1, LANES), F32)
        acc = jnp.zeros((nb, HEAD_DIM, LANES), F32)
        nxt = jnp.broadcast_to(v_col, (nb, HEAD_DIM, LANES))
        for j in reversed(range(n_tiles)):
            ts = slice(j * LANES, (j + 1) * LANES)
            p = jnp.exp(s_tiles[j] - m)
            l_row = l_row + p
            vt = c_ref[:, 1, h, :, ts]
            acc = acc + vt * p
            rolled = pltpu.roll(vt, LANES - 1, axis=2)
            cout_ref[:, 1, h, :, ts] = jnp.where(last_lane, nxt, rolled)
            nxt = rolled
        l = jnp.sum(l_row, axis=2, keepdims=True) + p_new
        o_col = (jnp.sum(acc, axis=2, keepdims=True) + p_new * v_col) / l
        parts += [jnp.sum(eye * o_col, axis=1, keepdims=True), jnp.broadcast_to(m + jnp.log(l), (nb, 1, HEAD_DIM))]
    o_ref[...] = jnp.concatenate(parts, axis=2)


def _cache_attn(qkv, cache_t, dil):
    window = cache_t.shape[-1]
    nb = CACHE_BATCH_PER_STEP[window]
    blk = (nb, 2, HEADS_PER_GROUP, HEAD_DIM, window)
    return pl.pallas_call(
        functools.partial(_cache_attn_kernel, dil=dil, nb=nb),
        grid=(DEC_BATCH // nb,),
        in_specs=[pl.BlockSpec((nb, 1, 3 * D_GROUP), lambda i: (i, 0, 0)),
                  pl.BlockSpec(blk, lambda i: (i, 0, 0, 0, 0))],
        out_specs=[pl.BlockSpec((nb, 1, HEADS_PER_GROUP * LANES), lambda i: (i, 0, 0)),
                   pl.BlockSpec(blk, lambda i: (i, 0, 0, 0, 0))],
        out_shape=[jax.ShapeDtypeStruct((DEC_BATCH, 1, HEADS_PER_GROUP * LANES), F32),
                   jax.ShapeDtypeStruct(cache_t.shape, F32)],
        compiler_params=_params("parallel"),
        name=f"cache_attn_w{window}",
    )(qkv, cache_t)


MERGE_TM = 208


def _merge_kernel(x_ref, y_ref, bonus_ref, g_ref, a0_ref, a1_ref, a2_ref, pg_ref, bg_ref,
                  lnw_ref, lnb_ref, er_ref, eb_ref, wbr_ref, wba0_ref, wba1_ref, wba2_ref, wo_ref, o_ref):
    e = (er_ref[...], eb_ref[...])
    inv_n = 1.0 / HEAD_DIM
    y = y_ref[...]
    d = y - _segsum(y, e) * inv_n
    var = _segsum(d * d, e) * inv_n
    yn = d * lax.rsqrt(var + RWKV_LN_EPS) * lnw_ref[...] + lnb_ref[...]
    o_a = ((yn + bonus_ref[...]) * g_ref[...]).astype(BF16)
    br_a = jnp.dot(o_a, wbr_ref[...], preferred_element_type=F32)
    a = (a0_ref[...], a1_ref[...], a2_ref[...])
    m = jnp.maximum(jnp.maximum(a[0], a[1]), a[2])
    ex = [jnp.exp(t - m) for t in a]
    inv = 1.0 / (ex[0] + ex[1] + ex[2])
    br_b = jnp.zeros_like(br_a)
    for grp, wb_ref in enumerate((wba0_ref, wba1_ref, wba2_ref)):
        alpha = ex[grp] * inv
        parts = [a[grp][h] * pltpu.roll(alpha[h], HEAD_DIM, axis=1) for h in range(HEADS_PER_GROUP)]
        xg = jnp.concatenate(parts, axis=1).astype(BF16)
        br_b = br_b + jnp.dot(xg, wb_ref[...], preferred_element_type=F32)
    gates = jax.nn.sigmoid(pg_ref[...] + bg_ref[...])
    merged = (gates[:, 0:D_MODEL] * br_a + gates[:, D_MODEL:] * br_b).astype(BF16)
    o_ref[...] = x_ref[...] + jnp.dot(merged, wo_ref[...], preferred_element_type=F32)


def _merge(x, y, bonus, g, att, p_gate, b_gate, ln_w, ln_b, e_heads, w_br, w_ba, w_o):
    tile = lambda n: pl.BlockSpec((MERGE_TM, n), lambda i: (i, 0))
    row = lambda n: pl.BlockSpec((1, n), lambda i: (0, 0))
    att_tile = pl.BlockSpec((HEADS_PER_GROUP, MERGE_TM, LANES), lambda i: (0, i, 0))
    return pl.pallas_call(
        _merge_kernel,
        grid=(M_ALL // MERGE_TM,),
        in_specs=[tile(D_MODEL), tile(D_RWKV), tile(D_RWKV), tile(D_RWKV),
                  att_tile, att_tile, att_tile, tile(2 * D_MODEL), row(2 * D_MODEL),
                  row(D_RWKV), row(D_RWKV), _const_spec(e_heads[0].shape), _const_spec(e_heads[1].shape),
                  _const_spec(w_br.shape),
                  _const_spec(w_ba[0].shape), _const_spec(w_ba[1].shape), _const_spec(w_ba[2].shape),
                  _const_spec(w_o.shape)],
        out_specs=tile(D_MODEL),
        out_shape=jax.ShapeDtypeStruct((M_ALL, D_MODEL), F32),
        compiler_params=_params("parallel"),
        name="merge",
    )(x, y, bonus, g, att[0], att[1], att[2], p_gate, b_gate, ln_w, ln_b, *e_heads, w_br,
      w_ba[0], w_ba[1], w_ba[2], w_o)


def _split_bf16(w):
    hi = w.astype(BF16)
    return hi, (w - hi.astype(F32)).astype(BF16)


def _head_ones(n_heads):
    reduce = (jnp.arange(n_heads * HEAD_DIM)[:, None] // HEAD_DIM == jnp.arange(LANES)[None, :]).astype(BF16)
    return reduce, reduce.T


def _rotary_tables():
    half = HEAD_DIM // 2
    inv_freq = ROPE_THETA ** (-jnp.arange(half, dtype=F32) / half)
    pos = jnp.concatenate([jnp.tile(jnp.arange(SEQ), BATCH), jnp.full((DEC_BATCH,), PAST_LEN)]).astype(F32)
    ang = pos[:, None] * inv_freq[None, :]
    cos, sin = jnp.cos(ang), jnp.sin(ang)
    cos = jnp.concatenate([cos, cos] * (LANES // HEAD_DIM), axis=1)
    sin = jnp.concatenate([-sin, sin] * (LANES // HEAD_DIM), axis=1)
    return cos, sin


def kernel(x_prompt, x_sample, state_rwkv_shift, state_rwkv_wkv, cache_att_w128, cache_att_w512, cache_att_w2048,
           ffn1_norm, ffn1_w_in, ffn1_w_out, mix_norm, w_in, b_gate, rwkv_mu, rwkv_w0, rwkv_w2, rwkv_a0, rwkv_a2,
           rwkv_g2, rwkv_k_k, rwkv_k_a, rwkv_r_k, rwkv_ln_w, rwkv_ln_b, w_branch_rwkv, attn_q_norm, attn_k_norm,
           w_branch_attn, w_out, ffn2_norm, ffn2_w_in, ffn2_w_out):
    caches = (cache_att_w128, cache_att_w512, cache_att_w2048)

    w_mix = w_in[0]
    pad = RWKV_PROJ_PAD - D_RWKV_PROJ
    w_rwkv = jnp.pad(w_mix[:, :D_RWKV_PROJ], ((0, 0), (0, pad))).astype(BF16)
    w_att = w_mix[:, D_RWKV_PROJ:D_RWKV_PROJ + 3 * D_ATT].astype(BF16)
    w_gate = w_mix[:, D_RWKV_PROJ + 3 * D_ATT:].astype(BF16)
    mu = jnp.pad(rwkv_mu, ((0, 0), (0, pad)))
    state_shift = jnp.pad(state_rwkv_shift[0], ((0, 0), (0, pad)))
    zeros = jnp.zeros((D_DECAY_LORA, D_RWKV), F32)
    w_lora = jnp.concatenate([jnp.concatenate([rwkv_w2[0], zeros], axis=1),
                              jnp.concatenate([zeros, rwkv_a2[0]], axis=1)], axis=0)
    w_glora = jnp.pad(rwkv_g2[0], ((0, 2 * LANES - D_GATE_LORA), (0, 0)))
    wl_hi, wl_lo = _split_bf16(w_lora)
    wg_hi, wg_lo = _split_bf16(w_glora)
    e_heads = _head_ones(RWKV_HEADS)
    e_att = jnp.kron(jnp.eye(ATT_HEADS, dtype=F32), jnp.ones((HEAD_DIM, HEAD_DIM), F32)).astype(BF16)
    cos, sin = _rotary_tables()
    q_gain = jnp.tile(attn_q_norm, (1, ATT_HEADS))
    k_gain = jnp.tile(attn_k_norm, (1, ATT_HEADS))
    wba = w_branch_attn[0].reshape(len(ATT_GROUPS), HEADS_PER_GROUP, HEAD_DIM, D_MODEL)
    wba = jnp.pad(wba, ((0, 0), (0, 0), (0, LANES - HEAD_DIM), (0, 0))).astype(BF16)
    wba = wba.reshape(len(ATT_GROUPS), HEADS_PER_GROUP * LANES, D_MODEL)

    x1, h_mix = _ffn(x_prompt.reshape(M_PROMPT, D_MODEL), ffn1_norm, ffn1_w_in[0].astype(BF16),
                     ffn1_w_out[0].astype(BF16), x_sample=x_sample.reshape(DEC_BATCH, D_MODEL), next_norm=mix_norm)

    p_rwkv = _proj(h_mix, w_rwkv, 512, "proj_rwkv")
    kw, v_r, kw_s_t, v_s_t, bonus, g_r = _rwkv_prep(p_rwkv, state_shift, mu, rwkv_w0, rwkv_a0, rwkv_k_k, rwkv_k_a,
                                                    rwkv_r_k.reshape(1, D_RWKV), wl_hi, wl_lo, wg_hi, wg_lo, e_heads)
    kw, v_r, h_att = lax.optimization_barrier((kw, v_r, h_mix))
    kw_cm = jnp.swapaxes(kw, 1, 2)
    v_t = v_r.reshape(BATCH, SEQ, RWKV_HEADS, HEAD_DIM).transpose(1, 3, 0, 2).reshape(SEQ, HEAD_DIM, BH)

    p_att = _proj(h_att, w_att, 768, "proj_att")
    p_gate = _proj(h_att, w_gate, 512, "proj_gate")
    kw_cm, p_att, p_gate = lax.optimization_barrier((kw_cm, p_att, p_gate))
    kw_t = kw_cm.reshape(5, RWKV_HEADS, 2, K_HALF, BATCH, SEQ).transpose(5, 0, 3, 2, 4, 1).reshape(SEQ, 5, K_HALF, LANES)
    qkv = _att_prep(p_att, cos, sin, q_gain, k_gain, e_att)
    att, kv_p, kv_s = [], [], []
    for g, (window, dil) in enumerate(ATT_GROUPS):
        att_p = _att_prompt(qkv[g], dil)
        keep = min(window, SEQ)
        kv = qkv[g][PAIRS:, :M_PROMPT].reshape(2, PAIRS, BATCH, SEQ, LANES // HEAD_DIM, HEAD_DIM)
        kv = kv.transpose(2, 3, 0, 1, 4, 5).reshape(BATCH, SEQ, 2, HEADS_PER_GROUP, HEAD_DIM)
        kv_p.append(kv[:, SEQ - keep:][None])
        qkv_s = qkv[g][:, M_PROMPT:].transpose(1, 0, 2).reshape(DEC_BATCH, 1, 3 * D_GROUP)
        o_s, cache_new = _cache_attn(qkv_s, caches[g][0].transpose(0, 2, 3, 4, 1), dil)
        att_s = o_s.reshape(DEC_BATCH, HEADS_PER_GROUP, LANES).transpose(1, 0, 2)
        att.append(jnp.concatenate([att_p, att_s], axis=1))
        kv_s.append(cache_new.transpose(0, 4, 1, 2, 3)[None])

    kw_t, v_t, att = lax.optimization_barrier((kw_t, v_t, att))
    y_t, s_t = _rwkv_seq(kw_t, v_t)
    y_p = y_t[:, :, :BH].reshape(SEQ, HEAD_DIM, BATCH, RWKV_HEADS).transpose(2, 0, 3, 1).reshape(M_PROMPT, D_RWKV)
    wkv_p = s_t.reshape(K_HALF, HEAD_DIM, 2, BATCH, RWKV_HEADS).transpose(3, 4, 1, 2, 0)
    wkv_p = wkv_p.reshape(BATCH, RWKV_HEADS, HEAD_DIM, HEAD_DIM)
    y_s_t, wkv_s_t = _rwkv_step(kw_s_t, v_s_t, state_rwkv_wkv[0].transpose(1, 2, 3, 0))
    wkv_s = wkv_s_t.transpose(3, 0, 1, 2)
    y_r = jnp.concatenate([y_p, y_s_t.T], axis=0)

    x2 = _merge(x1, y_r, bonus, g_r, att, p_gate, b_gate, rwkv_ln_w, rwkv_ln_b,
                e_heads, w_branch_rwkv[0].astype(BF16), wba, w_out[0].astype(BF16))
    y_prompt, y_sample = _ffn(x2, ffn2_norm, ffn2_w_in[0].astype(BF16), ffn2_w_out[0].astype(BF16), split_out=True)

    shift_p = jnp.concatenate([p_rwkv[(b + 1) * SEQ - 1:(b + 1) * SEQ, :D_RWKV_PROJ] for b in range(BATCH)])[None]
    shift_s = p_rwkv[M_PROMPT:, :D_RWKV_PROJ][None]
    return (y_prompt.reshape(BATCH, SEQ, D_MODEL), y_sample.reshape(DEC_BATCH, 1, D_MODEL),
            shift_p, wkv_p[None], kv_p[0], kv_p[1], kv_p[2],
            shift_s, wkv_s[None], kv_s[0], kv_s[1], kv_s[2])
```

```python
import functools

import jax
import jax.numpy as jnp
from jax import lax
from jax.experimental import pallas as pl
from jax.experimental.pallas import tpu as pltpu

D_MODEL = 2048
BATCH = 4
SEQ = 2048
DEC_BATCH = 128
PAST_LEN = 2048
M_PROMPT = BATCH * SEQ
M_ALL = M_PROMPT + DEC_BATCH

RWKV_HEADS = 16
HEAD_DIM = 64
D_RWKV = RWKV_HEADS * HEAD_DIM
D_DECAY_LORA = 64
D_ICLR_LORA = 64
D_GATE_LORA = 160
D_RWKV_PROJ = 3 * D_RWKV + D_DECAY_LORA + D_ICLR_LORA + D_GATE_LORA
RWKV_PROJ_PAD = 3584
RWKV_LN_EPS = 64e-5

ATT_GROUPS = ((128, 1), (512, 4), (2048, 16))
HEADS_PER_GROUP = 4
ATT_HEADS = HEADS_PER_GROUP * len(ATT_GROUPS)
D_ATT = ATT_HEADS * HEAD_DIM
D_GROUP = HEADS_PER_GROUP * HEAD_DIM
N_KEYS_PAST = 128
ROPE_THETA = 10000.0
D_FF = 5632
NORM_EPS = 1e-6
NEG_INF = -1e30

LANES = 128
VMEM_LIMIT = 56 * 1024 * 1024

F32 = jnp.float32
BF16 = jnp.bfloat16


def _params(*sem):
    return pltpu.CompilerParams(dimension_semantics=sem, vmem_limit_bytes=VMEM_LIMIT)


def _const_spec(shape):
    nd = len(shape)
    return pl.BlockSpec(shape, lambda *_: (0,) * nd, pipeline_mode=pl.Buffered(1))


def _rms_to_bf16(x, g):
    ms = jnp.mean(x * x, axis=-1, keepdims=True)
    return (x * lax.rsqrt(ms + NORM_EPS) * g).astype(BF16)


def _split_dot(x, w):
    hi = x.astype(BF16)
    lo = (x - hi.astype(F32)).astype(BF16)
    return jnp.dot(hi, w, preferred_element_type=F32) + jnp.dot(lo, w, preferred_element_type=F32)


def _segsum(x, e):
    if isinstance(e, tuple):
        return _split_dot(_split_dot(x, e[0]), e[1])
    return _split_dot(x, e)


def _dot_split(x, w_hi, w_lo):
    hi = x.astype(BF16)
    lo = (x - hi.astype(F32)).astype(BF16)
    return (jnp.dot(hi, w_hi, preferred_element_type=F32) + jnp.dot(lo, w_hi, preferred_element_type=F32)
            + jnp.dot(hi, w_lo, preferred_element_type=F32))


FFN_TM = 640
FFN_TF = 512


FFN_SAMPLE_ROW0 = M_PROMPT % FFN_TM
assert FFN_TM - FFN_SAMPLE_ROW0 == DEC_BATCH


def _ffn_kernel(*refs, split_in, split_out):
    refs = list(refs)
    x_ref = refs.pop(0)
    xs_ref = refs.pop(0) if split_in else None
    g_ref, wg_ref, wu_ref, wo_ref = refs[:4]
    refs = refs[4:]
    gn_ref = refs.pop(0) if split_in else None
    o_ref = refs.pop(0)
    os_ref = refs.pop(0) if split_out else None
    hn_ref = refs.pop(0) if split_in else None
    h_ref = refs.pop(0)
    xin_ref = refs.pop(0) if split_in else x_ref
    j = pl.program_id(1)
    last_tile = pl.program_id(0) == pl.num_programs(0) - 1

    def start(src_ref):
        h_ref[...] = _rms_to_bf16(src_ref[...], g_ref[...])
        o_ref[...] = jnp.zeros_like(o_ref)

    def finish(src_ref):
        out = src_ref[...] + 0.5 * o_ref[...]
        o_ref[...] = out
        if split_in:
            hn_ref[...] = _rms_to_bf16(out, gn_ref[...])
        return out

    first_j = j == 0
    last_j = j == pl.num_programs(1) - 1
    if split_in:
        @pl.when(first_j & last_tile)
        def _():
            xin_ref[0:FFN_SAMPLE_ROW0] = x_ref[0:FFN_SAMPLE_ROW0]
            xin_ref[FFN_SAMPLE_ROW0:] = xs_ref[...]
            start(xin_ref)

        @pl.when(first_j & jnp.logical_not(last_tile))
        def _():
            start(x_ref)
    else:
        @pl.when(first_j)
        def _():
            start(x_ref)

    h = h_ref[...]
    gate = jnp.dot(h, wg_ref[...], preferred_element_type=F32)
    up = jnp.dot(h, wu_ref[...], preferred_element_type=F32)
    act = (gate * jax.nn.sigmoid(gate) * up).astype(BF16)
    o_ref[...] += jnp.dot(act, wo_ref[...], preferred_element_type=F32)

    if split_in:
        @pl.when(last_j & last_tile)
        def _():
            finish(xin_ref)

        @pl.when(last_j & jnp.logical_not(last_tile))
        def _():
            finish(x_ref)
    else:
        @pl.when(last_j)
        def _():
            out = finish(x_ref)
            if split_out:
                @pl.when(last_tile)
                def _():
                    os_ref[...] = out[FFN_SAMPLE_ROW0:]


def _ffn(x, g, w_in, w_out, x_sample=None, next_norm=None, split_out=False):
    nj = D_FF // FFN_TF
    split_in = x_sample is not None
    row = pl.BlockSpec((1, D_MODEL), lambda i, j: (0, 0))
    tile = pl.BlockSpec((FFN_TM, D_MODEL), lambda i, j: (i, 0))
    sample = pl.BlockSpec((DEC_BATCH, D_MODEL), lambda i, j: (0, 0))
    weights = [pl.BlockSpec((D_MODEL, FFN_TF), lambda i, j: (0, j)),
               pl.BlockSpec((D_MODEL, FFN_TF), lambda i, j: (0, j + nj)),
               pl.BlockSpec((FFN_TF, D_MODEL), lambda i, j: (j, 0))]
    in_specs = [tile] + ([sample] if split_in else []) + [row] + weights + ([row] if split_in else [])
    args = [x] + ([x_sample] if split_in else []) + [g, w_in, w_in, w_out] + ([next_norm] if split_in else [])
    scratch = [pltpu.VMEM((FFN_TM, D_MODEL), BF16)] + ([pltpu.VMEM((FFN_TM, D_MODEL), F32)] if split_in else [])
    if split_out:
        out_specs = [tile, sample]
        out_shape = [jax.ShapeDtypeStruct((M_PROMPT, D_MODEL), F32), jax.ShapeDtypeStruct((DEC_BATCH, D_MODEL), F32)]
    else:
        out_specs = [tile, tile]
        out_shape = [jax.ShapeDtypeStruct((M_ALL, D_MODEL), F32), jax.ShapeDtypeStruct((M_ALL, D_MODEL), BF16)]
    return pl.pallas_call(
        functools.partial(_ffn_kernel, split_in=split_in, split_out=split_out),
        grid=(M_ALL // FFN_TM, nj),
        in_specs=in_specs,
        out_specs=out_specs,
        out_shape=out_shape,
        scratch_shapes=scratch,
        compiler_params=_params("arbitrary" if split_out else "parallel", "arbitrary"),
        name="ffn",
    )(*args)


PROJ_TM = 1664


def _proj_kernel(h_ref, w_ref, o_ref):
    o_ref[...] = jnp.dot(h_ref[...], w_ref[...], preferred_element_type=F32).astype(o_ref.dtype)


def _proj(h, w, tn, name, out_dtype=F32):
    m, n = h.shape[0], w.shape[1]
    return pl.pallas_call(
        _proj_kernel,
        grid=(m // PROJ_TM, n // tn),
        in_specs=[
            pl.BlockSpec((PROJ_TM, D_MODEL), lambda i, j: (i, 0)),
            pl.BlockSpec((D_MODEL, tn), lambda i, j: (0, j)),
        ],
        out_specs=pl.BlockSpec((PROJ_TM, tn), lambda i, j: (i, j)),
        out_shape=jax.ShapeDtypeStruct((m, n), out_dtype),
        compiler_params=_params("parallel", "arbitrary"),
        name=name,
    )(h, w)


PREP_TM = 128
PREP_TILES_PER_SEQ = SEQ // PREP_TM
PREP_SAMPLE_TILE = M_PROMPT // PREP_TM
KW_R, KW_DECAY, KW_K, KW_KK, KW_B = range(5)


def _prep_tile(i):
    return jnp.where(i == 0, PREP_SAMPLE_TILE, i - 1)


def _rwkv_prep_kernel(p_ref, tail_ref, state_ref, mu_ref, w0_ref, a0_ref, kk_ref, ka_ref, rk_ref,
                      wl_hi_ref, wl_lo_ref, wg_hi_ref, wg_lo_ref, er_ref, eb_ref,
                      kw_ref, v_ref, kws_ref, vs_ref, bonus_ref, g_ref):
    i = pl.program_id(0)
    is_sample = i == 0
    p = p_ref[...]
    tail = jnp.where((i - 1) % PREP_TILES_PER_SEQ == 0, 0.0, tail_ref[7:8, :])
    row = lax.broadcasted_iota(jnp.int32, p.shape, 0)
    prev = jnp.where(row == 0, tail, pltpu.roll(p, 1, axis=0))
    prev = jnp.where(is_sample, state_ref[...], prev)
    pm = p + (prev - p) * mu_ref[...]

    r = pm[:, 0:D_RWKV]
    k = pm[:, D_RWKV:2 * D_RWKV]
    v = pm[:, 2 * D_RWKV:3 * D_RWKV]
    lora = pm[:, 3 * D_RWKV:]
    lane = lax.broadcasted_iota(jnp.int32, (p.shape[0], LANES), 1)
    x_wa = lora[:, 0:LANES]
    x_wa = jnp.where(lane < D_DECAY_LORA, jnp.tanh(x_wa), x_wa)
    x_g = jax.nn.sigmoid(lora[:, LANES:3 * LANES])
    wa = _dot_split(x_wa, wl_hi_ref[...], wl_lo_ref[...])
    g = _dot_split(x_g, wg_hi_ref[...], wg_lo_ref[...])

    z = -(w0_ref[...] + wa[:, 0:D_RWKV])
    softplus = jnp.maximum(z, 0.0) + jnp.log(1.0 + jnp.exp(-jnp.abs(z)))
    decay = jnp.exp(-jnp.exp(-softplus - 0.5))
    a = jax.nn.sigmoid(a0_ref[...] + wa[:, D_RWKV:])
    e = (er_ref[...], eb_ref[...])
    kk = k * kk_ref[...]
    kk = kk * lax.rsqrt(jnp.maximum(_segsum(kk * kk, e), 1e-24))
    k_hat = k * (1.0 + (a - 1.0) * ka_ref[...])
    kw = (r, decay, k_hat, kk, kk * a)
    bonus_ref[...] = (_segsum(r * k_hat * rk_ref[...], e) * v).astype(bonus_ref.dtype)
    g_ref[...] = g.astype(g_ref.dtype)

    @pl.when(i > 0)
    def _():
        for idx in range(5):
            kw_ref[idx] = kw[idx]
        v_ref[...] = v

    @pl.when(is_sample)
    def _():
        for idx in range(5):
            kws_ref[idx] = kw[idx].T
        vs_ref[...] = v.T


def _rwkv_prep(p_rwkv, state_shift, mu, w0, a0, k_k, k_a, r_k, wl_hi, wl_lo, wg_hi, wg_lo, e_heads):
    n_tiles = M_ALL // PREP_TM
    row = lambda n: pl.BlockSpec((1, n), lambda i: (0, 0))
    prompt_tile = lambda i: jnp.maximum(i - 1, 0)
    return pl.pallas_call(
        _rwkv_prep_kernel,
        grid=(n_tiles,),
        in_specs=[
            pl.BlockSpec((PREP_TM, RWKV_PROJ_PAD), lambda i: (_prep_tile(i), 0)),
            pl.BlockSpec((8, RWKV_PROJ_PAD), lambda i: (jnp.maximum((i - 1) * (PREP_TM // 8) - 1, 0), 0)),
            pl.BlockSpec((DEC_BATCH, RWKV_PROJ_PAD), lambda i: (0, 0)),
            row(RWKV_PROJ_PAD), row(D_RWKV), row(D_RWKV), row(D_RWKV), row(D_RWKV), row(D_RWKV),
            _const_spec(wl_hi.shape), _const_spec(wl_lo.shape),
            _const_spec(wg_hi.shape), _const_spec(wg_lo.shape),
            _const_spec(e_heads[0].shape), _const_spec(e_heads[1].shape),
        ],
        out_specs=[
            pl.BlockSpec((5, PREP_TM, D_RWKV), lambda i: (0, prompt_tile(i), 0)),
            pl.BlockSpec((PREP_TM, D_RWKV), lambda i: (prompt_tile(i), 0)),
            pl.BlockSpec((5, D_RWKV, DEC_BATCH), lambda i: (0, 0, 0)),
            pl.BlockSpec((D_RWKV, DEC_BATCH), lambda i: (0, 0)),
            pl.BlockSpec((PREP_TM, D_RWKV), lambda i: (_prep_tile(i), 0)),
            pl.BlockSpec((PREP_TM, D_RWKV), lambda i: (_prep_tile(i), 0)),
        ],
        out_shape=[
            jax.ShapeDtypeStruct((5, M_PROMPT, D_RWKV), F32),
            jax.ShapeDtypeStruct((M_PROMPT, D_RWKV), F32),
            jax.ShapeDtypeStruct((5, D_RWKV, DEC_BATCH), F32),
            jax.ShapeDtypeStruct((D_RWKV, DEC_BATCH), F32),
            jax.ShapeDtypeStruct((M_ALL, D_RWKV), BF16),
            jax.ShapeDtypeStruct((M_ALL, D_RWKV), BF16),
        ],
        compiler_params=_params("arbitrary"),
        name="rwkv_prep",
    )(p_rwkv, p_rwkv, state_shift, mu, w0, a0, k_k, k_a, r_k, wl_hi, wl_lo, wg_hi, wg_lo, *e_heads)


SEQ_TB = 64
K_HALF = HEAD_DIM // 2
BH = BATCH * RWKV_HEADS


def _rwkv_seq_kernel(kw_ref, vin_ref, y_ref, sout_ref, s_ref, v_ref):
    @pl.when(pl.program_id(0) == 0)
    def _():
        s_ref[...] = jnp.zeros_like(s_ref)

    vin = vin_ref[...]
    v_ref[...] = jnp.concatenate([vin, vin], axis=-1)

    sk = jnp.zeros((HEAD_DIM, LANES), F32)
    for kp in range(K_HALF):
        sk = sk + s_ref[kp] * kw_ref[0, KW_KK, pl.ds(kp, 1), :]

    n_vh = 4
    vh_rows = HEAD_DIM // n_vh

    def step(t, sk):
        t_next = jnp.minimum(t + 1, SEQ_TB - 1)
        sk_next = []
        for vh in range(n_vh):
            vs = pl.ds(vh * vh_rows, vh_rows)
            sk_h = sk[vh * vh_rows:(vh + 1) * vh_rows]
            sa = -(sk_h + pltpu.roll(sk_h, BH, axis=1))
            vt = v_ref[t, vs, :]
            y = jnp.zeros((vh_rows, LANES), F32)
            nxt = jnp.zeros((vh_rows, LANES), F32)
            for kp in range(K_HALF):
                s_new = (s_ref[kp, vs, :] * kw_ref[t, KW_DECAY, pl.ds(kp, 1), :]
                         + sa * kw_ref[t, KW_B, pl.ds(kp, 1), :]
                         + vt * kw_ref[t, KW_K, pl.ds(kp, 1), :])
                s_ref[kp, vs, :] = s_new
                y = y + s_new * kw_ref[t, KW_R, pl.ds(kp, 1), :]
                nxt = nxt + s_new * kw_ref[t_next, KW_KK, pl.ds(kp, 1), :]
            y_ref[t, vs, :] = y + pltpu.roll(y, BH, axis=1)
            sk_next.append(nxt)
        return jnp.concatenate(sk_next, axis=0)

    lax.fori_loop(0, SEQ_TB, step, sk)

    @pl.when(pl.program_id(0) == pl.num_programs(0) - 1)
    def _():
        sout_ref[...] = s_ref[...]


def _rwkv_seq(kw_t, v_t):
    return pl.pallas_call(
        _rwkv_seq_kernel,
        grid=(SEQ // SEQ_TB,),
        in_specs=[
            pl.BlockSpec((SEQ_TB, 5, K_HALF, LANES), lambda i: (i, 0, 0, 0)),
            pl.BlockSpec((SEQ_TB, HEAD_DIM, BH), lambda i: (i, 0, 0)),
        ],
        out_specs=[
            pl.BlockSpec((SEQ_TB, HEAD_DIM, LANES), lambda i: (i, 0, 0)),
            pl.BlockSpec((K_HALF, HEAD_DIM, LANES), lambda i: (0, 0, 0)),
        ],
        out_shape=[
            jax.ShapeDtypeStruct((SEQ, HEAD_DIM, LANES), F32),
            jax.ShapeDtypeStruct((K_HALF, HEAD_DIM, LANES), F32),
        ],
        scratch_shapes=[pltpu.VMEM((K_HALF, HEAD_DIM, LANES), F32), pltpu.VMEM((SEQ_TB, HEAD_DIM, LANES), F32)],
        compiler_params=_params("arbitrary"),
        name="rwkv_seq",
    )(kw_t, v_t)


def _rwkv_step_kernel(kw_ref, v_ref, s_ref, y_ref, sout_ref):
    def body(vi, carry):
        s = s_ref[0, vi]
        sa = -jnp.sum(s * kw_ref[KW_KK], axis=0, keepdims=True)
        s_new = s * kw_ref[KW_DECAY] + sa * kw_ref[KW_B] + v_ref[pl.ds(vi, 1), :] * kw_ref[KW_K]
        sout_ref[0, vi] = s_new
        y_ref[pl.ds(vi, 1), :] = jnp.sum(s_new * kw_ref[KW_R], axis=0, keepdims=True)
        return carry

    lax.fori_loop(0, HEAD_DIM, body, 0)


def _rwkv_step(kw_t, v_t, state_t):
    return pl.pallas_call(
        _rwkv_step_kernel,
        grid=(RWKV_HEADS,),
        in_specs=[
            pl.BlockSpec((5, HEAD_DIM, DEC_BATCH), lambda h: (0, h, 0)),
            pl.BlockSpec((HEAD_DIM, DEC_BATCH), lambda h: (h, 0)),
            pl.BlockSpec((1, HEAD_DIM, HEAD_DIM, DEC_BATCH), lambda h: (h, 0, 0, 0)),
        ],
        out_specs=[
            pl.BlockSpec((HEAD_DIM, DEC_BATCH), lambda h: (h, 0)),
            pl.BlockSpec((1, HEAD_DIM, HEAD_DIM, DEC_BATCH), lambda h: (h, 0, 0, 0)),
        ],
        out_shape=[
            jax.ShapeDtypeStruct((D_RWKV, DEC_BATCH), F32),
            jax.ShapeDtypeStruct((RWKV_HEADS, HEAD_DIM, HEAD_DIM, DEC_BATCH), F32),
        ],
        compiler_params=_params("parallel"),
        name="rwkv_step",
    )(kw_t, v_t, state_t)


ATT_PREP_TM = 640


def _swap_halves(x):
    lane = lax.broadcasted_iota(jnp.int32, x.shape, 1)
    first = (lane % HEAD_DIM) < HEAD_DIM // 2
    return jnp.where(first, pltpu.roll(x, LANES - HEAD_DIM // 2, axis=1), pltpu.roll(x, HEAD_DIM // 2, axis=1))


PAIRS = D_GROUP // LANES
QKV_TILES = 3 * PAIRS


def _att_prep_kernel(p_ref, cos_ref, sin_ref, qg_ref, kg_ref, e_ref, o0_ref, o1_ref, o2_ref):
    cos = cos_ref[...]
    sin = sin_ref[...]
    e = e_ref[...]
    o_refs = (o0_ref, o1_ref, o2_ref)
    for part, g_ref in ((0, qg_ref), (1, kg_ref)):
        x = p_ref[:, part * D_ATT:(part + 1) * D_ATT]
        ms = _segsum(x * x, e) * (1.0 / HEAD_DIM)
        xn = x * lax.rsqrt(ms + NORM_EPS) * g_ref[...]
        for c in range(D_ATT // LANES):
            xc = xn[:, c * LANES:(c + 1) * LANES]
            o_refs[c // PAIRS][part * PAIRS + c % PAIRS] = xc * cos + _swap_halves(xc) * sin
    for c in range(D_ATT // LANES):
        o_refs[c // PAIRS][2 * PAIRS + c % PAIRS] = p_ref[:, 2 * D_ATT + c * LANES:2 * D_ATT + (c + 1) * LANES]


def _att_prep(p_att, cos, sin, q_gain, k_gain, e_att):
    out = jax.ShapeDtypeStruct((QKV_TILES, M_ALL, LANES), F32)
    return pl.pallas_call(
        _att_prep_kernel,
        grid=(M_ALL // ATT_PREP_TM,),
        in_specs=[
            pl.BlockSpec((ATT_PREP_TM, 3 * D_ATT), lambda i: (i, 0)),
            pl.BlockSpec((ATT_PREP_TM, LANES), lambda i: (i, 0)),
            pl.BlockSpec((ATT_PREP_TM, LANES), lambda i: (i, 0)),
            pl.BlockSpec((1, D_ATT), lambda i: (0, 0)),
            pl.BlockSpec((1, D_ATT), lambda i: (0, 0)),
            _const_spec(e_att.shape),
        ],
        out_specs=[pl.BlockSpec((QKV_TILES, ATT_PREP_TM, LANES), lambda i: (0, i, 0))] * 3,
        out_shape=[out] * 3,
        compiler_params=_params("parallel"),
        name="att_prep",
    )(p_att, cos, sin, q_gain, k_gain, e_att)


ATT_TQ = 128


def _att_prompt_kernel(qkv_ref, o_ref, *, dil):
    scale = HEAD_DIM ** -0.5
    ls = SEQ // dil

    def heads(part, rows):
        pairs = [qkv_ref[part * PAIRS + pr, rows, :] for pr in range(PAIRS)]
        return jnp.stack([x[:, h * HEAD_DIM:(h + 1) * HEAD_DIM] for x in pairs for h in range(LANES // HEAD_DIM)],
                         axis=0)

    for c in range(dil):
        for qi in range(ls // ATT_TQ):
            rows = pl.ds(c + dil * qi * ATT_TQ, ATT_TQ, stride=dil)
            q = heads(0, rows).astype(BF16)
            k = heads(1, rows)
            v = heads(2, rows)
            q_off = 0
            if qi > 0:
                prev = pl.ds(c + dil * (qi - 1) * ATT_TQ, ATT_TQ, stride=dil)
                k = jnp.concatenate([heads(1, prev), k], axis=1)
                v = jnp.concatenate([heads(2, prev), v], axis=1)
                q_off = ATT_TQ
            s = jnp.einsum("hqd,hkd->hqk", q, k.astype(BF16), preferred_element_type=F32) * scale
            dist = (lax.broadcasted_iota(jnp.int32, s.shape, 1) + q_off) - lax.broadcasted_iota(jnp.int32, s.shape, 2)
            s = jnp.where((dist >= 0) & (dist <= N_KEYS_PAST), s, NEG_INF)
            m = jnp.max(s, axis=-1, keepdims=True)
            p = jnp.exp(s - m)
            l = jnp.sum(p, axis=-1, keepdims=True)
            lse = jnp.broadcast_to(m + jnp.log(l), (HEADS_PER_GROUP, ATT_TQ, HEAD_DIM))
            o = jnp.einsum("hqk,hkd->hqd", (p / l).astype(BF16), v.astype(BF16), preferred_element_type=F32)
            for h in range(HEADS_PER_GROUP):
                o_ref[h, rows, :] = jnp.concatenate([o[h], lse[h]], axis=1)


def _att_prompt(qkv, dil):
    return pl.pallas_call(
        functools.partial(_att_prompt_kernel, dil=dil),
        grid=(BATCH,),
        in_specs=[pl.BlockSpec((QKV_TILES, SEQ, LANES), lambda b: (0, b, 0))],
        out_specs=pl.BlockSpec((HEADS_PER_GROUP, SEQ, LANES), lambda b: (0, b, 0)),
        out_shape=jax.ShapeDtypeStruct((HEADS_PER_GROUP, M_PROMPT, LANES), F32),
        compiler_params=_params("parallel"),
        name=f"att_prompt_d{dil}",
    )(qkv)


CACHE_BATCH_PER_STEP = {128: 16, 512: 8, 2048: 2}


def _cache_attn_kernel(qkv_ref, c_ref, o_ref, cout_ref, *, dil, nb):
    n_tiles = c_ref.shape[-1] // LANES
    scale = HEAD_DIM ** -0.5
    r_i = lax.broadcasted_iota(jnp.int32, (HEAD_DIM, HEAD_DIM), 0)
    c_i = lax.broadcasted_iota(jnp.int32, (HEAD_DIM, HEAD_DIM), 1)
    eye = (r_i == c_i).astype(F32)
    on_grid = (lax.broadcasted_iota(jnp.int32, (1, LANES), 1) % dil) == 0
    last_lane = lax.broadcasted_iota(jnp.int32, (HEAD_DIM, LANES), 1) == LANES - 1

    def to_col(rows):
        return jnp.sum(eye * rows, axis=2, keepdims=True)

    rows = qkv_ref[...]
    parts = []
    for h in range(HEADS_PER_GROUP):
        q = rows[:, :, h * HEAD_DIM:(h + 1) * HEAD_DIM]
        k_new = rows[:, :, D_GROUP + h * HEAD_DIM:D_GROUP + (h + 1) * HEAD_DIM]
        v_new = rows[:, :, 2 * D_GROUP + h * HEAD_DIM:2 * D_GROUP + (h + 1) * HEAD_DIM]
        q_col, k_col, v_col = to_col(q), to_col(k_new), to_col(v_new)
        s_tiles = [None] * n_tiles
        nxt = jnp.broadcast_to(k_col, (nb, HEAD_DIM, LANES))
        for j in reversed(range(n_tiles)):
            ts = slice(j * LANES, (j + 1) * LANES)
            kt = c_ref[:, 0, h, :, ts]
            s = jnp.sum(kt * q_col, axis=1, keepdims=True) * scale
            s_tiles[j] = jnp.where(on_grid, s, NEG_INF)
            rolled = pltpu.roll(kt, LANES - 1, axis=2)
            cout_ref[:, 0, h, :, ts] = jnp.where(last_lane, nxt, rolled)
            nxt = rolled
        s_new = jnp.sum(k_new * q, axis=2, keepdims=True) * scale
        m_row = s_tiles[0]
        for j in range(1, n_tiles):
            m_row = jnp.maximum(m_row, s_tiles[j])
        m = jnp.maximum(jnp.max(m_row, axis=2, keepdims=True), s_new)
        p_new = jnp.exp(s_new - m)
        l_row = jnp.zeros((nb, 1, LANES), F32)
        acc = jnp.zeros((nb, HEAD_DIM, LANES), F32)
        nxt = jnp.broadcast_to(v_col, (nb, HEAD_DIM, LANES))
        for j in reversed(range(n_tiles)):
            ts = slice(j * LANES, (j + 1) * LANES)
            p = jnp.exp(s_tiles[j] - m)
            l_row = l_row + p
            vt = c_ref[:, 1, h, :, ts]
            acc = acc + vt * p
            rolled = pltpu.roll(vt, LANES - 1, axis=2)
            cout_ref[:, 1, h, :, ts] = jnp.where(last_lane, nxt, rolled)
            nxt = rolled
        l = jnp.sum(l_row, axis=2, keepdims=True) + p_new
        o_col = (jnp.sum(acc, axis=2, keepdims=True) + p_new * v_col) / l
        parts += [jnp.sum(eye * o_col, axis=1, keepdims=True), jnp.broadcast_to(m + jnp.log(l), (nb, 1, HEAD_DIM))]
    o_ref[...] = jnp.concatenate(parts, axis=2)


def _cache_attn(qkv, cache_t, dil):
    window = cache_t.shape[-1]
    nb = CACHE_BATCH_PER_STEP[window]
    blk = (nb, 2, HEADS_PER_GROUP, HEAD_DIM, window)
    return pl.pallas_call(
        functools.partial(_cache_attn_kernel, dil=dil, nb=nb),
        grid=(DEC_BATCH // nb,),
        in_specs=[pl.BlockSpec((nb, 1, 3 * D_GROUP), lambda i: (i, 0, 0)),
                  pl.BlockSpec(blk, lambda i: (i, 0, 0, 0, 0))],
        out_specs=[pl.BlockSpec((nb, 1, HEADS_PER_GROUP * LANES), lambda i: (i, 0, 0)),
                   pl.BlockSpec(blk, lambda i: (i, 0, 0, 0, 0))],
        out_shape=[jax.ShapeDtypeStruct((DEC_BATCH, 1, HEADS_PER_GROUP * LANES), F32),
                   jax.ShapeDtypeStruct(cache_t.shape, F32)],
        compiler_params=_params("parallel"),
        name=f"cache_attn_w{window}",
    )(qkv, cache_t)


MERGE_TM = 208


def _merge_kernel(x_ref, y_ref, bonus_ref, g_ref, a0_ref, a1_ref, a2_ref, pg_ref, bg_ref,
                  lnw_ref, lnb_ref, er_ref, eb_ref, wbr_ref, wba0_ref, wba1_ref, wba2_ref, wo_ref, o_ref):
    e = (er_ref[...], eb_ref[...])
    inv_n = 1.0 / HEAD_DIM
    y = y_ref[...]
    d = y - _segsum(y, e) * inv_n
    var = _segsum(d * d, e) * inv_n
    yn = d * lax.rsqrt(var + RWKV_LN_EPS) * lnw_ref[...] + lnb_ref[...]
    o_a = ((yn + bonus_ref[...].astype(F32)) * g_ref[...].astype(F32)).astype(BF16)
    br_a = jnp.dot(o_a, wbr_ref[...], preferred_element_type=F32)
    a = (a0_ref[...], a1_ref[...], a2_ref[...])
    m = jnp.maximum(jnp.maximum(a[0], a[1]), a[2])
    ex = [jnp.exp(t - m) for t in a]
    inv = 1.0 / (ex[0] + ex[1] + ex[2])
    br_b = jnp.zeros_like(br_a)
    for grp, wb_ref in enumerate((wba0_ref, wba1_ref, wba2_ref)):
        alpha = ex[grp] * inv
        parts = [a[grp][h] * pltpu.roll(alpha[h], HEAD_DIM, axis=1) for h in range(HEADS_PER_GROUP)]
        xg = jnp.concatenate(parts, axis=1).astype(BF16)
        br_b = br_b + jnp.dot(xg, wb_ref[...], preferred_element_type=F32)
    gates = jax.nn.sigmoid(pg_ref[...].astype(F32) + bg_ref[...])
    merged = (gates[:, 0:D_MODEL] * br_a + gates[:, D_MODEL:] * br_b).astype(BF16)
    o_ref[...] = x_ref[...] + jnp.dot(merged, wo_ref[...], preferred_element_type=F32)


def _merge(x, y, bonus, g, att, p_gate, b_gate, ln_w, ln_b, e_heads, w_br, w_ba, w_o):
    tile = lambda n: pl.BlockSpec((MERGE_TM, n), lambda i: (i, 0))
    row = lambda n: pl.BlockSpec((1, n), lambda i: (0, 0))
    att_tile = pl.BlockSpec((HEADS_PER_GROUP, MERGE_TM, LANES), lambda i: (0, i, 0))
    return pl.pallas_call(
        _merge_kernel,
        grid=(M_ALL // MERGE_TM,),
        in_specs=[tile(D_MODEL), tile(D_RWKV), tile(D_RWKV), tile(D_RWKV),
                  att_tile, att_tile, att_tile, tile(2 * D_MODEL), row(2 * D_MODEL),
                  row(D_RWKV), row(D_RWKV), _const_spec(e_heads[0].shape), _const_spec(e_heads[1].shape),
                  _const_spec(w_br.shape),
                  _const_spec(w_ba[0].shape), _const_spec(w_ba[1].shape), _const_spec(w_ba[2].shape),
                  _const_spec(w_o.shape)],
        out_specs=tile(D_MODEL),
        out_shape=jax.ShapeDtypeStruct((M_ALL, D_MODEL), F32),
        compiler_params=_params("parallel"),
        name="merge",
    )(x, y, bonus, g, att[0], att[1], att[2], p_gate, b_gate, ln_w, ln_b, *e_heads, w_br,
      w_ba[0], w_ba[1], w_ba[2], w_o)


def _split_bf16(w):
    hi = w.astype(BF16)
    return hi, (w - hi.astype(F32)).astype(BF16)


def _head_ones(n_heads):
    reduce = (jnp.arange(n_heads * HEAD_DIM)[:, None] // HEAD_DIM == jnp.arange(LANES)[None, :]).astype(BF16)
    return reduce, reduce.T


def _rotary_tables():
    half = HEAD_DIM // 2
    inv_freq = ROPE_THETA ** (-jnp.arange(half, dtype=F32) / half)
    pos = jnp.concatenate([jnp.tile(jnp.arange(SEQ), BATCH), jnp.full((DEC_BATCH,), PAST_LEN)]).astype(F32)
    ang = pos[:, None] * inv_freq[None, :]
    cos, sin = jnp.cos(ang), jnp.sin(ang)
    cos = jnp.concatenate([cos, cos] * (LANES // HEAD_DIM), axis=1)
    sin = jnp.concatenate([-sin, sin] * (LANES // HEAD_DIM), axis=1)
    return cos, sin


def kernel(x_prompt, x_sample, state_rwkv_shift, state_rwkv_wkv, cache_att_w128, cache_att_w512, cache_att_w2048,
           ffn1_norm, ffn1_w_in, ffn1_w_out, mix_norm, w_in, b_gate, rwkv_mu, rwkv_w0, rwkv_w2, rwkv_a0, rwkv_a2,
           rwkv_g2, rwkv_k_k, rwkv_k_a, rwkv_r_k, rwkv_ln_w, rwkv_ln_b, w_branch_rwkv, attn_q_norm, attn_k_norm,
           w_branch_attn, w_out, ffn2_norm, ffn2_w_in, ffn2_w_out):
    caches = (cache_att_w128, cache_att_w512, cache_att_w2048)

    w_mix = w_in[0]
    pad = RWKV_PROJ_PAD - D_RWKV_PROJ
    w_rwkv = jnp.pad(w_mix[:, :D_RWKV_PROJ], ((0, 0), (0, pad))).astype(BF16)
    w_att = w_mix[:, D_RWKV_PROJ:D_RWKV_PROJ + 3 * D_ATT].astype(BF16)
    w_gate = w_mix[:, D_RWKV_PROJ + 3 * D_ATT:].astype(BF16)
    mu = jnp.pad(rwkv_mu, ((0, 0), (0, pad)))
    state_shift = jnp.pad(state_rwkv_shift[0], ((0, 0), (0, pad)))
    zeros = jnp.zeros((D_DECAY_LORA, D_RWKV), F32)
    w_lora = jnp.concatenate([jnp.concatenate([rwkv_w2[0], zeros], axis=1),
                              jnp.concatenate([zeros, rwkv_a2[0]], axis=1)], axis=0)
    w_glora = jnp.pad(rwkv_g2[0], ((0, 2 * LANES - D_GATE_LORA), (0, 0)))
    wl_hi, wl_lo = _split_bf16(w_lora)
    wg_hi, wg_lo = _split_bf16(w_glora)
    e_heads = _head_ones(RWKV_HEADS)
    e_att = jnp.kron(jnp.eye(ATT_HEADS, dtype=F32), jnp.ones((HEAD_DIM, HEAD_DIM), F32)).astype(BF16)
    cos, sin = _rotary_tables()
    q_gain = jnp.tile(attn_q_norm, (1, ATT_HEADS))
    k_gain = jnp.tile(attn_k_norm, (1, ATT_HEADS))
    wba = w_branch_attn[0].reshape(len(ATT_GROUPS), HEADS_PER_GROUP, HEAD_DIM, D_MODEL)
    wba = jnp.pad(wba, ((0, 0), (0, 0), (0, LANES - HEAD_DIM), (0, 0))).astype(BF16)
    wba = wba.reshape(len(ATT_GROUPS), HEADS_PER_GROUP * LANES, D_MODEL)

    x1, h_mix = _ffn(x_prompt.reshape(M_PROMPT, D_MODEL), ffn1_norm, ffn1_w_in[0].astype(BF16),
                     ffn1_w_out[0].astype(BF16), x_sample=x_sample.reshape(DEC_BATCH, D_MODEL), next_norm=mix_norm)

    p_rwkv = _proj(h_mix, w_rwkv, 512, "proj_rwkv")
    kw, v_r, kw_s_t, v_s_t, bonus, g_r = _rwkv_prep(p_rwkv, state_shift, mu, rwkv_w0, rwkv_a0, rwkv_k_k, rwkv_k_a,
                                                    rwkv_r_k.reshape(1, D_RWKV), wl_hi, wl_lo, wg_hi, wg_lo, e_heads)
    kw, v_r, h_att = lax.optimization_barrier((kw, v_r, h_mix))
    kw_cm = jnp.swapaxes(kw, 1, 2)
    v_t = v_r.reshape(BATCH, SEQ, RWKV_HEADS, HEAD_DIM).transpose(1, 3, 0, 2).reshape(SEQ, HEAD_DIM, BH)

    p_att = _proj(h_att, w_att, 768, "proj_att")
    p_gate = _proj(h_att, w_gate, 512, "proj_gate", out_dtype=BF16)
    kw_cm, p_att, p_gate = lax.optimization_barrier((kw_cm, p_att, p_gate))
    kw_t = kw_cm.reshape(5, RWKV_HEADS, 2, K_HALF, BATCH, SEQ).transpose(5, 0, 3, 2, 4, 1).reshape(SEQ, 5, K_HALF, LANES)
    qkv = _att_prep(p_att, cos, sin, q_gain, k_gain, e_att)
    att, kv_p, kv_s = [], [], []
    for g, (window, dil) in enumerate(ATT_GROUPS):
        att_p = _att_prompt(qkv[g], dil)
        keep = min(window, SEQ)
        kv = qkv[g][PAIRS:, :M_PROMPT].reshape(2, PAIRS, BATCH, SEQ, LANES // HEAD_DIM, HEAD_DIM)
        kv = kv.transpose(2, 3, 0, 1, 4, 5).reshape(BATCH, SEQ, 2, HEADS_PER_GROUP, HEAD_DIM)
        kv_p.append(kv[:, SEQ - keep:][None])
        qkv_s = qkv[g][:, M_PROMPT:].transpose(1, 0, 2).reshape(DEC_BATCH, 1, 3 * D_GROUP)
        o_s, cache_new = _cache_attn(qkv_s, caches[g][0].transpose(0, 2, 3, 4, 1), dil)
        att_s = o_s.reshape(DEC_BATCH, HEADS_PER_GROUP, LANES).transpose(1, 0, 2)
        att.append(jnp.concatenate([att_p, att_s], axis=1))
        kv_s.append(cache_new.transpose(0, 4, 1, 2, 3)[None])

    kw_t, v_t, att = lax.optimization_barrier((kw_t, v_t, att))
    y_t, s_t = _rwkv_seq(kw_t, v_t)
    y_p = y_t[:, :, :BH].reshape(SEQ, HEAD_DIM, BATCH, RWKV_HEADS).transpose(2, 0, 3, 1).reshape(M_PROMPT, D_RWKV)
    wkv_p = s_t.reshape(K_HALF, HEAD_DIM, 2, BATCH, RWKV_HEADS).transpose(3, 4, 1, 2, 0)
    wkv_p = wkv_p.reshape(BATCH, RWKV_HEADS, HEAD_DIM, HEAD_DIM)
    y_s_t, wkv_s_t = _rwkv_step(kw_s_t, v_s_t, state_rwkv_wkv[0].transpose(1, 2, 3, 0))
    wkv_s = wkv_s_t.transpose(3, 0, 1, 2)
    y_r = jnp.concatenate([y_p, y_s_t.T], axis=0)

    x2 = _merge(x1, y_r, bonus, g_r, att, p_gate, b_gate, rwkv_ln_w, rwkv_ln_b,
                e_heads, w_branch_rwkv[0].astype(BF16), wba, w_out[0].astype(BF16))
    y_prompt, y_sample = _ffn(x2, ffn2_norm, ffn2_w_in[0].astype(BF16), ffn2_w_out[0].astype(BF16), split_out=True)

    shift_p = jnp.concatenate([p_rwkv[(b + 1) * SEQ - 1:(b + 1) * SEQ, :D_RWKV_PROJ] for b in range(BATCH)])[None]
    shift_s = p_rwkv[M_PROMPT:, :D_RWKV_PROJ][None]
    return (y_prompt.reshape(BATCH, SEQ, D_MODEL), y_sample.reshape(DEC_BATCH, 1, D_MODEL),
            shift_p, wkv_p[None], kv_p[0], kv_p[1], kv_p[2],
            shift_s, wkv_s[None], kv_s[0], kv_s[1], kv_s[2])
```

```python
import functools

import jax
import jax.numpy as jnp
from jax import lax
from jax.experimental import pallas as pl
from jax.experimental.pallas import tpu as pltpu

D_MODEL = 2048
BATCH = 4
SEQ = 2048
DEC_BATCH = 128
PAST_LEN = 2048
M_PROMPT = BATCH * SEQ
M_ALL = M_PROMPT + DEC_BATCH

RWKV_HEADS = 16
HEAD_DIM = 64
D_RWKV = RWKV_HEADS * HEAD_DIM
D_DECAY_LORA = 64
D_ICLR_LORA = 64
D_GATE_LORA = 160
D_RWKV_PROJ = 3 * D_RWKV + D_DECAY_LORA + D_ICLR_LORA + D_GATE_LORA
RWKV_PROJ_PAD = 3584
RWKV_LN_EPS = 64e-5

ATT_GROUPS = ((128, 1), (512, 4), (2048, 16))
HEADS_PER_GROUP = 4
ATT_HEADS = HEADS_PER_GROUP * len(ATT_GROUPS)
D_ATT = ATT_HEADS * HEAD_DIM
D_GROUP = HEADS_PER_GROUP * HEAD_DIM
N_KEYS_PAST = 128
ROPE_THETA = 10000.0
D_FF = 5632
NORM_EPS = 1e-6
NEG_INF = -1e30

LANES = 128
VMEM_LIMIT = 56 * 1024 * 1024

F32 = jnp.float32
BF16 = jnp.bfloat16


def _params(*sem):
    return pltpu.CompilerParams(dimension_semantics=sem, vmem_limit_bytes=VMEM_LIMIT)


def _const_spec(shape):
    nd = len(shape)
    return pl.BlockSpec(shape, lambda *_: (0,) * nd, pipeline_mode=pl.Buffered(1))


def _rms_to_bf16(x, g):
    ms = jnp.mean(x * x, axis=-1, keepdims=True)
    return (x * lax.rsqrt(ms + NORM_EPS) * g).astype(BF16)


def _split_dot(x, w):
    hi = x.astype(BF16)
    lo = (x - hi.astype(F32)).astype(BF16)
    return jnp.dot(hi, w, preferred_element_type=F32) + jnp.dot(lo, w, preferred_element_type=F32)


def _segsum(x, e):
    if isinstance(e, tuple):
        return _split_dot(_split_dot(x, e[0]), e[1])
    return _split_dot(x, e)


def _dot_split(x, w_hi, w_lo):
    hi = x.astype(BF16)
    lo = (x - hi.astype(F32)).astype(BF16)
    return (jnp.dot(hi, w_hi, preferred_element_type=F32) + jnp.dot(lo, w_hi, preferred_element_type=F32)
            + jnp.dot(hi, w_lo, preferred_element_type=F32))


FFN_TM = 640
FFN_TF = 512


FFN_SAMPLE_ROW0 = M_PROMPT % FFN_TM
assert FFN_TM - FFN_SAMPLE_ROW0 == DEC_BATCH


def _ffn_kernel(*refs, split_in, split_out):
    refs = list(refs)
    x_ref = refs.pop(0)
    xs_ref = refs.pop(0) if split_in else None
    g_ref, wg_ref, wu_ref, wo_ref = refs[:4]
    refs = refs[4:]
    gn_ref = refs.pop(0) if split_in else None
    o_ref = refs.pop(0)
    os_ref = refs.pop(0) if split_out else None
    hn_ref = refs.pop(0) if split_in else None
    h_ref = refs.pop(0)
    xin_ref = refs.pop(0) if split_in else x_ref
    j = pl.program_id(1)
    last_tile = pl.program_id(0) == pl.num_programs(0) - 1

    def start(src_ref):
        h_ref[...] = _rms_to_bf16(src_ref[...], g_ref[...])
        o_ref[...] = jnp.zeros_like(o_ref)

    def finish(src_ref):
        out = src_ref[...] + 0.5 * o_ref[...]
        o_ref[...] = out
        if split_in:
            hn_ref[...] = _rms_to_bf16(out, gn_ref[...])
        return out

    first_j = j == 0
    last_j = j == pl.num_programs(1) - 1
    if split_in:
        @pl.when(first_j & last_tile)
        def _():
            xin_ref[0:FFN_SAMPLE_ROW0] = x_ref[0:FFN_SAMPLE_ROW0]
            xin_ref[FFN_SAMPLE_ROW0:] = xs_ref[...]
            start(xin_ref)

        @pl.when(first_j & jnp.logical_not(last_tile))
        def _():
            start(x_ref)
    else:
        @pl.when(first_j)
        def _():
            start(x_ref)

    h = h_ref[...]
    gate = jnp.dot(h, wg_ref[...], preferred_element_type=F32)
    up = jnp.dot(h, wu_ref[...], preferred_element_type=F32)
    act = (gate * jax.nn.sigmoid(gate) * up).astype(BF16)
    o_ref[...] += jnp.dot(act, wo_ref[...], preferred_element_type=F32)

    if split_in:
        @pl.when(last_j & last_tile)
        def _():
            finish(xin_ref)

        @pl.when(last_j & jnp.logical_not(last_tile))
        def _():
            finish(x_ref)
    else:
        @pl.when(last_j)
        def _():
            out = finish(x_ref)
            if split_out:
                @pl.when(last_tile)
                def _():
                    os_ref[...] = out[FFN_SAMPLE_ROW0:]


def _ffn(x, g, w_in, w_out, x_sample=None, next_norm=None, split_out=False):
    nj = D_FF // FFN_TF
    split_in = x_sample is not None
    row = pl.BlockSpec((1, D_MODEL), lambda i, j: (0, 0))
    tile = pl.BlockSpec((FFN_TM, D_MODEL), lambda i, j: (i, 0))
    sample = pl.BlockSpec((DEC_BATCH, D_MODEL), lambda i, j: (0, 0))
    weights = [pl.BlockSpec((D_MODEL, FFN_TF), lambda i, j: (0, j)),
               pl.BlockSpec((D_MODEL, FFN_TF), lambda i, j: (0, j + nj)),
               pl.BlockSpec((FFN_TF, D_MODEL), lambda i, j: (j, 0))]
    in_specs = [tile] + ([sample] if split_in else []) + [row] + weights + ([row] if split_in else [])
    args = [x] + ([x_sample] if split_in else []) + [g, w_in, w_in, w_out] + ([next_norm] if split_in else [])
    scratch = [pltpu.VMEM((FFN_TM, D_MODEL), BF16)] + ([pltpu.VMEM((FFN_TM, D_MODEL), F32)] if split_in else [])
    if split_out:
        out_specs = [tile, sample]
        out_shape = [jax.ShapeDtypeStruct((M_PROMPT, D_MODEL), F32), jax.ShapeDtypeStruct((DEC_BATCH, D_MODEL), F32)]
    else:
        out_specs = [tile, tile]
        out_shape = [jax.ShapeDtypeStruct((M_ALL, D_MODEL), F32), jax.ShapeDtypeStruct((M_ALL, D_MODEL), BF16)]
    return pl.pallas_call(
        functools.partial(_ffn_kernel, split_in=split_in, split_out=split_out),
        grid=(M_ALL // FFN_TM, nj),
        in_specs=in_specs,
        out_specs=out_specs,
        out_shape=out_shape,
        scratch_shapes=scratch,
        compiler_params=_params("arbitrary" if split_out else "parallel", "arbitrary"),
        name="ffn",
    )(*args)


PROJ_TM = 1664


def _proj_kernel(h_ref, w_ref, o_ref):
    o_ref[...] = jnp.dot(h_ref[...], w_ref[...], preferred_element_type=F32).astype(o_ref.dtype)


def _proj(h, w, tn, name, out_dtype=F32):
    m, n = h.shape[0], w.shape[1]
    return pl.pallas_call(
        _proj_kernel,
        grid=(m // PROJ_TM, n // tn),
        in_specs=[
            pl.BlockSpec((PROJ_TM, D_MODEL), lambda i, j: (i, 0)),
            pl.BlockSpec((D_MODEL, tn), lambda i, j: (0, j)),
        ],
        out_specs=pl.BlockSpec((PROJ_TM, tn), lambda i, j: (i, j)),
        out_shape=jax.ShapeDtypeStruct((m, n), out_dtype),
        compiler_params=_params("parallel", "arbitrary"),
        name=name,
    )(h, w)


PREP_TM = 128
PREP_TILES_PER_SEQ = SEQ // PREP_TM
PREP_SAMPLE_TILE = M_PROMPT // PREP_TM
KW_R, KW_DECAY, KW_K, KW_KK, KW_B = range(5)


def _prep_tile(i):
    return jnp.where(i == 0, PREP_SAMPLE_TILE, i - 1)


def _rwkv_prep_kernel(p_ref, tail_ref, state_ref, mu_ref, w0_ref, a0_ref, kk_ref, ka_ref, rk_ref,
                      wl_hi_ref, wl_lo_ref, wg_hi_ref, wg_lo_ref, er_ref, eb_ref,
                      kw_ref, v_ref, kws_ref, vs_ref, bonus_ref, g_ref):
    i = pl.program_id(0)
    is_sample = i == 0
    p = p_ref[...]
    tail = jnp.where((i - 1) % PREP_TILES_PER_SEQ == 0, 0.0, tail_ref[7:8, :])
    row = lax.broadcasted_iota(jnp.int32, p.shape, 0)
    prev = jnp.where(row == 0, tail, pltpu.roll(p, 1, axis=0))
    prev = jnp.where(is_sample, state_ref[...], prev)
    pm = p + (prev - p) * mu_ref[...]

    r = pm[:, 0:D_RWKV]
    k = pm[:, D_RWKV:2 * D_RWKV]
    v = pm[:, 2 * D_RWKV:3 * D_RWKV]
    lora = pm[:, 3 * D_RWKV:]
    lane = lax.broadcasted_iota(jnp.int32, (p.shape[0], LANES), 1)
    x_wa = lora[:, 0:LANES]
    x_wa = jnp.where(lane < D_DECAY_LORA, jnp.tanh(x_wa), x_wa)
    x_g = jax.nn.sigmoid(lora[:, LANES:3 * LANES])
    wa = _dot_split(x_wa, wl_hi_ref[...], wl_lo_ref[...])
    g = _dot_split(x_g, wg_hi_ref[...], wg_lo_ref[...])

    z = -(w0_ref[...] + wa[:, 0:D_RWKV])
    softplus = jnp.maximum(z, 0.0) + jnp.log(1.0 + jnp.exp(-jnp.abs(z)))
    decay = jnp.exp(-jnp.exp(-softplus - 0.5))
    a = jax.nn.sigmoid(a0_ref[...] + wa[:, D_RWKV:])
    e = (er_ref[...], eb_ref[...])
    kk = k * kk_ref[...]
    kk = kk * lax.rsqrt(jnp.maximum(_segsum(kk * kk, e), 1e-24))
    k_hat = k * (1.0 + (a - 1.0) * ka_ref[...])
    kw = (r, decay, k_hat, kk, kk * a)
    bonus_ref[...] = (_segsum(r * k_hat * rk_ref[...], e) * v).astype(bonus_ref.dtype)
    g_ref[...] = g.astype(g_ref.dtype)

    @pl.when(i > 0)
    def _():
        for idx in range(5):
            kw_ref[idx] = kw[idx]
        v_ref[...] = v

    @pl.when(is_sample)
    def _():
        for idx in range(5):
            kws_ref[idx] = kw[idx].T
        vs_ref[...] = v.T


def _rwkv_prep(p_rwkv, state_shift, mu, w0, a0, k_k, k_a, r_k, wl_hi, wl_lo, wg_hi, wg_lo, e_heads):
    n_tiles = M_ALL // PREP_TM
    row = lambda n: pl.BlockSpec((1, n), lambda i: (0, 0))
    prompt_tile = lambda i: jnp.maximum(i - 1, 0)
    return pl.pallas_call(
        _rwkv_prep_kernel,
        grid=(n_tiles,),
        in_specs=[
            pl.BlockSpec((PREP_TM, RWKV_PROJ_PAD), lambda i: (_prep_tile(i), 0)),
            pl.BlockSpec((8, RWKV_PROJ_PAD), lambda i: (jnp.maximum((i - 1) * (PREP_TM // 8) - 1, 0), 0)),
            pl.BlockSpec((DEC_BATCH, RWKV_PROJ_PAD), lambda i: (0, 0)),
            row(RWKV_PROJ_PAD), row(D_RWKV), row(D_RWKV), row(D_RWKV), row(D_RWKV), row(D_RWKV),
            _const_spec(wl_hi.shape), _const_spec(wl_lo.shape),
            _const_spec(wg_hi.shape), _const_spec(wg_lo.shape),
            _const_spec(e_heads[0].shape), _const_spec(e_heads[1].shape),
        ],
        out_specs=[
            pl.BlockSpec((5, PREP_TM, D_RWKV), lambda i: (0, prompt_tile(i), 0)),
            pl.BlockSpec((PREP_TM, D_RWKV), lambda i: (prompt_tile(i), 0)),
            pl.BlockSpec((5, D_RWKV, DEC_BATCH), lambda i: (0, 0, 0)),
            pl.BlockSpec((D_RWKV, DEC_BATCH), lambda i: (0, 0)),
            pl.BlockSpec((PREP_TM, D_RWKV), lambda i: (_prep_tile(i), 0)),
            pl.BlockSpec((PREP_TM, D_RWKV), lambda i: (_prep_tile(i), 0)),
        ],
        out_shape=[
            jax.ShapeDtypeStruct((5, M_PROMPT, D_RWKV), F32),
            jax.ShapeDtypeStruct((M_PROMPT, D_RWKV), F32),
            jax.ShapeDtypeStruct((5, D_RWKV, DEC_BATCH), F32),
            jax.ShapeDtypeStruct((D_RWKV, DEC_BATCH), F32),
            jax.ShapeDtypeStruct((M_ALL, D_RWKV), BF16),
            jax.ShapeDtypeStruct((M_ALL, D_RWKV), BF16),
        ],
        compiler_params=_params("arbitrary"),
        name="rwkv_prep",
    )(p_rwkv, p_rwkv, state_shift, mu, w0, a0, k_k, k_a, r_k, wl_hi, wl_lo, wg_hi, wg_lo, *e_heads)


SEQ_TB = 64
K_HALF = HEAD_DIM // 2
BH = BATCH * RWKV_HEADS


def _rwkv_seq_kernel(kw_ref, v_ref, y_ref, sout_ref, s_ref):
    @pl.when(pl.program_id(0) == 0)
    def _():
        s_ref[...] = jnp.zeros_like(s_ref)

    sk = jnp.zeros((HEAD_DIM, LANES), F32)
    for kp in range(K_HALF):
        sk = sk + s_ref[kp] * kw_ref[0, KW_KK, pl.ds(kp, 1), :]

    n_vh = 4
    vh_rows = HEAD_DIM // n_vh

    def step(t, sk):
        t_next = jnp.minimum(t + 1, SEQ_TB - 1)
        sk_next = []
        for vh in range(n_vh):
            vs = pl.ds(vh * vh_rows, vh_rows)
            sk_h = sk[vh * vh_rows:(vh + 1) * vh_rows]
            sa = -(sk_h + pltpu.roll(sk_h, BH, axis=1))
            vt = v_ref[t, vs, :]
            y = jnp.zeros((vh_rows, LANES), F32)
            nxt = jnp.zeros((vh_rows, LANES), F32)
            for kp in range(K_HALF):
                s_new = (s_ref[kp, vs, :] * kw_ref[t, KW_DECAY, pl.ds(kp, 1), :]
                         + sa * kw_ref[t, KW_B, pl.ds(kp, 1), :]
                         + vt * kw_ref[t, KW_K, pl.ds(kp, 1), :])
                s_ref[kp, vs, :] = s_new
                y = y + s_new * kw_ref[t, KW_R, pl.ds(kp, 1), :]
                nxt = nxt + s_new * kw_ref[t_next, KW_KK, pl.ds(kp, 1), :]
            y_ref[t, vs, :] = y + pltpu.roll(y, BH, axis=1)
            sk_next.append(nxt)
        return jnp.concatenate(sk_next, axis=0)

    lax.fori_loop(0, SEQ_TB, step, sk)

    @pl.when(pl.program_id(0) == pl.num_programs(0) - 1)
    def _():
        sout_ref[...] = s_ref[...]


def _rwkv_seq(kw_t, v_t):
    return pl.pallas_call(
        _rwkv_seq_kernel,
        grid=(SEQ // SEQ_TB,),
        in_specs=[
            pl.BlockSpec((SEQ_TB, 5, K_HALF, LANES), lambda i: (i, 0, 0, 0)),
            pl.BlockSpec((SEQ_TB, HEAD_DIM, LANES), lambda i: (i, 0, 0)),
        ],
        out_specs=[
            pl.BlockSpec((SEQ_TB, HEAD_DIM, LANES), lambda i: (i, 0, 0)),
            pl.BlockSpec((K_HALF, HEAD_DIM, LANES), lambda i: (0, 0, 0)),
        ],
        out_shape=[
            jax.ShapeDtypeStruct((SEQ, HEAD_DIM, LANES), F32),
            jax.ShapeDtypeStruct((K_HALF, HEAD_DIM, LANES), F32),
        ],
        scratch_shapes=[pltpu.VMEM((K_HALF, HEAD_DIM, LANES), F32)],
        compiler_params=_params("arbitrary"),
        name="rwkv_seq",
    )(kw_t, v_t)


def _rwkv_step_kernel(kw_ref, v_ref, s_ref, y_ref, sout_ref):
    def body(vi, carry):
        s = s_ref[0, vi]
        sa = -jnp.sum(s * kw_ref[KW_KK], axis=0, keepdims=True)
        s_new = s * kw_ref[KW_DECAY] + sa * kw_ref[KW_B] + v_ref[pl.ds(vi, 1), :] * kw_ref[KW_K]
        sout_ref[0, vi] = s_new
        y_ref[pl.ds(vi, 1), :] = jnp.sum(s_new * kw_ref[KW_R], axis=0, keepdims=True)
        return carry

    lax.fori_loop(0, HEAD_DIM, body, 0)


def _rwkv_step(kw_t, v_t, state_t):
    return pl.pallas_call(
        _rwkv_step_kernel,
        grid=(RWKV_HEADS,),
        in_specs=[
            pl.BlockSpec((5, HEAD_DIM, DEC_BATCH), lambda h: (0, h, 0)),
            pl.BlockSpec((HEAD_DIM, DEC_BATCH), lambda h: (h, 0)),
            pl.BlockSpec((1, HEAD_DIM, HEAD_DIM, DEC_BATCH), lambda h: (h, 0, 0, 0)),
        ],
        out_specs=[
            pl.BlockSpec((HEAD_DIM, DEC_BATCH), lambda h: (h, 0)),
            pl.BlockSpec((1, HEAD_DIM, HEAD_DIM, DEC_BATCH), lambda h: (h, 0, 0, 0)),
        ],
        out_shape=[
            jax.ShapeDtypeStruct((D_RWKV, DEC_BATCH), F32),
            jax.ShapeDtypeStruct((RWKV_HEADS, HEAD_DIM, HEAD_DIM, DEC_BATCH), F32),
        ],
        compiler_params=_params("parallel"),
        name="rwkv_step",
    )(kw_t, v_t, state_t)


ATT_PREP_TM = 640


def _swap_halves(x):
    lane = lax.broadcasted_iota(jnp.int32, x.shape, 1)
    first = (lane % HEAD_DIM) < HEAD_DIM // 2
    return jnp.where(first, pltpu.roll(x, LANES - HEAD_DIM // 2, axis=1), pltpu.roll(x, HEAD_DIM // 2, axis=1))


PAIRS = D_GROUP // LANES
QKV_TILES = 3 * PAIRS


def _att_prep_kernel(p_ref, cos_ref, sin_ref, qg_ref, kg_ref, e_ref, o0_ref, o1_ref, o2_ref):
    cos = cos_ref[...]
    sin = sin_ref[...]
    e = e_ref[...]
    o_refs = (o0_ref, o1_ref, o2_ref)
    for part, g_ref in ((0, qg_ref), (1, kg_ref)):
        x = p_ref[:, part * D_ATT:(part + 1) * D_ATT]
        ms = _segsum(x * x, e) * (1.0 / HEAD_DIM)
        xn = x * lax.rsqrt(ms + NORM_EPS) * g_ref[...]
        for c in range(D_ATT // LANES):
            xc = xn[:, c * LANES:(c + 1) * LANES]
            o_refs[c // PAIRS][part * PAIRS + c % PAIRS] = xc * cos + _swap_halves(xc) * sin
    for c in range(D_ATT // LANES):
        o_refs[c // PAIRS][2 * PAIRS + c % PAIRS] = p_ref[:, 2 * D_ATT + c * LANES:2 * D_ATT + (c + 1) * LANES]


def _att_prep(p_att, cos, sin, q_gain, k_gain, e_att):
    out = jax.ShapeDtypeStruct((QKV_TILES, M_ALL, LANES), F32)
    return pl.pallas_call(
        _att_prep_kernel,
        grid=(M_ALL // ATT_PREP_TM,),
        in_specs=[
            pl.BlockSpec((ATT_PREP_TM, 3 * D_ATT), lambda i: (i, 0)),
            pl.BlockSpec((ATT_PREP_TM, LANES), lambda i: (i, 0)),
            pl.BlockSpec((ATT_PREP_TM, LANES), lambda i: (i, 0)),
            pl.BlockSpec((1, D_ATT), lambda i: (0, 0)),
            pl.BlockSpec((1, D_ATT), lambda i: (0, 0)),
            _const_spec(e_att.shape),
        ],
        out_specs=[pl.BlockSpec((QKV_TILES, ATT_PREP_TM, LANES), lambda i: (0, i, 0))] * 3,
        out_shape=[out] * 3,
        compiler_params=_params("parallel"),
        name="att_prep",
    )(p_att, cos, sin, q_gain, k_gain, e_att)


ATT_TQ = 128


def _att_prompt_kernel(qkv_ref, o_ref, *, dil):
    scale = HEAD_DIM ** -0.5
    ls = SEQ // dil

    def heads(part, rows):
        pairs = [qkv_ref[part * PAIRS + pr, rows, :] for pr in range(PAIRS)]
        return jnp.stack([x[:, h * HEAD_DIM:(h + 1) * HEAD_DIM] for x in pairs for h in range(LANES // HEAD_DIM)],
                         axis=0)

    for c in range(dil):
        for qi in range(ls // ATT_TQ):
            rows = pl.ds(c + dil * qi * ATT_TQ, ATT_TQ, stride=dil)
            q = heads(0, rows).astype(BF16)
            k = heads(1, rows)
            v = heads(2, rows)
            q_off = 0
            if qi > 0:
                prev = pl.ds(c + dil * (qi - 1) * ATT_TQ, ATT_TQ, stride=dil)
                k = jnp.concatenate([heads(1, prev), k], axis=1)
                v = jnp.concatenate([heads(2, prev), v], axis=1)
                q_off = ATT_TQ
            s = jnp.einsum("hqd,hkd->hqk", q, k.astype(BF16), preferred_element_type=F32) * scale
            dist = (lax.broadcasted_iota(jnp.int32, s.shape, 1) + q_off) - lax.broadcasted_iota(jnp.int32, s.shape, 2)
            s = jnp.where((dist >= 0) & (dist <= N_KEYS_PAST), s, NEG_INF)
            m = jnp.max(s, axis=-1, keepdims=True)
            p = jnp.exp(s - m)
            l = jnp.sum(p, axis=-1, keepdims=True)
            lse = jnp.broadcast_to(m + jnp.log(l), (HEADS_PER_GROUP, ATT_TQ, HEAD_DIM))
            o = jnp.einsum("hqk,hkd->hqd", (p / l).astype(BF16), v.astype(BF16), preferred_element_type=F32)
            for h in range(HEADS_PER_GROUP):
                o_ref[h, rows, :] = jnp.concatenate([o[h], lse[h]], axis=1)


def _att_prompt(qkv, dil):
    return pl.pallas_call(
        functools.partial(_att_prompt_kernel, dil=dil),
        grid=(BATCH,),
        in_specs=[pl.BlockSpec((QKV_TILES, SEQ, LANES), lambda b: (0, b, 0))],
        out_specs=pl.BlockSpec((HEADS_PER_GROUP, SEQ, LANES), lambda b: (0, b, 0)),
        out_shape=jax.ShapeDtypeStruct((HEADS_PER_GROUP, M_PROMPT, LANES), F32),
        compiler_params=_params("parallel"),
        name=f"att_prompt_d{dil}",
    )(qkv)


CACHE_BATCH_PER_STEP = {128: 16, 512: 8, 2048: 2}


def _cache_attn_kernel(qkv_ref, c_ref, o_ref, cout_ref, *, dil, nb):
    n_tiles = c_ref.shape[-1] // LANES
    scale = HEAD_DIM ** -0.5
    r_i = lax.broadcasted_iota(jnp.int32, (HEAD_DIM, HEAD_DIM), 0)
    c_i = lax.broadcasted_iota(jnp.int32, (HEAD_DIM, HEAD_DIM), 1)
    eye = (r_i == c_i).astype(F32)
    on_grid = (lax.broadcasted_iota(jnp.int32, (1, LANES), 1) % dil) == 0
    last_lane = lax.broadcasted_iota(jnp.int32, (HEAD_DIM, LANES), 1) == LANES - 1

    def to_col(rows):
        return jnp.sum(eye * rows, axis=2, keepdims=True)

    rows = qkv_ref[...]
    parts = []
    for h in range(HEADS_PER_GROUP):
        q = rows[:, :, h * HEAD_DIM:(h + 1) * HEAD_DIM]
        k_new = rows[:, :, D_GROUP + h * HEAD_DIM:D_GROUP + (h + 1) * HEAD_DIM]
        v_new = rows[:, :, 2 * D_GROUP + h * HEAD_DIM:2 * D_GROUP + (h + 1) * HEAD_DIM]
        q_col, k_col, v_col = to_col(q), to_col(k_new), to_col(v_new)
        s_tiles = [None] * n_tiles
        nxt = jnp.broadcast_to(k_col, (nb, HEAD_DIM, LANES))
        for j in reversed(range(n_tiles)):
            ts = slice(j * LANES, (j + 1) * LANES)
            kt = c_ref[:, 0, h, :, ts]
            s = jnp.sum(kt * q_col, axis=1, keepdims=True) * scale
            s_tiles[j] = jnp.where(on_grid, s, NEG_INF)
            rolled = pltpu.roll(kt, LANES - 1, axis=2)
            cout_ref[:, 0, h, :, ts] = jnp.where(last_lane, nxt, rolled)
            nxt = rolled
        s_new = jnp.sum(k_new * q, axis=2, keepdims=True) * scale
        m_row = s_tiles[0]
        for j in range(1, n_tiles):
            m_row = jnp.maximum(m_row, s_tiles[j])
        m = jnp.maximum(jnp.max(m_row, axis=2, keepdims=True), s_new)
        p_new = jnp.exp(s_new - m)
        l_row = jnp.zeros((nb, 1, LANES), F32)
        acc = jnp.zeros((nb, HEAD_DIM, LANES), F32)
        nxt = jnp.broadcast_to(v_col, (nb, HEAD_DIM, LANES))
        for j in reversed(range(n_tiles)):
            ts = slice(j * LANES, (j + 1) * LANES)
            p = jnp.exp(s_tiles[j] - m)
            l_row = l_row + p
            vt = c_ref[:, 1, h, :, ts]
            acc = acc + vt * p
            rolled = pltpu.roll(vt, LANES - 1, axis=2)
            cout_ref[:, 1, h, :, ts] = jnp.where(last_lane, nxt, rolled)
            nxt = rolled
        l = jnp.sum(l_row, axis=2, keepdims=True) + p_new
        o_col = (jnp.sum(acc, axis=2, keepdims=True) + p_new * v_col) / l
        parts += [jnp.sum(eye * o_col, axis=1, keepdims=True), jnp.broadcast_to(m + jnp.log(l), (nb, 1, HEAD_DIM))]
    o_ref[...] = jnp.concatenate(parts, axis=2)


def _cache_attn(qkv, cache_t, dil):
    window = cache_t.shape[-1]
    nb = CACHE_BATCH_PER_STEP[window]
    blk = (nb, 2, HEADS_PER_GROUP, HEAD_DIM, window)
    return pl.pallas_call(
        functools.partial(_cache_attn_kernel, dil=dil, nb=nb),
        grid=(DEC_BATCH // nb,),
        in_specs=[pl.BlockSpec((nb, 1, 3 * D_GROUP), lambda i: (i, 0, 0)),
                  pl.BlockSpec(blk, lambda i: (i, 0, 0, 0, 0))],
        out_specs=[pl.BlockSpec((nb, 1, HEADS_PER_GROUP * LANES), lambda i: (i, 0, 0)),
                   pl.BlockSpec(blk, lambda i: (i, 0, 0, 0, 0))],
        out_shape=[jax.ShapeDtypeStruct((DEC_BATCH, 1, HEADS_PER_GROUP * LANES), F32),
                   jax.ShapeDtypeStruct(cache_t.shape, F32)],
        compiler_params=_params("parallel"),
        name=f"cache_attn_w{window}",
    )(qkv, cache_t)


MERGE_TM = 208


def _merge_kernel(x_ref, y_ref, bonus_ref, g_ref, a0_ref, a1_ref, a2_ref, pg_ref, bg_ref,
                  lnw_ref, lnb_ref, er_ref, eb_ref, wbr_ref, wba0_ref, wba1_ref, wba2_ref, wo_ref, o_ref):
    e = (er_ref[...], eb_ref[...])
    inv_n = 1.0 / HEAD_DIM
    y = y_ref[...]
    d = y - _segsum(y, e) * inv_n
    var = _segsum(d * d, e) * inv_n
    yn = d * lax.rsqrt(var + RWKV_LN_EPS) * lnw_ref[...] + lnb_ref[...]
    o_a = ((yn + bonus_ref[...].astype(F32)) * g_ref[...].astype(F32)).astype(BF16)
    br_a = jnp.dot(o_a, wbr_ref[...], preferred_element_type=F32)
    a = (a0_ref[...], a1_ref[...], a2_ref[...])
    m = jnp.maximum(jnp.maximum(a[0], a[1]), a[2])
    ex = [jnp.exp(t - m) for t in a]
    inv = 1.0 / (ex[0] + ex[1] + ex[2])
    br_b = jnp.zeros_like(br_a)
    for grp, wb_ref in enumerate((wba0_ref, wba1_ref, wba2_ref)):
        alpha = ex[grp] * inv
        parts = [a[grp][h] * pltpu.roll(alpha[h], HEAD_DIM, axis=1) for h in range(HEADS_PER_GROUP)]
        xg = jnp.concatenate(parts, axis=1).astype(BF16)
        br_b = br_b + jnp.dot(xg, wb_ref[...], preferred_element_type=F32)
    gates = jax.nn.sigmoid(pg_ref[...].astype(F32) + bg_ref[...])
    merged = (gates[:, 0:D_MODEL] * br_a + gates[:, D_MODEL:] * br_b).astype(BF16)
    o_ref[...] = x_ref[...] + jnp.dot(merged, wo_ref[...], preferred_element_type=F32)


def _merge(x, y, bonus, g, att, p_gate, b_gate, ln_w, ln_b, e_heads, w_br, w_ba, w_o):
    tile = lambda n: pl.BlockSpec((MERGE_TM, n), lambda i: (i, 0))
    row = lambda n: pl.BlockSpec((1, n), lambda i: (0, 0))
    att_tile = pl.BlockSpec((HEADS_PER_GROUP, MERGE_TM, LANES), lambda i: (0, i, 0))
    return pl.pallas_call(
        _merge_kernel,
        grid=(M_ALL // MERGE_TM,),
        in_specs=[tile(D_MODEL), tile(D_RWKV), tile(D_RWKV), tile(D_RWKV),
                  att_tile, att_tile, att_tile, tile(2 * D_MODEL), row(2 * D_MODEL),
                  row(D_RWKV), row(D_RWKV), _const_spec(e_heads[0].shape), _const_spec(e_heads[1].shape),
                  _const_spec(w_br.shape),
                  _const_spec(w_ba[0].shape), _const_spec(w_ba[1].shape), _const_spec(w_ba[2].shape),
                  _const_spec(w_o.shape)],
        out_specs=tile(D_MODEL),
        out_shape=jax.ShapeDtypeStruct((M_ALL, D_MODEL), F32),
        compiler_params=_params("parallel"),
        name="merge",
    )(x, y, bonus, g, att[0], att[1], att[2], p_gate, b_gate, ln_w, ln_b, *e_heads, w_br,
      w_ba[0], w_ba[1], w_ba[2], w_o)


def _split_bf16(w):
    hi = w.astype(BF16)
    return hi, (w - hi.astype(F32)).astype(BF16)


def _head_ones(n_heads):
    reduce = (jnp.arange(n_heads * HEAD_DIM)[:, None] // HEAD_DIM == jnp.arange(LANES)[None, :]).astype(BF16)
    return reduce, reduce.T


def _rotary_tables():
    half = HEAD_DIM // 2
    inv_freq = ROPE_THETA ** (-jnp.arange(half, dtype=F32) / half)
    pos = jnp.concatenate([jnp.tile(jnp.arange(SEQ), BATCH), jnp.full((DEC_BATCH,), PAST_LEN)]).astype(F32)
    ang = pos[:, None] * inv_freq[None, :]
    cos, sin = jnp.cos(ang), jnp.sin(ang)
    cos = jnp.concatenate([cos, cos] * (LANES // HEAD_DIM), axis=1)
    sin = jnp.concatenate([-sin, sin] * (LANES // HEAD_DIM), axis=1)
    return cos, sin


def kernel(x_prompt, x_sample, state_rwkv_shift, state_rwkv_wkv, cache_att_w128, cache_att_w512, cache_att_w2048,
           ffn1_norm, ffn1_w_in, ffn1_w_out, mix_norm, w_in, b_gate, rwkv_mu, rwkv_w0, rwkv_w2, rwkv_a0, rwkv_a2,
           rwkv_g2, rwkv_k_k, rwkv_k_a, rwkv_r_k, rwkv_ln_w, rwkv_ln_b, w_branch_rwkv, attn_q_norm, attn_k_norm,
           w_branch_attn, w_out, ffn2_norm, ffn2_w_in, ffn2_w_out):
    caches = (cache_att_w128, cache_att_w512, cache_att_w2048)

    w_mix = w_in[0]
    pad = RWKV_PROJ_PAD - D_RWKV_PROJ
    w_rwkv = jnp.pad(w_mix[:, :D_RWKV_PROJ], ((0, 0), (0, pad))).astype(BF16)
    w_att = w_mix[:, D_RWKV_PROJ:D_RWKV_PROJ + 3 * D_ATT].astype(BF16)
    w_gate = w_mix[:, D_RWKV_PROJ + 3 * D_ATT:].astype(BF16)
    mu = jnp.pad(rwkv_mu, ((0, 0), (0, pad)))
    state_shift = jnp.pad(state_rwkv_shift[0], ((0, 0), (0, pad)))
    zeros = jnp.zeros((D_DECAY_LORA, D_RWKV), F32)
    w_lora = jnp.concatenate([jnp.concatenate([rwkv_w2[0], zeros], axis=1),
                              jnp.concatenate([zeros, rwkv_a2[0]], axis=1)], axis=0)
    w_glora = jnp.pad(rwkv_g2[0], ((0, 2 * LANES - D_GATE_LORA), (0, 0)))
    wl_hi, wl_lo = _split_bf16(w_lora)
    wg_hi, wg_lo = _split_bf16(w_glora)
    e_heads = _head_ones(RWKV_HEADS)
    e_att = jnp.kron(jnp.eye(ATT_HEADS, dtype=F32), jnp.ones((HEAD_DIM, HEAD_DIM), F32)).astype(BF16)
    cos, sin = _rotary_tables()
    q_gain = jnp.tile(attn_q_norm, (1, ATT_HEADS))
    k_gain = jnp.tile(attn_k_norm, (1, ATT_HEADS))
    wba = w_branch_attn[0].reshape(len(ATT_GROUPS), HEADS_PER_GROUP, HEAD_DIM, D_MODEL)
    wba = jnp.pad(wba, ((0, 0), (0, 0), (0, LANES - HEAD_DIM), (0, 0))).astype(BF16)
    wba = wba.reshape(len(ATT_GROUPS), HEADS_PER_GROUP * LANES, D_MODEL)

    x1, h_mix = _ffn(x_prompt.reshape(M_PROMPT, D_MODEL), ffn1_norm, ffn1_w_in[0].astype(BF16),
                     ffn1_w_out[0].astype(BF16), x_sample=x_sample.reshape(DEC_BATCH, D_MODEL), next_norm=mix_norm)

    p_rwkv = _proj(h_mix, w_rwkv, 512, "proj_rwkv")
    kw, v_r, kw_s_t, v_s_t, bonus, g_r = _rwkv_prep(p_rwkv, state_shift, mu, rwkv_w0, rwkv_a0, rwkv_k_k, rwkv_k_a,
                                                    rwkv_r_k.reshape(1, D_RWKV), wl_hi, wl_lo, wg_hi, wg_lo, e_heads)
    kw, v_r, h_att = lax.optimization_barrier((kw, v_r, h_mix))
    kw_cm = jnp.swapaxes(kw.reshape(5, BATCH, SEQ, D_RWKV), 2, 3)
    v_cm = jnp.swapaxes(v_r.reshape(BATCH, SEQ, D_RWKV), 1, 2)

    p_att = _proj(h_att, w_att, 768, "proj_att")
    p_gate = _proj(h_att, w_gate, 512, "proj_gate", out_dtype=BF16)
    kw_cm, v_cm, p_att, p_gate = lax.optimization_barrier((kw_cm, v_cm, p_att, p_gate))
    v_rows = v_cm.reshape(BATCH, RWKV_HEADS, HEAD_DIM, SEQ).transpose(2, 0, 1, 3)
    v_rows = jnp.broadcast_to(v_rows[:, None], (HEAD_DIM, 2, BATCH, RWKV_HEADS, SEQ))
    v_rows = lax.optimization_barrier(v_rows.reshape(HEAD_DIM * LANES, SEQ))
    v_t = v_rows.T.reshape(SEQ, HEAD_DIM, LANES)
    kw_rows = kw_cm.reshape(5, BATCH, RWKV_HEADS, 2, K_HALF, SEQ).transpose(0, 4, 3, 1, 2, 5)
    kw_rows = lax.optimization_barrier(kw_rows.reshape(5 * K_HALF * LANES, SEQ))
    kw_t = kw_rows.T.reshape(SEQ, 5, K_HALF, LANES)
    qkv = _att_prep(p_att, cos, sin, q_gain, k_gain, e_att)
    att, kv_p, kv_s = [], [], []
    for g, (window, dil) in enumerate(ATT_GROUPS):
        att_p = _att_prompt(qkv[g], dil)
        keep = min(window, SEQ)
        kv = qkv[g][PAIRS:, :M_PROMPT].reshape(2, PAIRS, BATCH, SEQ, LANES // HEAD_DIM, HEAD_DIM)
        kv = kv.transpose(2, 3, 0, 1, 4, 5).reshape(BATCH, SEQ, 2, HEADS_PER_GROUP, HEAD_DIM)
        kv_p.append(kv[:, SEQ - keep:][None])
        qkv_s = qkv[g][:, M_PROMPT:].transpose(1, 0, 2).reshape(DEC_BATCH, 1, 3 * D_GROUP)
        o_s, cache_new = _cache_attn(qkv_s, caches[g][0].transpose(0, 2, 3, 4, 1), dil)
        att_s = o_s.reshape(DEC_BATCH, HEADS_PER_GROUP, LANES).transpose(1, 0, 2)
        att.append(jnp.concatenate([att_p, att_s], axis=1))
        kv_s.append(cache_new.transpose(0, 4, 1, 2, 3)[None])

    kw_t, v_t, att = lax.optimization_barrier((kw_t, v_t, att))
    y_t, s_t = _rwkv_seq(kw_t, v_t)
    y_p = y_t[:, :, :BH].reshape(SEQ, HEAD_DIM, BATCH, RWKV_HEADS).transpose(2, 0, 3, 1).reshape(M_PROMPT, D_RWKV)
    wkv_p = s_t.reshape(K_HALF, HEAD_DIM, 2, BATCH, RWKV_HEADS).transpose(3, 4, 1, 2, 0)
    wkv_p = wkv_p.reshape(BATCH, RWKV_HEADS, HEAD_DIM, HEAD_DIM)
    y_s_t, wkv_s_t = _rwkv_step(kw_s_t, v_s_t, state_rwkv_wkv[0].transpose(1, 2, 3, 0))
    wkv_s = wkv_s_t.transpose(3, 0, 1, 2)
    y_r = jnp.concatenate([y_p, y_s_t.T], axis=0)

    x2 = _merge(x1, y_r, bonus, g_r, att, p_gate, b_gate, rwkv_ln_w, rwkv_ln_b,
                e_heads, w_branch_rwkv[0].astype(BF16), wba, w_out[0].astype(BF16))
    y_prompt, y_sample = _ffn(x2, ffn2_norm, ffn2_w_in[0].astype(BF16), ffn2_w_out[0].astype(BF16), split_out=True)

    shift_p = jnp.concatenate([p_rwkv[(b + 1) * SEQ - 1:(b + 1) * SEQ, :D_RWKV_PROJ] for b in range(BATCH)])[None]
    shift_s = p_rwkv[M_PROMPT:, :D_RWKV_PROJ][None]
    return (y_prompt.reshape(BATCH, SEQ, D_MODEL), y_sample.reshape(DEC_BATCH, 1, D_MODEL),
            shift_p, wkv_p[None], kv_p[0], kv_p[1], kv_p[2],
            shift_s, wkv_s[None], kv_s[0], kv_s[1], kv_s[2])
```

```python
import functools

import jax
import jax.numpy as jnp
from jax import lax
from jax.experimental import pallas as pl
from jax.experimental.pallas import tpu as pltpu

D_MODEL = 2048
BATCH = 4
SEQ = 2048
DEC_BATCH = 128
PAST_LEN = 2048
M_PROMPT = BATCH * SEQ
M_ALL = M_PROMPT + DEC_BATCH

RWKV_HEADS = 16
HEAD_DIM = 64
D_RWKV = RWKV_HEADS * HEAD_DIM
D_DECAY_LORA = 64
D_ICLR_LORA = 64
D_GATE_LORA = 160
D_RWKV_PROJ = 3 * D_RWKV + D_DECAY_LORA + D_ICLR_LORA + D_GATE_LORA
RWKV_PROJ_PAD = 3584
RWKV_LN_EPS = 64e-5

ATT_GROUPS = ((128, 1), (512, 4), (2048, 16))
HEADS_PER_GROUP = 4
ATT_HEADS = HEADS_PER_GROUP * len(ATT_GROUPS)
D_ATT = ATT_HEADS * HEAD_DIM
D_GROUP = HEADS_PER_GROUP * HEAD_DIM
N_KEYS_PAST = 128
ROPE_THETA = 10000.0
D_FF = 5632
NORM_EPS = 1e-6
NEG_INF = -1e30

LANES = 128
VMEM_LIMIT = 56 * 1024 * 1024

F32 = jnp.float32
BF16 = jnp.bfloat16


def _params(*sem):
    return pltpu.CompilerParams(dimension_semantics=sem, vmem_limit_bytes=VMEM_LIMIT)


def _const_spec(shape):
    nd = len(shape)
    return pl.BlockSpec(shape, lambda *_: (0,) * nd, pipeline_mode=pl.Buffered(1))


def _rms_to_bf16(x, g):
    ms = jnp.mean(x * x, axis=-1, keepdims=True)
    return (x * lax.rsqrt(ms + NORM_EPS) * g).astype(BF16)


def _split_dot(x, w):
    hi = x.astype(BF16)
    lo = (x - hi.astype(F32)).astype(BF16)
    return jnp.dot(hi, w, preferred_element_type=F32) + jnp.dot(lo, w, preferred_element_type=F32)


def _segsum(x, e):
    if isinstance(e, tuple):
        return _split_dot(_split_dot(x, e[0]), e[1])
    return _split_dot(x, e)


def _dot_split(x, w_hi, w_lo):
    hi = x.astype(BF16)
    lo = (x - hi.astype(F32)).astype(BF16)
    return (jnp.dot(hi, w_hi, preferred_element_type=F32) + jnp.dot(lo, w_hi, preferred_element_type=F32)
            + jnp.dot(hi, w_lo, preferred_element_type=F32))


FFN_TM = 640
FFN_TF = 512


FFN_SAMPLE_ROW0 = M_PROMPT % FFN_TM
assert FFN_TM - FFN_SAMPLE_ROW0 == DEC_BATCH


def _ffn_kernel(*refs, split_in, split_out):
    refs = list(refs)
    x_ref = refs.pop(0)
    xs_ref = refs.pop(0) if split_in else None
    g_ref, wg_ref, wu_ref, wo_ref = refs[:4]
    refs = refs[4:]
    gn_ref = refs.pop(0) if split_in else None
    o_ref = refs.pop(0)
    os_ref = refs.pop(0) if split_out else None
    hn_ref = refs.pop(0) if split_in else None
    h_ref = refs.pop(0)
    xin_ref = refs.pop(0) if split_in else x_ref
    j = pl.program_id(1)
    last_tile = pl.program_id(0) == pl.num_programs(0) - 1

    def start(src_ref):
        h_ref[...] = _rms_to_bf16(src_ref[...], g_ref[...])
        o_ref[...] = jnp.zeros_like(o_ref)

    def finish(src_ref):
        out = src_ref[...] + 0.5 * o_ref[...]
        o_ref[...] = out
        if split_in:
            hn_ref[...] = _rms_to_bf16(out, gn_ref[...])
        return out

    first_j = j == 0
    last_j = j == pl.num_programs(1) - 1
    if split_in:
        @pl.when(first_j & last_tile)
        def _():
            xin_ref[0:FFN_SAMPLE_ROW0] = x_ref[0:FFN_SAMPLE_ROW0]
            xin_ref[FFN_SAMPLE_ROW0:] = xs_ref[...]
            start(xin_ref)

        @pl.when(first_j & jnp.logical_not(last_tile))
        def _():
            start(x_ref)
    else:
        @pl.when(first_j)
        def _():
            start(x_ref)

    h = h_ref[...]
    gate = jnp.dot(h, wg_ref[...], preferred_element_type=F32)
    up = jnp.dot(h, wu_ref[...], preferred_element_type=F32)
    act = (gate * jax.nn.sigmoid(gate) * up).astype(BF16)
    o_ref[...] += jnp.dot(act, wo_ref[...], preferred_element_type=F32)

    if split_in:
        @pl.when(last_j & last_tile)
        def _():
            finish(xin_ref)

        @pl.when(last_j & jnp.logical_not(last_tile))
        def _():
            finish(x_ref)
    else:
        @pl.when(last_j)
        def _():
            out = finish(x_ref)
            if split_out:
                @pl.when(last_tile)
                def _():
                    os_ref[...] = out[FFN_SAMPLE_ROW0:]


def _ffn(x, g, w_in, w_out, x_sample=None, next_norm=None, split_out=False):
    nj = D_FF // FFN_TF
    split_in = x_sample is not None
    row = pl.BlockSpec((1, D_MODEL), lambda i, j: (0, 0))
    tile = pl.BlockSpec((FFN_TM, D_MODEL), lambda i, j: (i, 0))
    sample = pl.BlockSpec((DEC_BATCH, D_MODEL), lambda i, j: (0, 0))
    weights = [pl.BlockSpec((D_MODEL, FFN_TF), lambda i, j: (0, j)),
               pl.BlockSpec((D_MODEL, FFN_TF), lambda i, j: (0, j + nj)),
               pl.BlockSpec((FFN_TF, D_MODEL), lambda i, j: (j, 0))]
    in_specs = [tile] + ([sample] if split_in else []) + [row] + weights + ([row] if split_in else [])
    args = [x] + ([x_sample] if split_in else []) + [g, w_in, w_in, w_out] + ([next_norm] if split_in else [])
    scratch = [pltpu.VMEM((FFN_TM, D_MODEL), BF16)] + ([pltpu.VMEM((FFN_TM, D_MODEL), F32)] if split_in else [])
    if split_out:
        out_specs = [tile, sample]
        out_shape = [jax.ShapeDtypeStruct((M_PROMPT, D_MODEL), F32), jax.ShapeDtypeStruct((DEC_BATCH, D_MODEL), F32)]
    else:
        out_specs = [tile, tile]
        out_shape = [jax.ShapeDtypeStruct((M_ALL, D_MODEL), F32), jax.ShapeDtypeStruct((M_ALL, D_MODEL), BF16)]
    return pl.pallas_call(
        functools.partial(_ffn_kernel, split_in=split_in, split_out=split_out),
        grid=(M_ALL // FFN_TM, nj),
        in_specs=in_specs,
        out_specs=out_specs,
        out_shape=out_shape,
        scratch_shapes=scratch,
        compiler_params=_params("arbitrary" if split_out else "parallel", "arbitrary"),
        name="ffn",
    )(*args)


PROJ_TM = 1664


def _proj_kernel(h_ref, w_ref, o_ref):
    o_ref[...] = jnp.dot(h_ref[...], w_ref[...], preferred_element_type=F32).astype(o_ref.dtype)


def _proj(h, w, tn, name, out_dtype=F32):
    m, n = h.shape[0], w.shape[1]
    return pl.pallas_call(
        _proj_kernel,
        grid=(m // PROJ_TM, n // tn),
        in_specs=[
            pl.BlockSpec((PROJ_TM, D_MODEL), lambda i, j: (i, 0)),
            pl.BlockSpec((D_MODEL, tn), lambda i, j: (0, j)),
        ],
        out_specs=pl.BlockSpec((PROJ_TM, tn), lambda i, j: (i, j)),
        out_shape=jax.ShapeDtypeStruct((m, n), out_dtype),
        compiler_params=_params("parallel", "arbitrary"),
        name=name,
    )(h, w)


PREP_TM = 128
PREP_TILES_PER_SEQ = SEQ // PREP_TM
PREP_SAMPLE_TILE = M_PROMPT // PREP_TM
KW_R, KW_DECAY, KW_K, KW_KK, KW_B = range(5)


def _prep_tile(i):
    return jnp.where(i == 0, PREP_SAMPLE_TILE, i - 1)


def _rwkv_prep_kernel(p_ref, tail_ref, state_ref, mu_ref, w0_ref, a0_ref, kk_ref, ka_ref, rk_ref,
                      wl_hi_ref, wl_lo_ref, wg_hi_ref, wg_lo_ref, er_ref, eb_ref,
                      kw_ref, v_ref, kws_ref, vs_ref, bonus_ref, g_ref):
    i = pl.program_id(0)
    is_sample = i == 0
    p = p_ref[...]
    tail = jnp.where((i - 1) % PREP_TILES_PER_SEQ == 0, 0.0, tail_ref[7:8, :])
    row = lax.broadcasted_iota(jnp.int32, p.shape, 0)
    prev = jnp.where(row == 0, tail, pltpu.roll(p, 1, axis=0))
    prev = jnp.where(is_sample, state_ref[...], prev)
    pm = p + (prev - p) * mu_ref[...]

    r = pm[:, 0:D_RWKV]
    k = pm[:, D_RWKV:2 * D_RWKV]
    v = pm[:, 2 * D_RWKV:3 * D_RWKV]
    lora = pm[:, 3 * D_RWKV:]
    lane = lax.broadcasted_iota(jnp.int32, (p.shape[0], LANES), 1)
    x_wa = lora[:, 0:LANES]
    x_wa = jnp.where(lane < D_DECAY_LORA, jnp.tanh(x_wa), x_wa)
    x_g = jax.nn.sigmoid(lora[:, LANES:3 * LANES])
    wa = _dot_split(x_wa, wl_hi_ref[...], wl_lo_ref[...])
    g = _dot_split(x_g, wg_hi_ref[...], wg_lo_ref[...])

    z = -(w0_ref[...] + wa[:, 0:D_RWKV])
    softplus = jnp.maximum(z, 0.0) + jnp.log(1.0 + jnp.exp(-jnp.abs(z)))
    decay = jnp.exp(-jnp.exp(-softplus - 0.5))
    a = jax.nn.sigmoid(a0_ref[...] + wa[:, D_RWKV:])
    e = (er_ref[...], eb_ref[...])
    kk = k * kk_ref[...]
    kk = kk * lax.rsqrt(jnp.maximum(_segsum(kk * kk, e), 1e-24))
    k_hat = k * (1.0 + (a - 1.0) * ka_ref[...])
    kw = (r, decay, k_hat, kk, kk * a)
    bonus_ref[...] = (_segsum(r * k_hat * rk_ref[...], e) * v).astype(bonus_ref.dtype)
    g_ref[...] = g.astype(g_ref.dtype)

    @pl.when(i > 0)
    def _():
        for idx in range(5):
            kw_ref[idx] = kw[idx]
        v_ref[...] = v

    @pl.when(is_sample)
    def _():
        for idx in range(5):
            kws_ref[idx] = kw[idx].T
        vs_ref[...] = v.T


def _rwkv_prep(p_rwkv, state_shift, mu, w0, a0, k_k, k_a, r_k, wl_hi, wl_lo, wg_hi, wg_lo, e_heads):
    n_tiles = M_ALL // PREP_TM
    row = lambda n: pl.BlockSpec((1, n), lambda i: (0, 0))
    prompt_tile = lambda i: jnp.maximum(i - 1, 0)
    return pl.pallas_call(
        _rwkv_prep_kernel,
        grid=(n_tiles,),
        in_specs=[
            pl.BlockSpec((PREP_TM, RWKV_PROJ_PAD), lambda i: (_prep_tile(i), 0)),
            pl.BlockSpec((8, RWKV_PROJ_PAD), lambda i: (jnp.maximum((i - 1) * (PREP_TM // 8) - 1, 0), 0)),
            pl.BlockSpec((DEC_BATCH, RWKV_PROJ_PAD), lambda i: (0, 0)),
            row(RWKV_PROJ_PAD), row(D_RWKV), row(D_RWKV), row(D_RWKV), row(D_RWKV), row(D_RWKV),
            _const_spec(wl_hi.shape), _const_spec(wl_lo.shape),
            _const_spec(wg_hi.shape), _const_spec(wg_lo.shape),
            _const_spec(e_heads[0].shape), _const_spec(e_heads[1].shape),
        ],
        out_specs=[
            pl.BlockSpec((5, PREP_TM, D_RWKV), lambda i: (0, prompt_tile(i), 0)),
            pl.BlockSpec((PREP_TM, D_RWKV), lambda i: (prompt_tile(i), 0)),
            pl.BlockSpec((5, D_RWKV, DEC_BATCH), lambda i: (0, 0, 0)),
            pl.BlockSpec((D_RWKV, DEC_BATCH), lambda i: (0, 0)),
            pl.BlockSpec((PREP_TM, D_RWKV), lambda i: (_prep_tile(i), 0)),
            pl.BlockSpec((PREP_TM, D_RWKV), lambda i: (_prep_tile(i), 0)),
        ],
        out_shape=[
            jax.ShapeDtypeStruct((5, M_PROMPT, D_RWKV), F32),
            jax.ShapeDtypeStruct((M_PROMPT, D_RWKV), F32),
            jax.ShapeDtypeStruct((5, D_RWKV, DEC_BATCH), F32),
            jax.ShapeDtypeStruct((D_RWKV, DEC_BATCH), F32),
            jax.ShapeDtypeStruct((M_ALL, D_RWKV), BF16),
            jax.ShapeDtypeStruct((M_ALL, D_RWKV), BF16),
        ],
        compiler_params=_params("arbitrary"),
        name="rwkv_prep",
    )(p_rwkv, p_rwkv, state_shift, mu, w0, a0, k_k, k_a, r_k, wl_hi, wl_lo, wg_hi, wg_lo, *e_heads)


SEQ_TB = 64
K_HALF = HEAD_DIM // 2
BH = BATCH * RWKV_HEADS


def _rwkv_seq_kernel(kw_ref, v_ref, y_ref, sout_ref, s_ref):
    @pl.when(pl.program_id(0) == 0)
    def _():
        s_ref[...] = jnp.zeros_like(s_ref)

    sk = jnp.zeros((HEAD_DIM, LANES), F32)
    for kp in range(K_HALF):
        sk = sk + s_ref[kp] * kw_ref[0, KW_KK, pl.ds(kp, 1), :]

    n_vh = 4
    vh_rows = HEAD_DIM // n_vh

    def step(t, sk):
        t_next = jnp.minimum(t + 1, SEQ_TB - 1)
        sk_next = []
        for vh in range(n_vh):
            vs = pl.ds(vh * vh_rows, vh_rows)
            sk_h = sk[vh * vh_rows:(vh + 1) * vh_rows]
            sa = -(sk_h + pltpu.roll(sk_h, BH, axis=1))
            vt = v_ref[t, vs, :]
            y = jnp.zeros((vh_rows, LANES), F32)
            nxt = jnp.zeros((vh_rows, LANES), F32)
            for kp in range(K_HALF):
                s_new = (s_ref[kp, vs, :] * kw_ref[t, KW_DECAY, pl.ds(kp, 1), :]
                         + sa * kw_ref[t, KW_B, pl.ds(kp, 1), :]
                         + vt * kw_ref[t, KW_K, pl.ds(kp, 1), :])
                s_ref[kp, vs, :] = s_new
                y = y + s_new * kw_ref[t, KW_R, pl.ds(kp, 1), :]
                nxt = nxt + s_new * kw_ref[t_next, KW_KK, pl.ds(kp, 1), :]
            y_ref[t, vs, :] = y + pltpu.roll(y, BH, axis=1)
            sk_next.append(nxt)
        return jnp.concatenate(sk_next, axis=0)

    lax.fori_loop(0, SEQ_TB, step, sk)

    @pl.when(pl.program_id(0) == pl.num_programs(0) - 1)
    def _():
        sout_ref[...] = s_ref[...]


def _rwkv_seq(kw_t, v_t):
    return pl.pallas_call(
        _rwkv_seq_kernel,
        grid=(SEQ // SEQ_TB,),
        in_specs=[
            pl.BlockSpec((SEQ_TB, 5, K_HALF, LANES), lambda i: (i, 0, 0, 0)),
            pl.BlockSpec((SEQ_TB, HEAD_DIM, LANES), lambda i: (i, 0, 0)),
        ],
        out_specs=[
            pl.BlockSpec((SEQ_TB, HEAD_DIM, LANES), lambda i: (i, 0, 0)),
            pl.BlockSpec((K_HALF, HEAD_DIM, LANES), lambda i: (0, 0, 0)),
        ],
        out_shape=[
            jax.ShapeDtypeStruct((SEQ, HEAD_DIM, LANES), F32),
            jax.ShapeDtypeStruct((K_HALF, HEAD_DIM, LANES), F32),
        ],
        scratch_shapes=[pltpu.VMEM((K_HALF, HEAD_DIM, LANES), F32)],
        compiler_params=_params("arbitrary"),
        name="rwkv_seq",
    )(kw_t, v_t)


def _rwkv_step_kernel(kw_ref, v_ref, s_ref, y_ref, sout_ref):
    def body(vi, carry):
        s = s_ref[0, vi]
        sa = -jnp.sum(s * kw_ref[KW_KK], axis=0, keepdims=True)
        s_new = s * kw_ref[KW_DECAY] + sa * kw_ref[KW_B] + v_ref[pl.ds(vi, 1), :] * kw_ref[KW_K]
        sout_ref[0, vi] = s_new
        y_ref[pl.ds(vi, 1), :] = jnp.sum(s_new * kw_ref[KW_R], axis=0, keepdims=True)
        return carry

    lax.fori_loop(0, HEAD_DIM, body, 0)


def _rwkv_step(kw_t, v_t, state_t):
    return pl.pallas_call(
        _rwkv_step_kernel,
        grid=(RWKV_HEADS,),
        in_specs=[
            pl.BlockSpec((5, HEAD_DIM, DEC_BATCH), lambda h: (0, h, 0)),
            pl.BlockSpec((HEAD_DIM, DEC_BATCH), lambda h: (h, 0)),
            pl.BlockSpec((1, HEAD_DIM, HEAD_DIM, DEC_BATCH), lambda h: (h, 0, 0, 0)),
        ],
        out_specs=[
            pl.BlockSpec((HEAD_DIM, DEC_BATCH), lambda h: (h, 0)),
            pl.BlockSpec((1, HEAD_DIM, HEAD_DIM, DEC_BATCH), lambda h: (h, 0, 0, 0)),
        ],
        out_shape=[
            jax.ShapeDtypeStruct((D_RWKV, DEC_BATCH), F32),
            jax.ShapeDtypeStruct((RWKV_HEADS, HEAD_DIM, HEAD_DIM, DEC_BATCH), F32),
        ],
        compiler_params=_params("parallel"),
        name="rwkv_step",
    )(kw_t, v_t, state_t)


ATT_PREP_TM = 640


def _swap_halves(x):
    lane = lax.broadcasted_iota(jnp.int32, x.shape, 1)
    first = (lane % HEAD_DIM) < HEAD_DIM // 2
    return jnp.where(first, pltpu.roll(x, LANES - HEAD_DIM // 2, axis=1), pltpu.roll(x, HEAD_DIM // 2, axis=1))


PAIRS = D_GROUP // LANES
QKV_TILES = 3 * PAIRS


def _att_prep_kernel(p_ref, cos_ref, sin_ref, qg_ref, kg_ref, e_ref, o0_ref, o1_ref, o2_ref):
    cos = cos_ref[...]
    sin = sin_ref[...]
    e = e_ref[...]
    o_refs = (o0_ref, o1_ref, o2_ref)
    for part, g_ref in ((0, qg_ref), (1, kg_ref)):
        x = p_ref[:, part * D_ATT:(part + 1) * D_ATT]
        ms = _segsum(x * x, e) * (1.0 / HEAD_DIM)
        xn = x * lax.rsqrt(ms + NORM_EPS) * g_ref[...]
        for c in range(D_ATT // LANES):
            xc = xn[:, c * LANES:(c + 1) * LANES]
            o_refs[c // PAIRS][part * PAIRS + c % PAIRS] = xc * cos + _swap_halves(xc) * sin
    for c in range(D_ATT // LANES):
        o_refs[c // PAIRS][2 * PAIRS + c % PAIRS] = p_ref[:, 2 * D_ATT + c * LANES:2 * D_ATT + (c + 1) * LANES]


def _att_prep(p_att, cos, sin, q_gain, k_gain, e_att):
    out = jax.ShapeDtypeStruct((QKV_TILES, M_ALL, LANES), F32)
    return pl.pallas_call(
        _att_prep_kernel,
        grid=(M_ALL // ATT_PREP_TM,),
        in_specs=[
            pl.BlockSpec((ATT_PREP_TM, 3 * D_ATT), lambda i: (i, 0)),
            pl.BlockSpec((ATT_PREP_TM, LANES), lambda i: (i, 0)),
            pl.BlockSpec((ATT_PREP_TM, LANES), lambda i: (i, 0)),
            pl.BlockSpec((1, D_ATT), lambda i: (0, 0)),
            pl.BlockSpec((1, D_ATT), lambda i: (0, 0)),
            _const_spec(e_att.shape),
        ],
        out_specs=[pl.BlockSpec((QKV_TILES, ATT_PREP_TM, LANES), lambda i: (0, i, 0))] * 3,
        out_shape=[out] * 3,
        compiler_params=_params("parallel"),
        name="att_prep",
    )(p_att, cos, sin, q_gain, k_gain, e_att)


ATT_TQ = 128


def _att_prompt_kernel(qkv_ref, o_ref, *, dil):
    scale = HEAD_DIM ** -0.5
    ls = SEQ // dil

    def heads(part, rows):
        pairs = [qkv_ref[part * PAIRS + pr, rows, :] for pr in range(PAIRS)]
        return jnp.stack([x[:, h * HEAD_DIM:(h + 1) * HEAD_DIM] for x in pairs for h in range(LANES // HEAD_DIM)],
                         axis=0)

    for c in range(dil):
        for qi in range(ls // ATT_TQ):
            rows = pl.ds(c + dil * qi * ATT_TQ, ATT_TQ, stride=dil)
            q = heads(0, rows).astype(BF16)
            k = heads(1, rows)
            v = heads(2, rows)
            q_off = 0
            if qi > 0:
                prev = pl.ds(c + dil * (qi - 1) * ATT_TQ, ATT_TQ, stride=dil)
                k = jnp.concatenate([heads(1, prev), k], axis=1)
                v = jnp.concatenate([heads(2, prev), v], axis=1)
                q_off = ATT_TQ
            s = jnp.einsum("hqd,hkd->hqk", q, k.astype(BF16), preferred_element_type=F32) * scale
            dist = (lax.broadcasted_iota(jnp.int32, s.shape, 1) + q_off) - lax.broadcasted_iota(jnp.int32, s.shape, 2)
            s = jnp.where((dist >= 0) & (dist <= N_KEYS_PAST), s, NEG_INF)
            m = jnp.max(s, axis=-1, keepdims=True)
            p = jnp.exp(s - m)
            l = jnp.sum(p, axis=-1, keepdims=True)
            lse = jnp.broadcast_to(m + jnp.log(l), (HEADS_PER_GROUP, ATT_TQ, HEAD_DIM))
            o = jnp.einsum("hqk,hkd->hqd", (p / l).astype(BF16), v.astype(BF16), preferred_element_type=F32)
            for h in range(HEADS_PER_GROUP):
                o_ref[h, rows, :] = jnp.concatenate([o[h], lse[h]], axis=1)


def _att_prompt(qkv, dil):
    return pl.pallas_call(
        functools.partial(_att_prompt_kernel, dil=dil),
        grid=(BATCH,),
        in_specs=[pl.BlockSpec((QKV_TILES, SEQ, LANES), lambda b: (0, b, 0))],
        out_specs=pl.BlockSpec((HEADS_PER_GROUP, SEQ, LANES), lambda b: (0, b, 0)),
        out_shape=jax.ShapeDtypeStruct((HEADS_PER_GROUP, M_PROMPT, LANES), F32),
        compiler_params=_params("parallel"),
        name=f"att_prompt_d{dil}",
    )(qkv)


CACHE_BATCH_PER_STEP = {128: 16, 512: 8, 2048: 2}


def _cache_attn_kernel(qkv_ref, c_ref, o_ref, cout_ref, *, dil, nb):
    n_tiles = c_ref.shape[-1] // LANES
    scale = HEAD_DIM ** -0.5
    r_i = lax.broadcasted_iota(jnp.int32, (HEAD_DIM, HEAD_DIM), 0)
    c_i = lax.broadcasted_iota(jnp.int32, (HEAD_DIM, HEAD_DIM), 1)
    eye = (r_i == c_i).astype(F32)
    on_grid = (lax.broadcasted_iota(jnp.int32, (1, LANES), 1) % dil) == 0
    last_lane = lax.broadcasted_iota(jnp.int32, (HEAD_DIM, LANES), 1) == LANES - 1

    def to_col(rows):
        return jnp.sum(eye * rows, axis=2, keepdims=True)

    rows = qkv_ref[...]
    parts = []
    for h in range(HEADS_PER_GROUP):
        q = rows[:, :, h * HEAD_DIM:(h + 1) * HEAD_DIM]
        k_new = rows[:, :, D_GROUP + h * HEAD_DIM:D_GROUP + (h + 1) * HEAD_DIM]
        v_new = rows[:, :, 2 * D_GROUP + h * HEAD_DIM:2 * D_GROUP + (h + 1) * HEAD_DIM]
        q_col, k_col, v_col = to_col(q), to_col(k_new), to_col(v_new)
        s_tiles = [None] * n_tiles
        nxt = jnp.broadcast_to(k_col, (nb, HEAD_DIM, LANES))
        for j in reversed(range(n_tiles)):
            ts = slice(j * LANES, (j + 1) * LANES)
            kt = c_ref[:, 0, h, :, ts]
            s = jnp.sum(kt * q_col, axis=1, keepdims=True) * scale
            s_tiles[j] = jnp.where(on_grid, s, NEG_INF)
            rolled = pltpu.roll(kt, LANES - 1, axis=2)
            cout_ref[:, 0, h, :, ts] = jnp.where(last_lane, nxt, rolled)
            nxt = rolled
        s_new = jnp.sum(k_new * q, axis=2, keepdims=True) * scale
        m_row = s_tiles[0]
        for j in range(1, n_tiles):
            m_row = jnp.maximum(m_row, s_tiles[j])
        m = jnp.maximum(jnp.max(m_row, axis=2, keepdims=True), s_new)
        p_new = jnp.exp(s_new - m)
        l_row = jnp.zeros((nb, 1, LANES), F32)
        acc = jnp.zeros((nb, HEAD_DIM, LANES), F32)
        nxt = jnp.broadcast_to(v_col, (nb, HEAD_DIM, LANES))
        for j in reversed(range(n_tiles)):
            ts = slice(j * LANES, (j + 1) * LANES)
            p = jnp.exp(s_tiles[j] - m)
            l_row = l_row + p
            vt = c_ref[:, 1, h, :, ts]
            acc = acc + vt * p
            rolled = pltpu.roll(vt, LANES - 1, axis=2)
            cout_ref[:, 1, h, :, ts] = jnp.where(last_lane, nxt, rolled)
            nxt = rolled
        l = jnp.sum(l_row, axis=2, keepdims=True) + p_new
        o_col = (jnp.sum(acc, axis=2, keepdims=True) + p_new * v_col) / l
        parts += [jnp.sum(eye * o_col, axis=1, keepdims=True), jnp.broadcast_to(m + jnp.log(l), (nb, 1, HEAD_DIM))]
    o_ref[...] = jnp.concatenate(parts, axis=2)


def _cache_attn(qkv, cache_t, dil):
    window = cache_t.shape[-1]
    nb = CACHE_BATCH_PER_STEP[window]
    blk = (nb, 2, HEADS_PER_GROUP, HEAD_DIM, window)
    return pl.pallas_call(
        functools.partial(_cache_attn_kernel, dil=dil, nb=nb),
        grid=(DEC_BATCH // nb,),
        in_specs=[pl.BlockSpec((nb, 1, 3 * D_GROUP), lambda i: (i, 0, 0)),
                  pl.BlockSpec(blk, lambda i: (i, 0, 0, 0, 0))],
        out_specs=[pl.BlockSpec((nb, 1, HEADS_PER_GROUP * LANES), lambda i: (i, 0, 0)),
                   pl.BlockSpec(blk, lambda i: (i, 0, 0, 0, 0))],
        out_shape=[jax.ShapeDtypeStruct((DEC_BATCH, 1, HEADS_PER_GROUP * LANES), F32),
                   jax.ShapeDtypeStruct(cache_t.shape, F32)],
        compiler_params=_params("parallel"),
        name=f"cache_attn_w{window}",
    )(qkv, cache_t)


MERGE_TM = 208


def _merge_kernel(x_ref, y_ref, bonus_ref, g_ref, a0_ref, a1_ref, a2_ref, pg_ref, bg_ref,
                  lnw_ref, lnb_ref, er_ref, eb_ref, wbr_ref, wba0_ref, wba1_ref, wba2_ref, wo_ref, o_ref):
    e = (er_ref[...], eb_ref[...])
    inv_n = 1.0 / HEAD_DIM
    y = y_ref[...]
    d = y - _segsum(y, e) * inv_n
    var = _segsum(d * d, e) * inv_n
    yn = d * lax.rsqrt(var + RWKV_LN_EPS) * lnw_ref[...] + lnb_ref[...]
    o_a = ((yn + bonus_ref[...].astype(F32)) * g_ref[...].astype(F32)).astype(BF16)
    br_a = jnp.dot(o_a, wbr_ref[...], preferred_element_type=F32)
    a = (a0_ref[...], a1_ref[...], a2_ref[...])
    m = jnp.maximum(jnp.maximum(a[0], a[1]), a[2])
    ex = [jnp.exp(t - m) for t in a]
    inv = 1.0 / (ex[0] + ex[1] + ex[2])
    br_b = jnp.zeros_like(br_a)
    for grp, wb_ref in enumerate((wba0_ref, wba1_ref, wba2_ref)):
        alpha = ex[grp] * inv
        parts = [a[grp][h] * pltpu.roll(alpha[h], HEAD_DIM, axis=1) for h in range(HEADS_PER_GROUP)]
        xg = jnp.concatenate(parts, axis=1).astype(BF16)
        br_b = br_b + jnp.dot(xg, wb_ref[...], preferred_element_type=F32)
    gates = jax.nn.sigmoid(pg_ref[...].astype(F32) + bg_ref[...])
    merged = (gates[:, 0:D_MODEL] * br_a + gates[:, D_MODEL:] * br_b).astype(BF16)
    o_ref[...] = x_ref[...] + jnp.dot(merged, wo_ref[...], preferred_element_type=F32)


def _merge(x, y, bonus, g, att, p_gate, b_gate, ln_w, ln_b, e_heads, w_br, w_ba, w_o):
    tile = lambda n: pl.BlockSpec((MERGE_TM, n), lambda i: (i, 0))
    row = lambda n: pl.BlockSpec((1, n), lambda i: (0, 0))
    att_tile = pl.BlockSpec((HEADS_PER_GROUP, MERGE_TM, LANES), lambda i: (0, i, 0))
    return pl.pallas_call(
        _merge_kernel,
        grid=(M_ALL // MERGE_TM,),
        in_specs=[tile(D_MODEL), tile(D_RWKV), tile(D_RWKV), tile(D_RWKV),
                  att_tile, att_tile, att_tile, tile(2 * D_MODEL), row(2 * D_MODEL),
                  row(D_RWKV), row(D_RWKV), _const_spec(e_heads[0].shape), _const_spec(e_heads[1].shape),
                  _const_spec(w_br.shape),
                  _const_spec(w_ba[0].shape), _const_spec(w_ba[1].shape), _const_spec(w_ba[2].shape),
                  _const_spec(w_o.shape)],
        out_specs=tile(D_MODEL),
        out_shape=jax.ShapeDtypeStruct((M_ALL, D_MODEL), F32),
        compiler_params=_params("parallel"),
        name="merge",
    )(x, y, bonus, g, att[0], att[1], att[2], p_gate, b_gate, ln_w, ln_b, *e_heads, w_br,
      w_ba[0], w_ba[1], w_ba[2], w_o)


def _split_bf16(w):
    hi = w.astype(BF16)
    return hi, (w - hi.astype(F32)).astype(BF16)


def _head_ones(n_heads):
    reduce = (jnp.arange(n_heads * HEAD_DIM)[:, None] // HEAD_DIM == jnp.arange(LANES)[None, :]).astype(BF16)
    return reduce, reduce.T


def _rotary_tables():
    half = HEAD_DIM // 2
    inv_freq = ROPE_THETA ** (-jnp.arange(half, dtype=F32) / half)
    pos = jnp.concatenate([jnp.tile(jnp.arange(SEQ), BATCH), jnp.full((DEC_BATCH,), PAST_LEN)]).astype(F32)
    ang = pos[:, None] * inv_freq[None, :]
    cos, sin = jnp.cos(ang), jnp.sin(ang)
    cos = jnp.concatenate([cos, cos] * (LANES // HEAD_DIM), axis=1)
    sin = jnp.concatenate([-sin, sin] * (LANES // HEAD_DIM), axis=1)
    return cos, sin


def kernel(x_prompt, x_sample, state_rwkv_shift, state_rwkv_wkv, cache_att_w128, cache_att_w512, cache_att_w2048,
           ffn1_norm, ffn1_w_in, ffn1_w_out, mix_norm, w_in, b_gate, rwkv_mu, rwkv_w0, rwkv_w2, rwkv_a0, rwkv_a2,
           rwkv_g2, rwkv_k_k, rwkv_k_a, rwkv_r_k, rwkv_ln_w, rwkv_ln_b, w_branch_rwkv, attn_q_norm, attn_k_norm,
           w_branch_attn, w_out, ffn2_norm, ffn2_w_in, ffn2_w_out):
    caches = (cache_att_w128, cache_att_w512, cache_att_w2048)

    w_mix = w_in[0]
    pad = RWKV_PROJ_PAD - D_RWKV_PROJ
    w_rwkv = jnp.pad(w_mix[:, :D_RWKV_PROJ], ((0, 0), (0, pad))).astype(BF16)
    w_att = w_mix[:, D_RWKV_PROJ:D_RWKV_PROJ + 3 * D_ATT].astype(BF16)
    w_gate = w_mix[:, D_RWKV_PROJ + 3 * D_ATT:].astype(BF16)
    mu = jnp.pad(rwkv_mu, ((0, 0), (0, pad)))
    state_shift = jnp.pad(state_rwkv_shift[0], ((0, 0), (0, pad)))
    zeros = jnp.zeros((D_DECAY_LORA, D_RWKV), F32)
    w_lora = jnp.concatenate([jnp.concatenate([rwkv_w2[0], zeros], axis=1),
                              jnp.concatenate([zeros, rwkv_a2[0]], axis=1)], axis=0)
    w_glora = jnp.pad(rwkv_g2[0], ((0, 2 * LANES - D_GATE_LORA), (0, 0)))
    wl_hi, wl_lo = _split_bf16(w_lora)
    wg_hi, wg_lo = _split_bf16(w_glora)
    e_heads = _head_ones(RWKV_HEADS)
    e_att = jnp.kron(jnp.eye(ATT_HEADS, dtype=F32), jnp.ones((HEAD_DIM, HEAD_DIM), F32)).astype(BF16)
    cos, sin = _rotary_tables()
    q_gain = jnp.tile(attn_q_norm, (1, ATT_HEADS))
    k_gain = jnp.tile(attn_k_norm, (1, ATT_HEADS))
    wba = w_branch_attn[0].reshape(len(ATT_GROUPS), HEADS_PER_GROUP, HEAD_DIM, D_MODEL)
    wba = jnp.pad(wba, ((0, 0), (0, 0), (0, LANES - HEAD_DIM), (0, 0))).astype(BF16)
    wba = wba.reshape(len(ATT_GROUPS), HEADS_PER_GROUP * LANES, D_MODEL)

    x1, h_mix = _ffn(x_prompt.reshape(M_PROMPT, D_MODEL), ffn1_norm, ffn1_w_in[0].astype(BF16),
                     ffn1_w_out[0].astype(BF16), x_sample=x_sample.reshape(DEC_BATCH, D_MODEL), next_norm=mix_norm)

    p_rwkv = _proj(h_mix, w_rwkv, 512, "proj_rwkv")
    kw, v_r, kw_s_t, v_s_t, bonus, g_r = _rwkv_prep(p_rwkv, state_shift, mu, rwkv_w0, rwkv_a0, rwkv_k_k, rwkv_k_a,
                                                    rwkv_r_k.reshape(1, D_RWKV), wl_hi, wl_lo, wg_hi, wg_lo, e_heads)
    kw, v_r, h_att = lax.optimization_barrier((kw, v_r, h_mix))
    kw_cm = jnp.swapaxes(kw.reshape(5, BATCH, SEQ, D_RWKV), 2, 3)
    v_cm = jnp.swapaxes(v_r.reshape(BATCH, SEQ, D_RWKV), 1, 2)

    p_att = _proj(h_att, w_att, 768, "proj_att")
    p_gate = _proj(h_att, w_gate, 512, "proj_gate", out_dtype=BF16)
    kw_cm, v_cm, p_att, p_gate = lax.optimization_barrier((kw_cm, v_cm, p_att, p_gate))
    v_rows = v_cm.reshape(BATCH, RWKV_HEADS, HEAD_DIM, SEQ).transpose(2, 0, 1, 3)
    v_rows = jnp.broadcast_to(v_rows[:, None], (HEAD_DIM, 2, BATCH, RWKV_HEADS, SEQ))
    kw_rows = kw_cm.reshape(5, BATCH, RWKV_HEADS, 2, K_HALF, SEQ).transpose(0, 4, 3, 1, 2, 5)
    qkv = _att_prep(p_att, cos, sin, q_gain, k_gain, e_att)
    kw_rows, v_rows, qkv = lax.optimization_barrier(
        (kw_rows.reshape(5 * K_HALF * LANES, SEQ), v_rows.reshape(HEAD_DIM * LANES, SEQ), qkv))
    v_t = v_rows.T.reshape(SEQ, HEAD_DIM, LANES)
    kw_t = kw_rows.T.reshape(SEQ, 5, K_HALF, LANES)
    att, kv_p, kv_s = [], [], []
    for g, (window, dil) in enumerate(ATT_GROUPS):
        att_p = _att_prompt(qkv[g], dil)
        keep = min(window, SEQ)
        kv = qkv[g][PAIRS:, :M_PROMPT].reshape(2, PAIRS, BATCH, SEQ, LANES // HEAD_DIM, HEAD_DIM)
        kv = kv.transpose(2, 3, 0, 1, 4, 5).reshape(BATCH, SEQ, 2, HEADS_PER_GROUP, HEAD_DIM)
        kv_p.append(kv[:, SEQ - keep:][None])
        qkv_s = qkv[g][:, M_PROMPT:].transpose(1, 0, 2).reshape(DEC_BATCH, 1, 3 * D_GROUP)
        o_s, cache_new = _cache_attn(qkv_s, caches[g][0].transpose(0, 2, 3, 4, 1), dil)
        att_s = o_s.reshape(DEC_BATCH, HEADS_PER_GROUP, LANES).transpose(1, 0, 2)
        att.append(jnp.concatenate([att_p, att_s], axis=1))
        kv_s.append(cache_new.transpose(0, 4, 1, 2, 3)[None])

    kw_t, v_t, att = lax.optimization_barrier((kw_t, v_t, att))
    y_t, s_t = _rwkv_seq(kw_t, v_t)
    y_p = y_t[:, :, :BH].reshape(SEQ, HEAD_DIM, BATCH, RWKV_HEADS).transpose(2, 0, 3, 1).reshape(M_PROMPT, D_RWKV)
    wkv_p = s_t.reshape(K_HALF, HEAD_DIM, 2, BATCH, RWKV_HEADS).transpose(3, 4, 1, 2, 0)
    wkv_p = wkv_p.reshape(BATCH, RWKV_HEADS, HEAD_DIM, HEAD_DIM)
    y_s_t, wkv_s_t = _rwkv_step(kw_s_t, v_s_t, state_rwkv_wkv[0].transpose(1, 2, 3, 0))
    wkv_s = wkv_s_t.transpose(3, 0, 1, 2)
    y_r = jnp.concatenate([y_p, y_s_t.T], axis=0)

    x2 = _merge(x1, y_r, bonus, g_r, att, p_gate, b_gate, rwkv_ln_w, rwkv_ln_b,
                e_heads, w_branch_rwkv[0].astype(BF16), wba, w_out[0].astype(BF16))
    y_prompt, y_sample = _ffn(x2, ffn2_norm, ffn2_w_in[0].astype(BF16), ffn2_w_out[0].astype(BF16), split_out=True)

    shift_p = jnp.concatenate([p_rwkv[(b + 1) * SEQ - 1:(b + 1) * SEQ, :D_RWKV_PROJ] for b in range(BATCH)])[None]
    shift_s = p_rwkv[M_PROMPT:, :D_RWKV_PROJ][None]
    return (y_prompt.reshape(BATCH, SEQ, D_MODEL), y_sample.reshape(DEC_BATCH, 1, D_MODEL),
            shift_p, wkv_p[None], kv_p[0], kv_p[1], kv_p[2],
            shift_s, wkv_s[None], kv_s[0], kv_s[1], kv_s[2])
```

```python
import functools

import jax
import jax.numpy as jnp
from jax import lax
from jax.experimental import pallas as pl
from jax.experimental.pallas import tpu as pltpu

D_MODEL = 2048
BATCH = 4
SEQ = 2048
DEC_BATCH = 128
PAST_LEN = 2048
M_PROMPT = BATCH * SEQ
M_ALL = M_PROMPT + DEC_BATCH

RWKV_HEADS = 16
HEAD_DIM = 64
D_RWKV = RWKV_HEADS * HEAD_DIM
D_DECAY_LORA = 64
D_ICLR_LORA = 64
D_GATE_LORA = 160
D_RWKV_PROJ = 3 * D_RWKV + D_DECAY_LORA + D_ICLR_LORA + D_GATE_LORA
RWKV_PROJ_PAD = 3584
RWKV_LN_EPS = 64e-5

ATT_GROUPS = ((128, 1), (512, 4), (2048, 16))
HEADS_PER_GROUP = 4
ATT_HEADS = HEADS_PER_GROUP * len(ATT_GROUPS)
D_ATT = ATT_HEADS * HEAD_DIM
D_GROUP = HEADS_PER_GROUP * HEAD_DIM
N_KEYS_PAST = 128
ROPE_THETA = 10000.0
D_FF = 5632
NORM_EPS = 1e-6
NEG_INF = -1e30

LANES = 128
VMEM_LIMIT = 56 * 1024 * 1024

F32 = jnp.float32
BF16 = jnp.bfloat16


def _params(*sem):
    return pltpu.CompilerParams(dimension_semantics=sem, vmem_limit_bytes=VMEM_LIMIT)


def _const_spec(shape):
    nd = len(shape)
    return pl.BlockSpec(shape, lambda *_: (0,) * nd, pipeline_mode=pl.Buffered(1))


def _rms_to_bf16(x, g):
    ms = jnp.mean(x * x, axis=-1, keepdims=True)
    return (x * lax.rsqrt(ms + NORM_EPS) * g).astype(BF16)


def _split_dot(x, w):
    hi = x.astype(BF16)
    lo = (x - hi.astype(F32)).astype(BF16)
    return jnp.dot(hi, w, preferred_element_type=F32) + jnp.dot(lo, w, preferred_element_type=F32)


def _segsum(x, e):
    if isinstance(e, tuple):
        return _split_dot(_split_dot(x, e[0]), e[1])
    return _split_dot(x, e)


def _dot_split(x, w_hi, w_lo):
    hi = x.astype(BF16)
    lo = (x - hi.astype(F32)).astype(BF16)
    return (jnp.dot(hi, w_hi, preferred_element_type=F32) + jnp.dot(lo, w_hi, preferred_element_type=F32)
            + jnp.dot(hi, w_lo, preferred_element_type=F32))


FFN_TM = 640
FFN_TF = 512


FFN_SAMPLE_ROW0 = M_PROMPT % FFN_TM
assert FFN_TM - FFN_SAMPLE_ROW0 == DEC_BATCH


def _ffn_kernel(*refs, split_in, split_out):
    refs = list(refs)
    x_ref = refs.pop(0)
    xs_ref = refs.pop(0) if split_in else None
    g_ref, wg_ref, wu_ref, wo_ref = refs[:4]
    refs = refs[4:]
    gn_ref = refs.pop(0) if split_in else None
    o_ref = refs.pop(0)
    os_ref = refs.pop(0) if split_out else None
    hn_ref = refs.pop(0) if split_in else None
    h_ref = refs.pop(0)
    xin_ref = refs.pop(0) if split_in else x_ref
    j = pl.program_id(1)
    last_tile = pl.program_id(0) == pl.num_programs(0) - 1

    def start(src_ref):
        h_ref[...] = _rms_to_bf16(src_ref[...], g_ref[...])
        o_ref[...] = jnp.zeros_like(o_ref)

    def finish(src_ref):
        out = src_ref[...] + 0.5 * o_ref[...]
        o_ref[...] = out
        if split_in:
            hn_ref[...] = _rms_to_bf16(out, gn_ref[...])
        return out

    first_j = j == 0
    last_j = j == pl.num_programs(1) - 1
    if split_in:
        @pl.when(first_j & last_tile)
        def _():
            xin_ref[0:FFN_SAMPLE_ROW0] = x_ref[0:FFN_SAMPLE_ROW0]
            xin_ref[FFN_SAMPLE_ROW0:] = xs_ref[...]
            start(xin_ref)

        @pl.when(first_j & jnp.logical_not(last_tile))
        def _():
            start(x_ref)
    else:
        @pl.when(first_j)
        def _():
            start(x_ref)

    h = h_ref[...]
    gate = jnp.dot(h, wg_ref[...], preferred_element_type=F32)
    up = jnp.dot(h, wu_ref[...], preferred_element_type=F32)
    act = (gate * jax.nn.sigmoid(gate) * up).astype(BF16)
    o_ref[...] += jnp.dot(act, wo_ref[...], preferred_element_type=F32)

    if split_in:
        @pl.when(last_j & last_tile)
        def _():
            finish(xin_ref)

        @pl.when(last_j & jnp.logical_not(last_tile))
        def _():
            finish(x_ref)
    else:
        @pl.when(last_j)
        def _():
            out = finish(x_ref)
            if split_out:
                @pl.when(last_tile)
                def _():
                    os_ref[...] = out[FFN_SAMPLE_ROW0:]


def _ffn(x, g, w_in, w_out, x_sample=None, next_norm=None, split_out=False):
    nj = D_FF // FFN_TF
    split_in = x_sample is not None
    row = pl.BlockSpec((1, D_MODEL), lambda i, j: (0, 0))
    tile = pl.BlockSpec((FFN_TM, D_MODEL), lambda i, j: (i, 0))
    sample = pl.BlockSpec((DEC_BATCH, D_MODEL), lambda i, j: (0, 0))
    weights = [pl.BlockSpec((D_MODEL, FFN_TF), lambda i, j: (0, j)),
               pl.BlockSpec((D_MODEL, FFN_TF), lambda i, j: (0, j + nj)),
               pl.BlockSpec((FFN_TF, D_MODEL), lambda i, j: (j, 0))]
    in_specs = [tile] + ([sample] if split_in else []) + [row] + weights + ([row] if split_in else [])
    args = [x] + ([x_sample] if split_in else []) + [g, w_in, w_in, w_out] + ([next_norm] if split_in else [])
    scratch = [pltpu.VMEM((FFN_TM, D_MODEL), BF16)] + ([pltpu.VMEM((FFN_TM, D_MODEL), F32)] if split_in else [])
    if split_out:
        out_specs = [tile, sample]
        out_shape = [jax.ShapeDtypeStruct((M_PROMPT, D_MODEL), F32), jax.ShapeDtypeStruct((DEC_BATCH, D_MODEL), F32)]
    else:
        out_specs = [tile, tile]
        out_shape = [jax.ShapeDtypeStruct((M_ALL, D_MODEL), F32), jax.ShapeDtypeStruct((M_ALL, D_MODEL), BF16)]
    return pl.pallas_call(
        functools.partial(_ffn_kernel, split_in=split_in, split_out=split_out),
        grid=(M_ALL // FFN_TM, nj),
        in_specs=in_specs,
        out_specs=out_specs,
        out_shape=out_shape,
        scratch_shapes=scratch,
        compiler_params=_params("arbitrary" if split_out else "parallel", "arbitrary"),
        name="ffn",
    )(*args)


PROJ_TM = 1664


def _proj_kernel(h_ref, w_ref, o_ref):
    o_ref[...] = jnp.dot(h_ref[...], w_ref[...], preferred_element_type=F32).astype(o_ref.dtype)


def _proj(h, w, tn, name, out_dtype=F32):
    m, n = h.shape[0], w.shape[1]
    return pl.pallas_call(
        _proj_kernel,
        grid=(m // PROJ_TM, n // tn),
        in_specs=[
            pl.BlockSpec((PROJ_TM, D_MODEL), lambda i, j: (i, 0)),
            pl.BlockSpec((D_MODEL, tn), lambda i, j: (0, j)),
        ],
        out_specs=pl.BlockSpec((PROJ_TM, tn), lambda i, j: (i, j)),
        out_shape=jax.ShapeDtypeStruct((m, n), out_dtype),
        compiler_params=_params("parallel", "arbitrary"),
        name=name,
    )(h, w)


PREP_TM = 128
PREP_TILES_PER_SEQ = SEQ // PREP_TM
PREP_SAMPLE_TILE = M_PROMPT // PREP_TM
KW_R, KW_DECAY, KW_K, KW_KK, KW_B = range(5)


def _prep_tile(i):
    return jnp.where(i == 0, PREP_SAMPLE_TILE, i - 1)


def _rwkv_prep_kernel(p_ref, tail_ref, state_ref, mu_ref, w0_ref, a0_ref, kk_ref, ka_ref, rk_ref,
                      wl_hi_ref, wl_lo_ref, wg_hi_ref, wg_lo_ref, er_ref, eb_ref,
                      kw_ref, v_ref, kws_ref, vs_ref, bonus_ref, g_ref):
    i = pl.program_id(0)
    is_sample = i == 0
    p = p_ref[...]
    tail = jnp.where((i - 1) % PREP_TILES_PER_SEQ == 0, 0.0, tail_ref[7:8, :])
    row = lax.broadcasted_iota(jnp.int32, p.shape, 0)
    prev = jnp.where(row == 0, tail, pltpu.roll(p, 1, axis=0))
    prev = jnp.where(is_sample, state_ref[...], prev)
    pm = p + (prev - p) * mu_ref[...]

    r = pm[:, 0:D_RWKV]
    k = pm[:, D_RWKV:2 * D_RWKV]
    v = pm[:, 2 * D_RWKV:3 * D_RWKV]
    lora = pm[:, 3 * D_RWKV:]
    lane = lax.broadcasted_iota(jnp.int32, (p.shape[0], LANES), 1)
    x_wa = lora[:, 0:LANES]
    x_wa = jnp.where(lane < D_DECAY_LORA, jnp.tanh(x_wa), x_wa)
    x_g = jax.nn.sigmoid(lora[:, LANES:3 * LANES])
    wa = _dot_split(x_wa, wl_hi_ref[...], wl_lo_ref[...])
    g = _dot_split(x_g, wg_hi_ref[...], wg_lo_ref[...])

    z = -(w0_ref[...] + wa[:, 0:D_RWKV])
    softplus = jnp.maximum(z, 0.0) + jnp.log(1.0 + jnp.exp(-jnp.abs(z)))
    decay = jnp.exp(-jnp.exp(-softplus - 0.5))
    a = jax.nn.sigmoid(a0_ref[...] + wa[:, D_RWKV:])
    e = (er_ref[...], eb_ref[...])
    kk = k * kk_ref[...]
    kk = kk * lax.rsqrt(jnp.maximum(_segsum(kk * kk, e), 1e-24))
    k_hat = k * (1.0 + (a - 1.0) * ka_ref[...])
    kw = (r, decay, k_hat, kk, kk * a)
    bonus_ref[...] = (_segsum(r * k_hat * rk_ref[...], e) * v).astype(bonus_ref.dtype)
    g_ref[...] = g.astype(g_ref.dtype)

    @pl.when(i > 0)
    def _():
        for idx in range(5):
            kw_ref[idx] = kw[idx]
        v_ref[...] = v

    @pl.when(is_sample)
    def _():
        for idx in range(5):
            kws_ref[idx] = kw[idx].T
        vs_ref[...] = v.T


def _rwkv_prep(p_rwkv, state_shift, mu, w0, a0, k_k, k_a, r_k, wl_hi, wl_lo, wg_hi, wg_lo, e_heads):
    n_tiles = M_ALL // PREP_TM
    row = lambda n: pl.BlockSpec((1, n), lambda i: (0, 0))
    prompt_tile = lambda i: jnp.maximum(i - 1, 0)
    return pl.pallas_call(
        _rwkv_prep_kernel,
        grid=(n_tiles,),
        in_specs=[
            pl.BlockSpec((PREP_TM, RWKV_PROJ_PAD), lambda i: (_prep_tile(i), 0)),
            pl.BlockSpec((8, RWKV_PROJ_PAD), lambda i: (jnp.maximum((i - 1) * (PREP_TM // 8) - 1, 0), 0)),
            pl.BlockSpec((DEC_BATCH, RWKV_PROJ_PAD), lambda i: (0, 0)),
            row(RWKV_PROJ_PAD), row(D_RWKV), row(D_RWKV), row(D_RWKV), row(D_RWKV), row(D_RWKV),
            _const_spec(wl_hi.shape), _const_spec(wl_lo.shape),
            _const_spec(wg_hi.shape), _const_spec(wg_lo.shape),
            _const_spec(e_heads[0].shape), _const_spec(e_heads[1].shape),
        ],
        out_specs=[
            pl.BlockSpec((5, PREP_TM, D_RWKV), lambda i: (0, prompt_tile(i), 0)),
            pl.BlockSpec((PREP_TM, D_RWKV), lambda i: (prompt_tile(i), 0)),
            pl.BlockSpec((5, D_RWKV, DEC_BATCH), lambda i: (0, 0, 0)),
            pl.BlockSpec((D_RWKV, DEC_BATCH), lambda i: (0, 0)),
            pl.BlockSpec((PREP_TM, D_RWKV), lambda i: (_prep_tile(i), 0)),
            pl.BlockSpec((PREP_TM, D_RWKV), lambda i: (_prep_tile(i), 0)),
        ],
        out_shape=[
            jax.ShapeDtypeStruct((5, M_PROMPT, D_RWKV), F32),
            jax.ShapeDtypeStruct((M_PROMPT, D_RWKV), F32),
            jax.ShapeDtypeStruct((5, D_RWKV, DEC_BATCH), F32),
            jax.ShapeDtypeStruct((D_RWKV, DEC_BATCH), F32),
            jax.ShapeDtypeStruct((M_ALL, D_RWKV), BF16),
            jax.ShapeDtypeStruct((M_ALL, D_RWKV), BF16),
        ],
        compiler_params=_params("arbitrary"),
        name="rwkv_prep",
    )(p_rwkv, p_rwkv, state_shift, mu, w0, a0, k_k, k_a, r_k, wl_hi, wl_lo, wg_hi, wg_lo, *e_heads)


SEQ_TB = 64
K_HALF = HEAD_DIM // 2
BH = BATCH * RWKV_HEADS


def _rwkv_seq_kernel(kw_ref, v_ref, y_ref, sout_ref, s_ref):
    @pl.when(pl.program_id(0) == 0)
    def _():
        s_ref[...] = jnp.zeros_like(s_ref)

    sk = jnp.zeros((HEAD_DIM, LANES), F32)
    for kp in range(K_HALF):
        sk = sk + s_ref[kp] * kw_ref[0, KW_KK, pl.ds(kp, 1), :]

    n_vh = 4
    vh_rows = HEAD_DIM // n_vh

    def step(t, sk):
        t_next = jnp.minimum(t + 1, SEQ_TB - 1)
        sk_next = []
        for vh in range(n_vh):
            vs = pl.ds(vh * vh_rows, vh_rows)
            sk_h = sk[vh * vh_rows:(vh + 1) * vh_rows]
            sa = -(sk_h + pltpu.roll(sk_h, BH, axis=1))
            vt = v_ref[t, vs, :]
            y = jnp.zeros((vh_rows, LANES), F32)
            nxt = jnp.zeros((vh_rows, LANES), F32)
            for kp in range(K_HALF):
                s_new = (s_ref[kp, vs, :] * kw_ref[t, KW_DECAY, pl.ds(kp, 1), :]
                         + sa * kw_ref[t, KW_B, pl.ds(kp, 1), :]
                         + vt * kw_ref[t, KW_K, pl.ds(kp, 1), :])
                s_ref[kp, vs, :] = s_new
                y = y + s_new * kw_ref[t, KW_R, pl.ds(kp, 1), :]
                nxt = nxt + s_new * kw_ref[t_next, KW_KK, pl.ds(kp, 1), :]
            y_ref[t, vs, :] = y + pltpu.roll(y, BH, axis=1)
            sk_next.append(nxt)
        return jnp.concatenate(sk_next, axis=0)

    lax.fori_loop(0, SEQ_TB, step, sk)

    @pl.when(pl.program_id(0) == pl.num_programs(0) - 1)
    def _():
        sout_ref[...] = s_ref[...]


def _rwkv_seq(kw_t, v_t):
    return pl.pallas_call(
        _rwkv_seq_kernel,
        grid=(SEQ // SEQ_TB,),
        in_specs=[
            pl.BlockSpec((SEQ_TB, 5, K_HALF, LANES), lambda i: (i, 0, 0, 0)),
            pl.BlockSpec((SEQ_TB, HEAD_DIM, LANES), lambda i: (i, 0, 0)),
        ],
        out_specs=[
            pl.BlockSpec((SEQ_TB, HEAD_DIM, LANES), lambda i: (i, 0, 0)),
            pl.BlockSpec((K_HALF, HEAD_DIM, LANES), lambda i: (0, 0, 0)),
        ],
        out_shape=[
            jax.ShapeDtypeStruct((SEQ, HEAD_DIM, LANES), F32),
            jax.ShapeDtypeStruct((K_HALF, HEAD_DIM, LANES), F32),
        ],
        scratch_shapes=[pltpu.VMEM((K_HALF, HEAD_DIM, LANES), F32)],
        compiler_params=_params("arbitrary"),
        name="rwkv_seq",
    )(kw_t, v_t)


def _rwkv_step_kernel(kw_ref, v_ref, s_ref, y_ref, sout_ref):
    def body(vi, carry):
        s = s_ref[0, vi]
        sa = -jnp.sum(s * kw_ref[KW_KK], axis=0, keepdims=True)
        s_new = s * kw_ref[KW_DECAY] + sa * kw_ref[KW_B] + v_ref[pl.ds(vi, 1), :] * kw_ref[KW_K]
        sout_ref[0, vi] = s_new
        y_ref[pl.ds(vi, 1), :] = jnp.sum(s_new * kw_ref[KW_R], axis=0, keepdims=True)
        return carry

    lax.fori_loop(0, HEAD_DIM, body, 0)


def _rwkv_step(kw_t, v_t, state_t):
    return pl.pallas_call(
        _rwkv_step_kernel,
        grid=(RWKV_HEADS,),
        in_specs=[
            pl.BlockSpec((5, HEAD_DIM, DEC_BATCH), lambda h: (0, h, 0)),
            pl.BlockSpec((HEAD_DIM, DEC_BATCH), lambda h: (h, 0)),
            pl.BlockSpec((1, HEAD_DIM, HEAD_DIM, DEC_BATCH), lambda h: (h, 0, 0, 0)),
        ],
        out_specs=[
            pl.BlockSpec((HEAD_DIM, DEC_BATCH), lambda h: (h, 0)),
            pl.BlockSpec((1, HEAD_DIM, HEAD_DIM, DEC_BATCH), lambda h: (h, 0, 0, 0)),
        ],
        out_shape=[
            jax.ShapeDtypeStruct((D_RWKV, DEC_BATCH), F32),
            jax.ShapeDtypeStruct((RWKV_HEADS, HEAD_DIM, HEAD_DIM, DEC_BATCH), F32),
        ],
        compiler_params=_params("parallel"),
        name="rwkv_step",
    )(kw_t, v_t, state_t)


ATT_PREP_TM = 640


def _swap_halves(x):
    lane = lax.broadcasted_iota(jnp.int32, x.shape, 1)
    first = (lane % HEAD_DIM) < HEAD_DIM // 2
    return jnp.where(first, pltpu.roll(x, LANES - HEAD_DIM // 2, axis=1), pltpu.roll(x, HEAD_DIM // 2, axis=1))


PAIRS = D_GROUP // LANES
QKV_TILES = 3 * PAIRS


def _att_prep_kernel(p_ref, cos_ref, sin_ref, qg_ref, kg_ref, e_ref, o0_ref, o1_ref, o2_ref):
    cos = cos_ref[...]
    sin = sin_ref[...]
    e = e_ref[...]
    o_refs = (o0_ref, o1_ref, o2_ref)
    for part, g_ref in ((0, qg_ref), (1, kg_ref)):
        x = p_ref[:, part * D_ATT:(part + 1) * D_ATT]
        ms = _segsum(x * x, e) * (1.0 / HEAD_DIM)
        xn = x * lax.rsqrt(ms + NORM_EPS) * g_ref[...]
        for c in range(D_ATT // LANES):
            xc = xn[:, c * LANES:(c + 1) * LANES]
            o_refs[c // PAIRS][part * PAIRS + c % PAIRS] = xc * cos + _swap_halves(xc) * sin
    for c in range(D_ATT // LANES):
        o_refs[c // PAIRS][2 * PAIRS + c % PAIRS] = p_ref[:, 2 * D_ATT + c * LANES:2 * D_ATT + (c + 1) * LANES]


def _att_prep(p_att, cos, sin, q_gain, k_gain, e_att):
    out = jax.ShapeDtypeStruct((QKV_TILES, M_ALL, LANES), F32)
    return pl.pallas_call(
        _att_prep_kernel,
        grid=(M_ALL // ATT_PREP_TM,),
        in_specs=[
            pl.BlockSpec((ATT_PREP_TM, 3 * D_ATT), lambda i: (i, 0)),
            pl.BlockSpec((ATT_PREP_TM, LANES), lambda i: (i, 0)),
            pl.BlockSpec((ATT_PREP_TM, LANES), lambda i: (i, 0)),
            pl.BlockSpec((1, D_ATT), lambda i: (0, 0)),
            pl.BlockSpec((1, D_ATT), lambda i: (0, 0)),
            _const_spec(e_att.shape),
        ],
        out_specs=[pl.BlockSpec((QKV_TILES, ATT_PREP_TM, LANES), lambda i: (0, i, 0))] * 3,
        out_shape=[out] * 3,
        compiler_params=_params("parallel"),
        name="att_prep",
    )(p_att, cos, sin, q_gain, k_gain, e_att)


ATT_TQ = 128


def _att_prompt_kernel(qkv_ref, o_ref, *, dil):
    scale = HEAD_DIM ** -0.5
    ls = SEQ // dil

    def heads(part, rows):
        pairs = [qkv_ref[part * PAIRS + pr, rows, :] for pr in range(PAIRS)]
        return jnp.stack([x[:, h * HEAD_DIM:(h + 1) * HEAD_DIM] for x in pairs for h in range(LANES // HEAD_DIM)],
                         axis=0)

    for c in range(dil):
        for qi in range(ls // ATT_TQ):
            rows = pl.ds(c + dil * qi * ATT_TQ, ATT_TQ, stride=dil)
            q = heads(0, rows).astype(BF16)
            k = heads(1, rows)
            v = heads(2, rows)
            q_off = 0
            if qi > 0:
                prev = pl.ds(c + dil * (qi - 1) * ATT_TQ, ATT_TQ, stride=dil)
                k = jnp.concatenate([heads(1, prev), k], axis=1)
                v = jnp.concatenate([heads(2, prev), v], axis=1)
                q_off = ATT_TQ
            s = jnp.einsum("hqd,hkd->hqk", q, k.astype(BF16), preferred_element_type=F32) * scale
            dist = (lax.broadcasted_iota(jnp.int32, s.shape, 1) + q_off) - lax.broadcasted_iota(jnp.int32, s.shape, 2)
            s = jnp.where((dist >= 0) & (dist <= N_KEYS_PAST), s, NEG_INF)
            m = jnp.max(s, axis=-1, keepdims=True)
            p = jnp.exp(s - m)
            l = jnp.sum(p, axis=-1, keepdims=True)
            lse = jnp.broadcast_to(m + jnp.log(l), (HEADS_PER_GROUP, ATT_TQ, HEAD_DIM))
            o = jnp.einsum("hqk,hkd->hqd", (p / l).astype(BF16), v.astype(BF16), preferred_element_type=F32)
            for h in range(HEADS_PER_GROUP):
                o_ref[h, rows, :] = jnp.concatenate([o[h], lse[h]], axis=1)


def _att_prompt(qkv, dil):
    return pl.pallas_call(
        functools.partial(_att_prompt_kernel, dil=dil),
        grid=(BATCH,),
        in_specs=[pl.BlockSpec((QKV_TILES, SEQ, LANES), lambda b: (0, b, 0))],
        out_specs=pl.BlockSpec((HEADS_PER_GROUP, SEQ, LANES), lambda b: (0, b, 0)),
        out_shape=jax.ShapeDtypeStruct((HEADS_PER_GROUP, M_PROMPT, LANES), F32),
        compiler_params=_params("parallel"),
        name=f"att_prompt_d{dil}",
    )(qkv)


CACHE_BATCH_PER_STEP = {128: 16, 512: 8, 2048: 2}


def _cache_attn_kernel(qkv_ref, c_ref, o_ref, cout_ref, *, dil, nb):
    n_tiles = c_ref.shape[-1] // LANES
    scale = HEAD_DIM ** -0.5
    r_i = lax.broadcasted_iota(jnp.int32, (HEAD_DIM, HEAD_DIM), 0)
    c_i = lax.broadcasted_iota(jnp.int32, (HEAD_DIM, HEAD_DIM), 1)
    eye = (r_i == c_i).astype(F32)
    on_grid = (lax.broadcasted_iota(jnp.int32, (1, LANES), 1) % dil) == 0
    last_lane = lax.broadcasted_iota(jnp.int32, (HEAD_DIM, LANES), 1) == LANES - 1

    def to_col(rows):
        return jnp.sum(eye * rows, axis=2, keepdims=True)

    rows = qkv_ref[...]
    parts = []
    for h in range(HEADS_PER_GROUP):
        q = rows[:, :, h * HEAD_DIM:(h + 1) * HEAD_DIM]
        k_new = rows[:, :, D_GROUP + h * HEAD_DIM:D_GROUP + (h + 1) * HEAD_DIM]
        v_new = rows[:, :, 2 * D_GROUP + h * HEAD_DIM:2 * D_GROUP + (h + 1) * HEAD_DIM]
        q_col, k_col, v_col = to_col(q), to_col(k_new), to_col(v_new)
        s_tiles = [None] * n_tiles
        nxt = jnp.broadcast_to(k_col, (nb, HEAD_DIM, LANES))
        for j in reversed(range(n_tiles)):
            ts = slice(j * LANES, (j + 1) * LANES)
            kt = c_ref[:, 0, h, :, ts]
            s = jnp.sum(kt * q_col, axis=1, keepdims=True) * scale
            s_tiles[j] = jnp.where(on_grid, s, NEG_INF)
            rolled = pltpu.roll(kt, LANES - 1, axis=2)
            cout_ref[:, 0, h, :, ts] = jnp.where(last_lane, nxt, rolled)
            nxt = rolled
        s_new = jnp.sum(k_new * q, axis=2, keepdims=True) * scale
        m_row = s_tiles[0]
        for j in range(1, n_tiles):
            m_row = jnp.maximum(m_row, s_tiles[j])
        m = jnp.maximum(jnp.max(m_row, axis=2, keepdims=True), s_new)
        p_new = jnp.exp(s_new - m)
        l_row = jnp.zeros((nb, 1, LANES), F32)
        acc = jnp.zeros((nb, HEAD_DIM, LANES), F32)
        nxt = jnp.broadcast_to(v_col, (nb, HEAD_DIM, LANES))
        for j in reversed(range(n_tiles)):
            ts = slice(j * LANES, (j + 1) * LANES)
            p = jnp.exp(s_tiles[j] - m)
            l_row = l_row + p
            vt = c_ref[:, 1, h, :, ts]
            acc = acc + vt * p
            rolled = pltpu.roll(vt, LANES - 1, axis=2)
            cout_ref[:, 1, h, :, ts] = jnp.where(last_lane, nxt, rolled)
            nxt = rolled
        l = jnp.sum(l_row, axis=2, keepdims=True) + p_new
        o_col = (jnp.sum(acc, axis=2, keepdims=True) + p_new * v_col) / l
        parts += [jnp.sum(eye * o_col, axis=1, keepdims=True), jnp.broadcast_to(m + jnp.log(l), (nb, 1, HEAD_DIM))]
    o_ref[...] = jnp.concatenate(parts, axis=2)


def _cache_attn(qkv, cache_t, dil):
    window = cache_t.shape[-1]
    nb = CACHE_BATCH_PER_STEP[window]
    blk = (nb, 2, HEADS_PER_GROUP, HEAD_DIM, window)
    return pl.pallas_call(
        functools.partial(_cache_attn_kernel, dil=dil, nb=nb),
        grid=(DEC_BATCH // nb,),
        in_specs=[pl.BlockSpec((nb, 1, 3 * D_GROUP), lambda i: (i, 0, 0)),
                  pl.BlockSpec(blk, lambda i: (i, 0, 0, 0, 0))],
        out_specs=[pl.BlockSpec((nb, 1, HEADS_PER_GROUP * LANES), lambda i: (i, 0, 0)),
                   pl.BlockSpec(blk, lambda i: (i, 0, 0, 0, 0))],
        out_shape=[jax.ShapeDtypeStruct((DEC_BATCH, 1, HEADS_PER_GROUP * LANES), F32),
                   jax.ShapeDtypeStruct(cache_t.shape, F32)],
        compiler_params=_params("parallel"),
        name=f"cache_attn_w{window}",
    )(qkv, cache_t)


MERGE_TM = 208


def _merge_kernel(x_ref, y_ref, bonus_ref, g_ref, a0_ref, a1_ref, a2_ref, pg_ref, bg_ref,
                  lnw_ref, lnb_ref, er_ref, eb_ref, wbr_ref, wba0_ref, wba1_ref, wba2_ref, wo_ref, o_ref):
    e = (er_ref[...], eb_ref[...])
    inv_n = 1.0 / HEAD_DIM
    y = y_ref[...]
    d = y - _segsum(y, e) * inv_n
    var = _segsum(d * d, e) * inv_n
    yn = d * lax.rsqrt(var + RWKV_LN_EPS) * lnw_ref[...] + lnb_ref[...]
    o_a = ((yn + bonus_ref[...].astype(F32)) * g_ref[...].astype(F32)).astype(BF16)
    br_a = jnp.dot(o_a, wbr_ref[...], preferred_element_type=F32)
    a = (a0_ref[...], a1_ref[...], a2_ref[...])
    m = jnp.maximum(jnp.maximum(a[0], a[1]), a[2])
    ex = [jnp.exp(t - m) for t in a]
    inv = 1.0 / (ex[0] + ex[1] + ex[2])
    br_b = jnp.zeros_like(br_a)
    for grp, wb_ref in enumerate((wba0_ref, wba1_ref, wba2_ref)):
        alpha = ex[grp] * inv
        parts = [a[grp][h] * pltpu.roll(alpha[h], HEAD_DIM, axis=1) for h in range(HEADS_PER_GROUP)]
        xg = jnp.concatenate(parts, axis=1).astype(BF16)
        br_b = br_b + jnp.dot(xg, wb_ref[...], preferred_element_type=F32)
    gates = jax.nn.sigmoid(pg_ref[...].astype(F32) + bg_ref[...])
    merged = (gates[:, 0:D_MODEL] * br_a + gates[:, D_MODEL:] * br_b).astype(BF16)
    o_ref[...] = x_ref[...] + jnp.dot(merged, wo_ref[...], preferred_element_type=F32)


def _merge(x, y, bonus, g, att, p_gate, b_gate, ln_w, ln_b, e_heads, w_br, w_ba, w_o):
    tile = lambda n: pl.BlockSpec((MERGE_TM, n), lambda i: (i, 0))
    row = lambda n: pl.BlockSpec((1, n), lambda i: (0, 0))
    att_tile = pl.BlockSpec((HEADS_PER_GROUP, MERGE_TM, LANES), lambda i: (0, i, 0))
    return pl.pallas_call(
        _merge_kernel,
        grid=(M_ALL // MERGE_TM,),
        in_specs=[tile(D_MODEL), tile(D_RWKV), tile(D_RWKV), tile(D_RWKV),
                  att_tile, att_tile, att_tile, tile(2 * D_MODEL), row(2 * D_MODEL),
                  row(D_RWKV), row(D_RWKV), _const_spec(e_heads[0].shape), _const_spec(e_heads[1].shape),
                  _const_spec(w_br.shape),
                  _const_spec(w_ba[0].shape), _const_spec(w_ba[1].shape), _const_spec(w_ba[2].shape),
                  _const_spec(w_o.shape)],
        out_specs=tile(D_MODEL),
        out_shape=jax.ShapeDtypeStruct((M_ALL, D_MODEL), F32),
        compiler_params=_params("parallel"),
        name="merge",
    )(x, y, bonus, g, att[0], att[1], att[2], p_gate, b_gate, ln_w, ln_b, *e_heads, w_br,
      w_ba[0], w_ba[1], w_ba[2], w_o)


def _split_bf16(w):
    hi = w.astype(BF16)
    return hi, (w - hi.astype(F32)).astype(BF16)


def _head_ones(n_heads):
    reduce = (jnp.arange(n_heads * HEAD_DIM)[:, None] // HEAD_DIM == jnp.arange(LANES)[None, :]).astype(BF16)
    return reduce, reduce.T


def _rotary_tables():
    half = HEAD_DIM // 2
    inv_freq = ROPE_THETA ** (-jnp.arange(half, dtype=F32) / half)
    pos = jnp.concatenate([jnp.tile(jnp.arange(SEQ), BATCH), jnp.full((DEC_BATCH,), PAST_LEN)]).astype(F32)
    ang = pos[:, None] * inv_freq[None, :]
    cos, sin = jnp.cos(ang), jnp.sin(ang)
    cos = jnp.concatenate([cos, cos] * (LANES // HEAD_DIM), axis=1)
    sin = jnp.concatenate([-sin, sin] * (LANES // HEAD_DIM), axis=1)
    return cos, sin


def kernel(x_prompt, x_sample, state_rwkv_shift, state_rwkv_wkv, cache_att_w128, cache_att_w512, cache_att_w2048,
           ffn1_norm, ffn1_w_in, ffn1_w_out, mix_norm, w_in, b_gate, rwkv_mu, rwkv_w0, rwkv_w2, rwkv_a0, rwkv_a2,
           rwkv_g2, rwkv_k_k, rwkv_k_a, rwkv_r_k, rwkv_ln_w, rwkv_ln_b, w_branch_rwkv, attn_q_norm, attn_k_norm,
           w_branch_attn, w_out, ffn2_norm, ffn2_w_in, ffn2_w_out):
    caches = (cache_att_w128, cache_att_w512, cache_att_w2048)

    w_mix = w_in[0]
    pad = RWKV_PROJ_PAD - D_RWKV_PROJ
    w_rwkv = jnp.pad(w_mix[:, :D_RWKV_PROJ], ((0, 0), (0, pad))).astype(BF16)
    w_att = w_mix[:, D_RWKV_PROJ:D_RWKV_PROJ + 3 * D_ATT].astype(BF16)
    w_gate = w_mix[:, D_RWKV_PROJ + 3 * D_ATT:].astype(BF16)
    mu = jnp.pad(rwkv_mu, ((0, 0), (0, pad)))
    state_shift = jnp.pad(state_rwkv_shift[0], ((0, 0), (0, pad)))
    zeros = jnp.zeros((D_DECAY_LORA, D_RWKV), F32)
    w_lora = jnp.concatenate([jnp.concatenate([rwkv_w2[0], zeros], axis=1),
                              jnp.concatenate([zeros, rwkv_a2[0]], axis=1)], axis=0)
    w_glora = jnp.pad(rwkv_g2[0], ((0, 2 * LANES - D_GATE_LORA), (0, 0)))
    wl_hi, wl_lo = _split_bf16(w_lora)
    wg_hi, wg_lo = _split_bf16(w_glora)
    e_heads = _head_ones(RWKV_HEADS)
    e_att = jnp.kron(jnp.eye(ATT_HEADS, dtype=F32), jnp.ones((HEAD_DIM, HEAD_DIM), F32)).astype(BF16)
    cos, sin = _rotary_tables()
    q_gain = jnp.tile(attn_q_norm, (1, ATT_HEADS))
    k_gain = jnp.tile(attn_k_norm, (1, ATT_HEADS))
    wba = w_branch_attn[0].reshape(len(ATT_GROUPS), HEADS_PER_GROUP, HEAD_DIM, D_MODEL)
    wba = jnp.pad(wba, ((0, 0), (0, 0), (0, LANES - HEAD_DIM), (0, 0))).astype(BF16)
    wba = wba.reshape(len(ATT_GROUPS), HEADS_PER_GROUP * LANES, D_MODEL)

    x1, h_mix = _ffn(x_prompt.reshape(M_PROMPT, D_MODEL), ffn1_norm, ffn1_w_in[0].astype(BF16),
                     ffn1_w_out[0].astype(BF16), x_sample=x_sample.reshape(DEC_BATCH, D_MODEL), next_norm=mix_norm)

    p_rwkv = _proj(h_mix, w_rwkv, 512, "proj_rwkv")
    kw, v_r, kw_s_t, v_s_t, bonus, g_r = _rwkv_prep(p_rwkv, state_shift, mu, rwkv_w0, rwkv_a0, rwkv_k_k, rwkv_k_a,
                                                    rwkv_r_k.reshape(1, D_RWKV), wl_hi, wl_lo, wg_hi, wg_lo, e_heads)
    kw, v_r, h_att = lax.optimization_barrier((kw, v_r, h_mix))
    kw_cm = jnp.swapaxes(kw.reshape(5, BATCH, SEQ, D_RWKV), 2, 3)
    v_cm = jnp.swapaxes(v_r.reshape(BATCH, SEQ, D_RWKV), 1, 2)

    p_att = _proj(h_att, w_att, 768, "proj_att")
    p_gate = _proj(h_att, w_gate, 512, "proj_gate", out_dtype=BF16)
    kw_cm, v_cm, p_att, p_gate = lax.optimization_barrier((kw_cm, v_cm, p_att, p_gate))
    v_rows = v_cm.reshape(BATCH, RWKV_HEADS, HEAD_DIM, SEQ).transpose(2, 0, 1, 3)
    v_rows = jnp.broadcast_to(v_rows[:, None], (HEAD_DIM, 2, BATCH, RWKV_HEADS, SEQ))
    kw_rows = kw_cm.reshape(5, BATCH, RWKV_HEADS, 2, K_HALF, SEQ).transpose(0, 4, 3, 1, 2, 5)
    qkv = _att_prep(p_att, cos, sin, q_gain, k_gain, e_att)
    kw_rows, v_rows, qkv = lax.optimization_barrier(
        (kw_rows.reshape(5 * K_HALF * LANES, SEQ), v_rows.reshape(HEAD_DIM * LANES, SEQ), qkv))
    v_t = v_rows.T.reshape(SEQ, HEAD_DIM, LANES)
    kw_t = kw_rows.T.reshape(SEQ, 5, K_HALF, LANES)
    att, kv_p, kv_s = [], [], []
    for g, (window, dil) in enumerate(ATT_GROUPS):
        att_p = _att_prompt(qkv[g], dil)
        keep = min(window, SEQ)
        kv = qkv[g][PAIRS:, :M_PROMPT].reshape(2, PAIRS, BATCH, SEQ, LANES // HEAD_DIM, HEAD_DIM)
        kv = kv.transpose(2, 3, 0, 1, 4, 5).reshape(BATCH, SEQ, 2, HEADS_PER_GROUP, HEAD_DIM)
        kv_p.append(kv[:, SEQ - keep:][None])
        qkv_s = qkv[g][:, M_PROMPT:].transpose(1, 0, 2).reshape(DEC_BATCH, 1, 3 * D_GROUP)
        o_s, cache_new = _cache_attn(qkv_s, caches[g][0].transpose(0, 2, 3, 4, 1), dil)
        att_s = o_s.reshape(DEC_BATCH, HEADS_PER_GROUP, LANES).transpose(1, 0, 2)
        att.append(jnp.concatenate([att_p, att_s], axis=1))
        kv_s.append(cache_new.transpose(0, 4, 1, 2, 3)[None])

    kw_t, v_t, att = lax.optimization_barrier((kw_t, v_t, att))
    y_t, s_t = _rwkv_seq(kw_t, v_t)
    wkv_p = s_t.reshape(K_HALF, HEAD_DIM, 2, BATCH, RWKV_HEADS).transpose(3, 4, 1, 2, 0)
    wkv_p = wkv_p.reshape(BATCH, RWKV_HEADS, HEAD_DIM, HEAD_DIM)
    y_t, w2_in, w2_out = lax.optimization_barrier((y_t, ffn2_w_in, ffn2_w_out))
    y_cm = y_t.reshape(SEQ, HEAD_DIM * LANES).T
    y_s_t, wkv_s_t = _rwkv_step(kw_s_t, v_s_t, state_rwkv_wkv[0].transpose(1, 2, 3, 0))
    y_cm, y_s_t, wkv_s_t = lax.optimization_barrier((y_cm, y_s_t, wkv_s_t))
    wkv_s = wkv_s_t.transpose(3, 0, 1, 2)
    y_cm = y_cm.reshape(HEAD_DIM, 2, BATCH, RWKV_HEADS, SEQ)[:, 0].transpose(1, 2, 0, 3)
    w2_in_bf, w2_out_bf = w2_in[0].astype(BF16), w2_out[0].astype(BF16)
    y_cm, w2_in_bf, w2_out_bf = lax.optimization_barrier((y_cm.reshape(BATCH, D_RWKV, SEQ), w2_in_bf, w2_out_bf))
    y_p = jnp.swapaxes(y_cm, 1, 2).reshape(M_PROMPT, D_RWKV)
    y_r = jnp.concatenate([y_p, y_s_t.T], axis=0)

    x2 = _merge(x1, y_r, bonus, g_r, att, p_gate, b_gate, rwkv_ln_w, rwkv_ln_b,
                e_heads, w_branch_rwkv[0].astype(BF16), wba, w_out[0].astype(BF16))
    y_prompt, y_sample = _ffn(x2, ffn2_norm, w2_in_bf, w2_out_bf, split_out=True)

    shift_p = jnp.concatenate([p_rwkv[(b + 1) * SEQ - 1:(b + 1) * SEQ, :D_RWKV_PROJ] for b in range(BATCH)])[None]
    shift_s = p_rwkv[M_PROMPT:, :D_RWKV_PROJ][None]
    return (y_prompt.reshape(BATCH, SEQ, D_MODEL), y_sample.reshape(DEC_BATCH, 1, D_MODEL),
            shift_p, wkv_p[None], kv_p[0], kv_p[1], kv_p[2],
            shift_s, wkv_s[None], kv_s[0], kv_s[1], kv_s[2])
```
